```python
import jax, jax.numpy as jnp
from jax import lax
import numpy as np

D_MODEL = 1024
BATCH = 4
SEQ = 4096
DEPTH = 2

CHUNK = 64
EPS = 1e-6
NEG_BIG = -1e30
D_MIX = D_MODEL
A_HEADS = 8
A_KV_HEADS = 2
HEAD_DIM = 64
IDX_HEADS = 4
IDX_DIM = 64
IDX_TOPK_MAX = 256
Q_BLOCK = 128
ROPE_THETA = 10000.0
A_WIDTH = A_HEADS * HEAD_DIM
POOL_WINDOWS = (2, 4, 8, 16)
POOL_GROUP = 64
B_WIDTH = len(POOL_WINDOWS) * POOL_GROUP
C_HEADS = 4
C_KDIM = 64
C_VDIM = 64
C_WIDTH = C_HEADS * C_VDIM
N_EXPERTS = 32
TOP_K = 4
D_FF = D_MODEL
SWIGLU_LIMIT = 7.0
SWIGLU_ALPHA = 1.702
MOE_BLOCK = 128
IN_SPLITS = (A_WIDTH, A_KV_HEADS * HEAD_DIM, A_KV_HEADS * HEAD_DIM,
             IDX_HEADS * IDX_DIM, IDX_DIM, IDX_HEADS,
             B_WIDTH,
             C_HEADS * C_KDIM, C_HEADS * C_KDIM, C_WIDTH, C_WIDTH)
D_IN = sum(IN_SPLITS)

kernel_name = "chunk_causal_hybrid_dsa_pool_hgrn2_moe"


def rms_norm(x, g):
    xf = x.astype(jnp.float32)
    y = xf * lax.rsqrt(jnp.mean(xf * xf, axis=-1, keepdims=True) + EPS)
    return (y * g.astype(jnp.float32)).astype(x.dtype)


def rope(x, pos):
    d = x.shape[-1]
    inv = jnp.power(jnp.float32(ROPE_THETA), -jnp.arange(0, d, 2, dtype=jnp.float32) / d)
    ang = pos.astype(jnp.float32)[..., None] * inv
    cos = jnp.cos(ang)[:, :, None, :]
    sin = jnp.sin(ang)[:, :, None, :]
    xf = x.astype(jnp.float32)
    x1, x2 = xf[..., : d // 2], xf[..., d // 2:]
    return jnp.concatenate([x1 * cos - x2 * sin, x2 * cos + x1 * sin], axis=-1).astype(x.dtype)


def split_cols(z):
    out, off = [], 0
    for n in IN_SPLITS:
        out.append(z[..., off:off + n])
        off += n
    return out


def dsa_mixer(q, k, v, qi, ki, wi, pos):
    B_, S_, _ = q.shape
    q = rope(q.reshape(B_, S_, A_HEADS, HEAD_DIM), pos)
    k = rope(k.reshape(B_, S_, A_KV_HEADS, HEAD_DIM), pos)
    v = v.reshape(B_, S_, A_KV_HEADS, HEAD_DIM)
    qi = rope(qi.reshape(B_, S_, IDX_HEADS, IDX_DIM), pos).astype(jnp.float32)
    ki = rope(ki.reshape(B_, S_, 1, IDX_DIM), pos)[:, :, 0].astype(jnp.float32)
    wi = wi.astype(jnp.float32) * (IDX_HEADS * IDX_DIM) ** -0.5
    topk = min(IDX_TOPK_MAX, S_ // 4)
    group = A_HEADS // A_KV_HEADS
    key_pos = jnp.arange(S_)
    n_blocks = S_ // Q_BLOCK

    def block(i):
        t0 = i * Q_BLOCK
        qb = lax.dynamic_slice_in_dim(q, t0, Q_BLOCK, axis=1)
        qib = lax.dynamic_slice_in_dim(qi, t0, Q_BLOCK, axis=1)
        wib = lax.dynamic_slice_in_dim(wi, t0, Q_BLOCK, axis=1)
        t_idx = t0 + jnp.arange(Q_BLOCK)
        limit = (t_idx // CHUNK + 1) * CHUNK
        admissible = key_pos[None, :] < limit[:, None]
        s = jnp.einsum('bthd,bsd->bths', qib, ki)
        score = jnp.einsum('bths,bth->bts', jax.nn.relu(s), wib)
        score = jnp.where(admissible[None], score, NEG_BIG)
        _, sel = lax.top_k(score, topk)
        valid = sel < limit[None, :, None]
        k_sel = jax.vmap(lambda kb, ib: kb[ib])(k, sel)
        v_sel = jax.vmap(lambda vb, ib: vb[ib])(v, sel)
        qg = qb.reshape(B_, Q_BLOCK, A_KV_HEADS, group, HEAD_DIM).astype(jnp.float32)
        logits = jnp.einsum('btngd,btknd->btngk', qg, k_sel.astype(jnp.float32)) * HEAD_DIM ** -0.5
        logits = jnp.where(valid[:, :, None, None, :], logits, NEG_BIG)
        p = jax.nn.softmax(logits, axis=-1)
        o = jnp.einsum('btngk,btknd->btngd', p, v_sel.astype(jnp.float32))
        return o.reshape(B_, Q_BLOCK, A_WIDTH).astype(q.dtype)

    out = lax.map(block, jnp.arange(n_blocks))
    return out.transpose(1, 0, 2, 3).reshape(B_, S_, A_WIDTH)


def pool_mixer(u, w_pool, pool_scale):
    B_, S_, _ = u.shape
    uf = u.astype(jnp.float32).reshape(B_, S_, len(POOL_WINDOWS), POOL_GROUP)
    cs = jnp.cumsum(uf, axis=1)
    outs = []
    for g, w in enumerate(POOL_WINDOWS):
        cg = cs[:, :, g]
        prev = jnp.pad(cg, ((0, 0), (w, 0), (0, 0)))[:, :S_]
        cnt = jnp.minimum(jnp.arange(1, S_ + 1), w).astype(jnp.float32)[None, :, None]
        outs.append((cg - prev) / cnt - uf[:, :, g])
    pooled = jnp.stack(outs, axis=2)
    y = jnp.einsum('bsgc,gce->bsge', pooled, w_pool.astype(jnp.float32))
    return (y.reshape(B_, S_, B_WIDTH) * pool_scale.astype(jnp.float32)).astype(u.dtype)


def hgrn2_mixer(q, fz, i, g, lb, norm_g):
    B_, S_, _ = q.shape
    nC = S_ // CHUNK
    zf = fz.astype(jnp.float32)
    lbf = lb.astype(jnp.float32)
    f = lbf + (1.0 - lbf) * jax.nn.sigmoid(zf)
    log_f = jnp.log(f)
    k_in = (1.0 - lbf) * jax.nn.sigmoid(-zf)

    def heads(a, d):
        return a.astype(jnp.float32).reshape(B_, nC, CHUNK, C_HEADS, d).transpose(1, 0, 3, 2, 4)

    qh = heads(jax.nn.silu(q.astype(jnp.float32)), C_KDIM)
    lfh = heads(log_f, C_KDIM)
    kh = heads(k_in, C_KDIM)
    vh = heads(i, C_VDIM)
    causal = jnp.tril(jnp.ones((CHUNK, CHUNK), dtype=bool))

    def step(state, xs):
        qc, lfc, kc, vc = xs
        b = jnp.cumsum(lfc, axis=2)
        o_inter = jnp.einsum('bhtk,bhkv->bhtv', qc * jnp.exp(b), state)
        diff = jnp.where(causal[None, None, :, :, None],
                         b[:, :, :, None, :] - b[:, :, None, :, :], NEG_BIG)
        attn = jnp.einsum('bhtk,bhtsk,bhsk->bhts', qc, jnp.exp(diff), kc)
        o = o_inter + jnp.einsum('bhts,bhsv->bhtv', attn, vc)
        b_last = b[:, :, -1, :]
        state = jnp.exp(b_last)[..., None] * state + jnp.einsum(
            'bhsk,bhsv->bhkv', kc * jnp.exp(b_last[:, :, None, :] - b), vc)
        return state, o

    s0 = jnp.zeros((B_, C_HEADS, C_KDIM, C_VDIM), jnp.float32)
    _, o = lax.scan(step, s0, (qh, lfh, kh, vh))
    o = o.transpose(1, 0, 3, 2, 4).reshape(B_, S_, C_HEADS, C_VDIM)
    o = rms_norm(o, norm_g).reshape(B_, S_, C_WIDTH) * jax.nn.silu(g.astype(jnp.float32))
    return o.astype(q.dtype)


def token_mix(u, pos, w_in, w_out, pool_w, pool_scale, hgrn_g, lb):
    z = u @ w_in
    qa, ka, va, qi, ki, wi, ub, qc, fc, ic, gc = split_cols(z)
    ya = dsa_mixer(qa, ka, va, qi, ki, wi, pos)
    yb = pool_mixer(ub, pool_w, pool_scale)
    yc = hgrn2_mixer(qc, fc, ic, gc, lb, hgrn_g)
    return jnp.concatenate([ya, yb, yc], axis=-1) @ w_out


def moe(h, w_router, b_router, w1, b1, w2, b2):
    B_, S_, D_ = h.shape
    N = B_ * S_
    xf = h.reshape(N, D_)
    logits = xf.astype(jnp.float32) @ w_router.astype(jnp.float32) + b_router.astype(jnp.float32)
    top_val, top_idx = lax.top_k(logits, TOP_K)
    gates = jax.nn.softmax(top_val, axis=-1)
    A = N * TOP_K
    e_flat = top_idx.reshape(A)
    order = jnp.argsort(e_flat)
    e_sorted = e_flat[order]
    tok_sorted = order // TOP_K
    gate_sorted = gates.reshape(A)[order]
    counts = jnp.bincount(e_flat, length=N_EXPERTS)
    starts = jnp.cumsum(counts) - counts
    padded = (counts + MOE_BLOCK - 1) // MOE_BLOCK * MOE_BLOCK
    pad_ends = jnp.cumsum(padded)
    pad_starts = pad_ends - padded
    dest = pad_starts[e_sorted] + jnp.arange(A) - starts[e_sorted]
    nb = -(-A // MOE_BLOCK) + N_EXPERTS
    x_pad = jnp.zeros((nb * MOE_BLOCK, D_), h.dtype).at[dest].set(xf[tok_sorted])
    blk_e = jnp.minimum(jnp.searchsorted(pad_ends, jnp.arange(nb) * MOE_BLOCK, side='right'),
                        N_EXPERTS - 1)

    def expert_block(args):
        xb, e = args
        hid = xb @ w1[e] + b1[e]
        x_glu = jnp.minimum(hid[:, 0::2], SWIGLU_LIMIT)
        x_lin = jnp.clip(hid[:, 1::2], -SWIGLU_LIMIT, SWIGLU_LIMIT)
        act = x_glu * jax.nn.sigmoid(SWIGLU_ALPHA * x_glu) * (x_lin + 1.0)
        return act @ w2[e] + b2[e]

    y_pad = lax.map(expert_block, (x_pad.reshape(nb, MOE_BLOCK, D_), blk_e))
    y_sorted = y_pad.reshape(nb * MOE_BLOCK, D_)[dest] * gate_sorted[:, None].astype(h.dtype)
    y = jnp.zeros((N, D_), h.dtype).at[tok_sorted].add(y_sorted)
    return y.reshape(B_, S_, D_)


def setup_inputs(seed: int = 0) -> dict:
    key = jax.random.key(seed)
    ks = jax.random.split(key, 24)
    f32 = jnp.float32
    nrm = lambda k, shape, s: jax.random.normal(k, shape, f32) * s
    x = jax.random.normal(ks[0], (BATCH, SEQ, D_MODEL), f32)
    c = jax.random.normal(ks[1], (BATCH, D_MODEL), f32)
    offset = jax.random.randint(ks[2], (BATCH, 1), 0, 4096, dtype=jnp.int32)
    positions = offset + jnp.arange(SEQ, dtype=jnp.int32)[None, :]
    return {
        "x": x,
        "c": c,
        "positions": positions,
        "w_ada": nrm(ks[3], (DEPTH, D_MODEL, 6 * D_MODEL), 0.5 * D_MODEL ** -0.5),
        "b_ada": nrm(ks[4], (DEPTH, 6 * D_MODEL), 0.02),
        "norm1_g": 1.0 + nrm(ks[5], (DEPTH, D_MODEL), 0.1),
        "norm2_g": 1.0 + nrm(ks[6], (DEPTH, D_MODEL), 0.1),
        "w_in": nrm(ks[7], (DEPTH, D_MODEL, D_IN), D_MODEL ** -0.5),
        "w_out": nrm(ks[8], (DEPTH, D_MIX, D_MODEL), D_MIX ** -0.5),
        "pool_w": nrm(ks[9], (DEPTH, len(POOL_WINDOWS), POOL_GROUP, POOL_GROUP), POOL_GROUP ** -0.5),
        "pool_scale": 1.0 + nrm(ks[10], (DEPTH, B_WIDTH), 0.1),
        "hgrn_norm_g": 1.0 + nrm(ks[11], (DEPTH, C_VDIM), 0.1),
        "lb_logits": nrm(ks[12], (DEPTH, C_HEADS * C_KDIM), 1.0),
        "w_router": nrm(ks[13], (DEPTH, D_MODEL, N_EXPERTS), D_MODEL ** -0.5),
        "b_router": nrm(ks[14], (DEPTH, N_EXPERTS), 0.01),
        "w1": nrm(ks[15], (DEPTH, N_EXPERTS, D_MODEL, 2 * D_FF), D_MODEL ** -0.5),
        "b1": nrm(ks[16], (DEPTH, N_EXPERTS, 2 * D_FF), 0.01),
        "w2": nrm(ks[17], (DEPTH, N_EXPERTS, D_FF, D_MODEL), D_FF ** -0.5),
        "b2": nrm(ks[18], (DEPTH, N_EXPERTS, D_MODEL), 0.01),
        "final_g": 1.0 + nrm(ks[19], (D_MODEL,), 0.1),
    }


def reference(x, c, positions, w_ada, b_ada, norm1_g, norm2_g, w_in, w_out, pool_w,
              pool_scale, hgrn_norm_g, lb_logits, w_router, b_router, w1, b1, w2, b2, final_g):
    p = jax.nn.softmax(lb_logits.astype(jnp.float32), axis=0)
    lower_bounds = jnp.clip(jnp.cumsum(p, axis=0) - p[0], 0.0, 1.0)
    c_act = jax.nn.silu(c)
    h = x
    for l in range(DEPTH):
        mod = c_act @ w_ada[l] + b_ada[l]
        sh1, sc1, g1, sh2, sc2, g2 = [m[:, None, :] for m in jnp.split(mod, 6, axis=-1)]
        u = rms_norm(h, norm1_g[l]) * (1.0 + sc1) + sh1
        h = h + g1 * token_mix(u, positions, w_in[l], w_out[l], pool_w[l], pool_scale[l],
                               hgrn_norm_g[l], lower_bounds[l])
        u = rms_norm(h, norm2_g[l]) * (1.0 + sc2) + sh2
        h = h + g2 * moe(u, w_router[l], b_router[l], w1[l], b1[l], w2[l], b2[l])
    return rms_norm(h, final_g)
```

```python
import functools
import math

import numpy as np
import jax
import jax.numpy as jnp
from jax import lax
from jax.experimental import pallas as pl
from jax.experimental.pallas import tpu as pltpu

F32 = jnp.float32
BF16 = jnp.bfloat16
I32 = jnp.int32

CHUNK = 64
EPS = 1e-6
NEG_BIG = -1e30
A_HEADS, A_KV_HEADS, HEAD_DIM = 8, 2, 64
IDX_HEADS, IDX_DIM, IDX_TOPK_MAX = 4, 64, 256
Q_BLOCK = 128
ROPE_THETA = 10000.0
A_WIDTH = A_HEADS * HEAD_DIM
POOL_WINDOWS = (2, 4, 8, 16)
POOL_GROUP = 64
B_WIDTH = len(POOL_WINDOWS) * POOL_GROUP
C_HEADS, C_KDIM, C_VDIM = 4, 64, 64
C_WIDTH = C_HEADS * C_VDIM
N_EXPERTS, TOP_K = 32, 4
SWIGLU_LIMIT, SWIGLU_ALPHA = 7.0, 1.702
KV_WIDTH = A_KV_HEADS * HEAD_DIM
IN_SPLITS = (A_WIDTH, KV_WIDTH, KV_WIDTH, IDX_HEADS * IDX_DIM, IDX_DIM, IDX_HEADS,
             B_WIDTH, C_HEADS * C_KDIM, C_HEADS * C_KDIM, C_WIDTH, C_WIDTH)

LANES = 128
INT_MIN = -(2 ** 31)
VMEM_LIMIT = 56 * 1024 * 1024

KIWI_W = LANES
REST_W = B_WIDTH + 4 * C_WIDTH
PACK_W = A_WIDTH + 2 * KV_WIDTH + IDX_HEADS * IDX_DIM + KIWI_W + REST_W
HGRN_LEVELS = (32, 16, 8, 4, 2, 1)


def _nt(a, b):
    return lax.dot_general(a, b, (((1,), (1,)), ((), ())), preferred_element_type=F32)


def _tn(a, b):
    return lax.dot_general(a, b, (((0,), (0,)), ((), ())), preferred_element_type=F32)


def _mm(a, b):
    return jnp.dot(a, b, preferred_element_type=F32)


def _cparams(sem):
    return pltpu.CompilerParams(dimension_semantics=sem, vmem_limit_bytes=VMEM_LIMIT)


def _ada_kernel(c_ref, w_ref, b_ref, o_ref):
    c = c_ref[...]
    ca = c * jax.nn.sigmoid(c)
    o_ref[...] = jnp.dot(ca, w_ref[...], precision=lax.Precision.HIGHEST,
                         preferred_element_type=F32) + b_ref[...]


def _ada_call(c, w_ada, b_ada):
    depth, d, d6 = w_ada.shape
    b = c.shape[0]
    nblk = d6 // d
    return pl.pallas_call(
        _ada_kernel,
        out_shape=jax.ShapeDtypeStruct((depth, b, d6), F32),
        grid=(depth, nblk),
        in_specs=[pl.BlockSpec((b, d), lambda l, j: (0, 0)),
                  pl.BlockSpec((None, d, d), lambda l, j: (l, 0, j)),
                  pl.BlockSpec((None, 1, d), lambda l, j: (l, 0, j))],
        out_specs=pl.BlockSpec((None, b, d), lambda l, j: (l, 0, j)),
        compiler_params=_cparams(("arbitrary", "arbitrary")),
        name="ada_mod",
    )(c, w_ada, b_ada.reshape(depth, 1, d6))


def _rope_table_kernel(pos_ref, inv_ref, sign_ref, cos_ref, sin_ref):
    ang = pos_ref[...].astype(F32) * inv_ref[...]
    cos_ref[...] = jnp.cos(ang)
    sin_ref[...] = jnp.sin(ang) * sign_ref[...]


def _rope_table_call(positions, ts=512):
    b, s = positions.shape
    half = HEAD_DIM // 2
    inv = jnp.power(jnp.float32(ROPE_THETA), -jnp.arange(0, HEAD_DIM, 2, dtype=F32) / HEAD_DIM)
    inv128 = jnp.tile(inv, LANES // half).reshape(1, LANES)
    sign128 = jnp.tile(jnp.concatenate([-jnp.ones((half,), F32), jnp.ones((half,), F32)]),
                       LANES // HEAD_DIM).reshape(1, LANES)
    spec = pl.BlockSpec((None, ts, LANES), lambda bi, i: (bi, i, 0))
    return pl.pallas_call(
        _rope_table_kernel,
        out_shape=(jax.ShapeDtypeStruct((b, s, LANES), F32),) * 2,
        grid=(b, s // ts),
        in_specs=[pl.BlockSpec((None, ts, 1), lambda bi, i: (bi, i, 0)),
                  pl.BlockSpec((1, LANES), lambda bi, i: (0, 0)),
                  pl.BlockSpec((1, LANES), lambda bi, i: (0, 0))],
        out_specs=(spec, spec),
        compiler_params=_cparams(("arbitrary", "arbitrary")),
        name="rope_table",
    )(positions.reshape(b, s, 1), inv128, sign128)


def _rope_tile(x, cos, sin_signed, first_half):
    partner = jnp.where(first_half, pltpu.roll(x, LANES - HEAD_DIM // 2, 1),
                        pltpu.roll(x, HEAD_DIM // 2, 1))
    return x * cos + partner * sin_signed


def _ada_norm(x, g, sc, sh):
    y = x * lax.rsqrt(jnp.mean(x * x, axis=-1, keepdims=True) + EPS)
    return (y * g) * (1.0 + sc) + sh


def _inproj_kernel(h_ref, sh_ref, sc_ref, g_ref, w_ref, cos_ref, sin_ref,
                   q_ref, k_ref, v_ref, qi_ref, kiwi_ref, rest_ref):
    u = _ada_norm(h_ref[...], g_ref[...], sc_ref[...], sh_ref[...]).astype(BF16)
    cos = cos_ref[...]
    sin = sin_ref[...]
    lane = lax.broadcasted_iota(I32, cos.shape, 1)
    first_half = (lane % HEAD_DIM) < (HEAD_DIM // 2)
    rope = lambda x: _rope_tile(x, cos, sin, first_half)

    off = 0
    q_scale = HEAD_DIM ** -0.5
    for j in range(A_WIDTH // LANES):
        z = _mm(u, w_ref[:, off + j * LANES: off + (j + 1) * LANES])
        q_ref[:, j * LANES:(j + 1) * LANES] = (rope(z) * q_scale).astype(BF16)
    off += A_WIDTH
    k_ref[...] = rope(_mm(u, w_ref[:, off:off + KV_WIDTH])).astype(BF16)
    off += KV_WIDTH
    v_ref[...] = _mm(u, w_ref[:, off:off + KV_WIDTH]).astype(BF16)
    off += KV_WIDTH
    for j in range(IDX_HEADS * IDX_DIM // LANES):
        z = _mm(u, w_ref[:, off + j * LANES: off + (j + 1) * LANES])
        qi_ref[:, j * LANES:(j + 1) * LANES] = rope(z)
    off += IDX_HEADS * IDX_DIM
    z = _mm(u, w_ref[:, off:off + KIWI_W])
    wi_scale = (IDX_HEADS * IDX_DIM) ** -0.5
    kiwi_ref[...] = jnp.where(lane < IDX_DIM, rope(z), z * wi_scale)
    off += KIWI_W
    rest_ref[...] = _mm(u, w_ref[:, off:off + REST_W])


def _inproj_call(h, sh, sc, g, w_pack, cos, sin, tm=256):
    b, s, d = h.shape
    tok = lambda w: pl.BlockSpec((None, tm, w), lambda bi, i: (bi, i, 0))
    per_b = pl.BlockSpec((None, 1, d), lambda bi, i: (bi, 0, 0))
    outs = [(A_WIDTH, BF16), (KV_WIDTH, BF16), (KV_WIDTH, BF16),
            (IDX_HEADS * IDX_DIM, F32), (KIWI_W, F32), (REST_W, F32)]
    return pl.pallas_call(
        _inproj_kernel,
        out_shape=tuple(jax.ShapeDtypeStruct((b, s, w), dt) for w, dt in outs),
        grid=(b, s // tm),
        in_specs=[tok(d), per_b, per_b,
                  pl.BlockSpec((1, d), lambda bi, i: (0, 0)),
                  pl.BlockSpec((d, PACK_W), lambda bi, i: (0, 0)),
                  tok(LANES), tok(LANES)],
        out_specs=tuple(tok(w) for w, _ in outs),
        compiler_params=_cparams(("arbitrary", "arbitrary")),
        name="inproj",
    )(h, sh, sc, g, w_pack, cos, sin)


def _dsa_kernel(q_ref, qi_ref, kwq_ref, k_ref, v_ref, kwk_ref, o_ref, keys_ref, bias_ref,
                *, kc, topk, idx_bits):
    qb = Q_BLOCK
    i = pl.program_id(1)
    t0 = i * qb
    nkc = (t0 + qb + kc - 1) // kc
    row = lax.broadcasted_iota(I32, (qb, 1), 0)
    limit = t0 + (row // CHUNK + 1) * CHUNK
    lane_kc = lax.broadcasted_iota(I32, (qb, kc), 1)
    lane_128 = lax.broadcasted_iota(I32, (qb, LANES), 1)
    group = A_HEADS // A_KV_HEADS

    qi = qi_ref[...]
    qi_stack = jnp.concatenate(
        [qi[:, h * IDX_DIM:(h + 1) * IDX_DIM] for h in range(IDX_HEADS)], axis=0).astype(BF16)
    wi = kwq_ref[:, IDX_DIM:IDX_DIM + IDX_HEADS]

    def score_body(c, carry):
        off = pl.multiple_of(c * kc, kc)
        ki = kwk_ref[pl.ds(off, kc), 0:IDX_DIM].astype(BF16)
        s = jnp.maximum(_nt(qi_stack, ki), 0.0)
        score = s[0:qb] * wi[:, 0:1]
        for h in range(1, IDX_HEADS):
            score = score + s[h * qb:(h + 1) * qb] * wi[:, h:h + 1]
        score = jnp.where(score == 0.0, 0.0, score)
        bits = lax.bitcast_convert_type(score, I32)
        key = jnp.where(bits < 0, bits ^ jnp.int32(0x7FFFFFFF), bits)
        keys_ref[c] = jnp.where(off + lane_kc < limit, key, jnp.int32(INT_MIN))
        return carry

    lax.fori_loop(0, nkc, score_body, 0)

    def count(pred):
        def body(c, acc):
            for j in range(kc // LANES):
                kk = keys_ref[c, :, j * LANES:(j + 1) * LANES]
                kidx = c * kc + j * LANES + lane_128
                acc = acc + jnp.where(pred(kk, kidx), 1.0, 0.0)
            return acc
        acc = lax.fori_loop(0, nkc, body, jnp.zeros((qb, LANES), F32))
        return jnp.sum(acc, axis=1, keepdims=True)

    def thr_body(j, thr):
        cand = thr + jnp.left_shift(jnp.int32(1), 31 - j)
        cnt = count(lambda kk, _: kk >= cand)
        return jnp.where(cnt >= topk, cand, thr)

    thr = lax.fori_loop(0, 32, thr_body, jnp.full((qb, 1), INT_MIN, I32))
    need = topk - count(lambda kk, _: kk > thr)

    def tie_body(j, jmax):
        cand = jmax + jnp.left_shift(jnp.int32(1), idx_bits - 1 - j)
        cnt = count(lambda kk, kidx: (kk == thr) & (kidx < cand))
        return jnp.where(cnt < need, cand, jmax)

    jmax = lax.fori_loop(0, idx_bits, tie_body, jnp.zeros((qb, 1), I32))

    def bias_body(c, carry):
        kk = keys_ref[c]
        kidx = c * kc + lane_kc
        sel = (kk > thr) | ((kk == thr) & (kidx <= jmax))
        bias_ref[c] = jnp.where(sel & (kidx < limit), 0.0, NEG_BIG)
        return carry

    lax.fori_loop(0, nkc, bias_body, 0)

    q = q_ref[...]
    for n in range(A_KV_HEADS):
        qn = jnp.concatenate(
            [q[:, (n * group + g) * HEAD_DIM:(n * group + g + 1) * HEAD_DIM] for g in range(group)],
            axis=0)

        def attn_body(c, carry, n=n, qn=qn):
            m, l, acc = carry
            off = pl.multiple_of(c * kc, kc)
            kch = k_ref[pl.ds(off, kc), n * HEAD_DIM:(n + 1) * HEAD_DIM]
            vch = v_ref[pl.ds(off, kc), n * HEAD_DIM:(n + 1) * HEAD_DIM]
            s = _nt(qn, kch).reshape(group, qb, kc) + bias_ref[c][None]
            s = s.reshape(group * qb, kc)
            m_new = jnp.maximum(m, jnp.max(s, axis=1, keepdims=True))
            alpha = jnp.exp(m - m_new)
            p = jnp.exp(s - m_new)
            l = alpha * l + jnp.sum(p, axis=1, keepdims=True)
            acc = alpha * acc + _mm(p.astype(BF16), vch)
            return m_new, l, acc

        init = (jnp.full((group * qb, 1), NEG_BIG, F32), jnp.zeros((group * qb, 1), F32),
                jnp.zeros((group * qb, HEAD_DIM), F32))
        _, l, acc = lax.fori_loop(0, nkc, attn_body, init)
        o = acc / l
        for g in range(group):
            hh = n * group + g
            o_ref[:, hh * HEAD_DIM:(hh + 1) * HEAD_DIM] = o[g * qb:(g + 1) * qb]


def _dsa_call(q, k, v, qi, kiwi, kc=512):
    b, s, _ = q.shape
    kc = min(kc, s)
    topk = min(IDX_TOPK_MAX, s // 4)
    idx_bits = max(1, (s - 1).bit_length())
    qblk = lambda w: pl.BlockSpec((None, Q_BLOCK, w), lambda bi, i: (bi, i, 0))
    seq = lambda w: pl.BlockSpec((None, s, w), lambda bi, i: (bi, 0, 0))
    return pl.pallas_call(
        functools.partial(_dsa_kernel, kc=kc, topk=topk, idx_bits=idx_bits),
        out_shape=jax.ShapeDtypeStruct((b, s, A_WIDTH), F32),
        grid=(b, s // Q_BLOCK),
        in_specs=[qblk(A_WIDTH), qblk(IDX_HEADS * IDX_DIM), qblk(KIWI_W),
                  seq(KV_WIDTH), seq(KV_WIDTH), seq(KIWI_W)],
        out_specs=qblk(A_WIDTH),
        scratch_shapes=[pltpu.VMEM((s // kc, Q_BLOCK, kc), I32),
                        pltpu.VMEM((s // kc, Q_BLOCK, kc), F32)],
        compiler_params=_cparams(("arbitrary", "arbitrary")),
        name="dsa",
    )(q, qi, kiwi, k, v, kiwi)


POOL_HALO = 32


def _pool_kernel(u_ref, w_ref, scale_ref, o_ref, x_buf, a_buf, b_buf, *, tm):
    hl = POOL_HALO
    rows = tm + hl
    first = pl.program_id(1) == 0

    @pl.when(first)
    def _():
        x_buf[0:hl, :] = jnp.zeros((hl, B_WIDTH), F32)

    x = u_ref[...]
    x_buf[hl:rows, :] = x
    a_buf[8:rows, :] = x_buf[8:rows, :] + x_buf[7:rows - 1, :]
    b_buf[16:rows, :] = a_buf[16:rows, :] + a_buf[14:rows - 2, :]
    w2 = a_buf[hl:rows, :]
    w4 = b_buf[hl:rows, :]
    a_buf[24:rows, :] = b_buf[24:rows, :] + b_buf[20:rows - 4, :]
    w8 = a_buf[hl:rows, :]
    b_buf[hl:rows, :] = a_buf[hl:rows, :] + a_buf[hl - 8:rows - 8, :]
    w16 = b_buf[hl:rows, :]
    x_buf[0:hl, :] = x_buf[tm:rows, :]

    lane = lax.broadcasted_iota(I32, (tm, B_WIDTH), 1)
    grp = lane // POOL_GROUP
    wsum = jnp.where(grp == 0, w2, jnp.where(grp == 1, w4, jnp.where(grp == 2, w8, w16)))
    win = jnp.where(grp == 0, 2, jnp.where(grp == 1, 4, jnp.where(grp == 2, 8, 16)))
    t = pl.program_id(1) * tm + lax.broadcasted_iota(I32, (tm, B_WIDTH), 0)
    cnt = jnp.minimum(t + 1, win).astype(F32)
    pooled = wsum / cnt - x
    y = _mm(pooled.astype(BF16), w_ref[...])
    o_ref[...] = y * scale_ref[...]


def _pool_call(rest, w_bd, scale, tm=512):
    b, s, _ = rest.shape
    tm = min(tm, s)
    rows = tm + POOL_HALO
    return pl.pallas_call(
        functools.partial(_pool_kernel, tm=tm),
        out_shape=jax.ShapeDtypeStruct((b, s, B_WIDTH), F32),
        grid=(b, s // tm),
        in_specs=[pl.BlockSpec((None, tm, B_WIDTH), lambda bi, i: (bi, i, 0)),
                  pl.BlockSpec((B_WIDTH, B_WIDTH), lambda bi, i: (0, 0)),
                  pl.BlockSpec((1, B_WIDTH), lambda bi, i: (0, 0))],
        out_specs=pl.BlockSpec((None, tm, B_WIDTH), lambda bi, i: (bi, i, 0)),
        scratch_shapes=[pltpu.VMEM((rows, B_WIDTH), F32)] * 3,
        compiler_params=_cparams(("arbitrary", "arbitrary")),
        name="pool",
    )(rest, w_bd, scale)


def _hgrn_consts():
    tril = np.tril(np.ones((CHUNK, CHUNK), np.float32))
    mats = [tril]
    r = np.arange(CHUNK)
    for h in HGRN_LEVELS:
        mats.append(tril[(r // (2 * h)) * (2 * h) + h - 1])
    return np.concatenate(mats, axis=0)


def _split3(x):
    hi = x.astype(BF16)
    r1 = x - hi.astype(F32)
    mid = r1.astype(BF16)
    lo = (r1 - mid.astype(F32)).astype(BF16)
    return hi, mid, lo


def _hgrn_kernel(q_ref, f_ref, i_ref, g_ref, lb_ref, ng_ref, cm_ref, o_ref, state_ref,
                 *, layer, tm):
    @pl.when(pl.program_id(1) == 0)
    def _():
        state_ref[...] = jnp.zeros(state_ref.shape, F32)

    lbl = lb_ref[...]
    e = jnp.exp(lbl - jnp.max(lbl, axis=0, keepdims=True))
    p = e / jnp.sum(e, axis=0, keepdims=True)
    cum = p[0:1]
    for l in range(1, layer + 1):
        cum = cum + p[l:l + 1]
    lb = jnp.clip(cum - p[0:1], 0.0, 1.0)

    cm = cm_ref[...]
    ng = ng_ref[...]
    row = lax.broadcasted_iota(I32, (CHUNK, 1), 0)
    tt = lax.broadcasted_iota(I32, (CHUNK, CHUNK), 0)
    ss = lax.broadcasted_iota(I32, (CHUNK, CHUNK), 1)
    w = C_WIDTH

    def chunk_body(ci, carry):
        r0 = pl.multiple_of(ci * CHUNK, CHUNK)
        z = f_ref[pl.ds(r0, CHUNK), :]
        f = lb + (1.0 - lb) * jax.nn.sigmoid(z)
        log_f = jnp.log(f)
        kin = (1.0 - lb) * jax.nn.sigmoid(-z)
        qx = q_ref[pl.ds(r0, CHUNK), :]
        qv = qx * jax.nn.sigmoid(qx)
        vv = i_ref[pl.ds(r0, CHUNK), :]
        hi, mid, lo = _split3(log_f)
        cs = _mm(cm, jnp.concatenate([hi, mid, lo], axis=1))
        cs = cs[:, 0:w] + cs[:, w:2 * w] + cs[:, 2 * w:3 * w]
        bcum = cs[0:CHUNK]
        b_last = bcum[CHUNK - 1:CHUNK]
        q_dec = (qv * jnp.exp(bcum)).astype(BF16)
        k_dec = (kin * jnp.exp(b_last - bcum)).astype(BF16)
        s_dec = jnp.exp(b_last)
        qb16, kb16, vb16 = qv.astype(BF16), kin.astype(BF16), vv.astype(BF16)
        q_lv, k_lv = [], []
        for li, h in enumerate(HGRN_LEVELS):
            ref = cs[(li + 1) * CHUNK:(li + 2) * CHUNK]
            odd = ((row // h) % 2) == 1
            q_lv.append((qv * jnp.exp(jnp.where(odd, bcum - ref, NEG_BIG))).astype(BF16))
            k_lv.append((kin * jnp.exp(jnp.where(odd, NEG_BIG, ref - bcum))).astype(BF16))
        outs = []
        for hd in range(C_HEADS):
            sl = slice(hd * C_KDIM, (hd + 1) * C_KDIM)
            attn = jnp.where(tt == ss, _nt(qb16[:, sl], kb16[:, sl]), 0.0)
            for li, h in enumerate(HGRN_LEVELS):
                a = _nt(q_lv[li][:, sl], k_lv[li][:, sl])
                attn = attn + jnp.where((tt // (2 * h)) == (ss // (2 * h)), a, 0.0)
            st = state_ref[hd]
            o = _nt(q_dec[:, sl], st.astype(BF16)) + _mm(attn.astype(BF16), vb16[:, sl])
            state_ref[hd] = st * s_dec[:, sl] + _tn(vb16[:, sl], k_dec[:, sl])
            o = o * lax.rsqrt(jnp.mean(o * o, axis=-1, keepdims=True) + EPS) * ng
            outs.append(o)
        gx = g_ref[pl.ds(r0, CHUNK), :]
        o_ref[pl.ds(r0, CHUNK), :] = jnp.concatenate(outs, axis=1) * (gx * jax.nn.sigmoid(gx))
        return carry

    lax.fori_loop(0, tm // CHUNK, chunk_body, 0)


def _hgrn_call(rest, lb_logits, norm_g, layer, tm=256):
    b, s, _ = rest.shape
    depth = lb_logits.shape[0]
    cm = jnp.asarray(_hgrn_consts(), BF16)
    col = lambda j: pl.BlockSpec((None, tm, C_WIDTH), lambda bi, i, j=j: (bi, i, j))
    return pl.pallas_call(
        functools.partial(_hgrn_kernel, layer=layer, tm=tm),
        out_shape=jax.ShapeDtypeStruct((b, s, C_WIDTH), F32),
        grid=(b, s // tm),
        in_specs=[col(1), col(2), col(3), col(4),
                  pl.BlockSpec((depth, C_WIDTH), lambda bi, i: (0, 0)),
                  pl.BlockSpec((1, C_VDIM), lambda bi, i: (0, 0)),
                  pl.BlockSpec(cm.shape, lambda bi, i: (0, 0))],
        out_specs=pl.BlockSpec((None, tm, C_WIDTH), lambda bi, i: (bi, i, 0)),
        scratch_shapes=[pltpu.VMEM((C_HEADS, C_VDIM, C_KDIM), F32)],
        compiler_params=_cparams(("arbitrary", "arbitrary")),
        name="hgrn2",
    )(rest, rest, rest, rest, lb_logits, norm_g.reshape(1, C_VDIM), cm)


def _outproj_router_kernel(ya_ref, yb_ref, yc_ref, h_ref, g1_ref, wo_ref, sh_ref, sc_ref, g_ref,
                           wr_ref, br_ref, h1_ref, u_ref, gate_ref):
    y = _mm(ya_ref[...].astype(BF16), wo_ref[0:A_WIDTH, :])
    y = y + _mm(yb_ref[...].astype(BF16), wo_ref[A_WIDTH:A_WIDTH + B_WIDTH, :])
    y = y + _mm(yc_ref[...].astype(BF16), wo_ref[A_WIDTH + B_WIDTH:, :])
    h1 = h_ref[...] + g1_ref[...] * y
    h1_ref[...] = h1
    u = _ada_norm(h1, g_ref[...], sc_ref[...], sh_ref[...])
    u_ref[...] = u.astype(BF16)

    logits = jnp.dot(u, wr_ref[...], precision=lax.Precision.HIGHEST,
                     preferred_element_type=F32) + br_ref[...]
    lane = lax.broadcasted_iota(I32, logits.shape, 1).astype(F32)
    logits = jnp.where(lane < N_EXPERTS, logits, -jnp.inf)
    work = logits
    sel = jnp.zeros(logits.shape, jnp.bool_)
    top = None
    for k in range(TOP_K):
        m = jnp.max(work, axis=1, keepdims=True)
        if k == 0:
            top = m
        first = jnp.min(jnp.where(work == m, lane, float(LANES)), axis=1, keepdims=True)
        pick = lane == first
        sel = sel | pick
        work = jnp.where(pick, -jnp.inf, work)
    ex = jnp.where(sel, jnp.exp(logits - top), 0.0)
    gate_ref[...] = ex / jnp.sum(ex, axis=1, keepdims=True)


def _outproj_router_call(ya, yb, yc, h, g1, w_out, sh, sc, g, w_router, b_router, tm=512):
    b, s, d = h.shape
    tm = min(tm, s)
    tok = lambda w: pl.BlockSpec((None, tm, w), lambda bi, i: (bi, i, 0))
    per_b = pl.BlockSpec((None, 1, d), lambda bi, i: (bi, 0, 0))
    full = lambda a: pl.BlockSpec(a.shape, lambda bi, i: (0,) * a.ndim)
    return pl.pallas_call(
        _outproj_router_kernel,
        out_shape=(jax.ShapeDtypeStruct((b, s, d), F32), jax.ShapeDtypeStruct((b, s, d), BF16),
                   jax.ShapeDtypeStruct((b, s, LANES), F32)),
        grid=(b, s // tm),
        in_specs=[tok(A_WIDTH), tok(B_WIDTH), tok(C_WIDTH), tok(d), per_b, full(w_out),
                  per_b, per_b, full(g), full(w_router), full(b_router)],
        out_specs=(tok(d), tok(d), tok(LANES)),
        compiler_params=_cparams(("arbitrary", "arbitrary")),
        name="outproj_router",
    )(ya, yb, yc, h, g1, w_out, sh, sc, g, w_router, b_router)


def _moe_kernel(x_ref, gate_ref, h_ref, g2_ref, w1g_ref, w1l_ref, b1g_ref, b1l_ref, w2_ref, b2_ref,
                fg_ref, o_ref, acc_ref, *, final_norm, fsplit):
    e = pl.program_id(2)

    @pl.when(e == 0)
    def _():
        acc_ref[...] = jnp.zeros(acc_ref.shape, F32)

    x = x_ref[...]
    gates = gate_ref[...]
    lane = lax.broadcasted_iota(I32, gates.shape, 1)
    ge = jnp.sum(jnp.where(lane == e, gates, 0.0), axis=1, keepdims=True)
    ff = w1g_ref.shape[1]
    fs = ff // fsplit
    y = b2_ref[...] + jnp.zeros((x.shape[0], 1), F32)
    for j in range(fsplit):
        cols = slice(j * fs, (j + 1) * fs)
        glu = jnp.minimum(_mm(x, w1g_ref[:, cols]) + b1g_ref[:, cols], SWIGLU_LIMIT)
        lin = jnp.clip(_mm(x, w1l_ref[:, cols]) + b1l_ref[:, cols], -SWIGLU_LIMIT, SWIGLU_LIMIT)
        act = glu * jax.nn.sigmoid(SWIGLU_ALPHA * glu) * (lin + 1.0)
        y = y + _mm(act.astype(BF16), w2_ref[cols, :])
    acc_ref[...] += ge * y

    @pl.when(e == pl.num_programs(2) - 1)
    def _():
        out = h_ref[...] + g2_ref[...] * acc_ref[...]
        if final_norm:
            out = out * lax.rsqrt(jnp.mean(out * out, axis=-1, keepdims=True) + EPS) * fg_ref[...]
        o_ref[...] = out


def _moe_call(u, gates, h1, g2, w1g, w1l, b1g, b1l, w2, b2, final_g, final_norm, tm=1024):
    b, s, d = h1.shape
    tm = min(tm, s)
    ne, _, ff = w1g.shape
    tok = lambda w: pl.BlockSpec((None, tm, w), lambda bi, i, e: (bi, i, 0))
    ex = lambda r, c: pl.BlockSpec((None, r, c), lambda bi, i, e: (e, 0, 0))
    return pl.pallas_call(
        functools.partial(_moe_kernel, final_norm=final_norm, fsplit=2),
        out_shape=jax.ShapeDtypeStruct((b, s, d), F32),
        grid=(b, s // tm, ne),
        in_specs=[tok(d), tok(LANES), tok(d),
                  pl.BlockSpec((None, 1, d), lambda bi, i, e: (bi, 0, 0)),
                  ex(d, ff), ex(d, ff), ex(1, ff), ex(1, ff), ex(ff, d), ex(1, d),
                  pl.BlockSpec((1, d), lambda bi, i, e: (0, 0))],
        out_specs=tok(d),
        scratch_shapes=[pltpu.VMEM((tm, d), F32)],
        compiler_params=_cparams(("arbitrary", "arbitrary", "arbitrary")),
        name="moe_dense",
    )(u, gates, h1, g2, w1g, w1l, b1g, b1l, w2, b2, final_g)


def _pack_w_in(w_in):
    d = w_in.shape[0]
    offs = np.cumsum((0,) + IN_SPLITS)
    head = w_in[:, :offs[5]]
    wi = w_in[:, offs[5]:offs[6]]
    pad = jnp.zeros((d, KIWI_W - IDX_DIM - IDX_HEADS), w_in.dtype)
    return jnp.concatenate([head, wi, pad, w_in[:, offs[6]:]], axis=1).astype(BF16)


def _block_diag(pool_w):
    g, c, _ = pool_w.shape
    out = jnp.zeros((g * c, g * c), pool_w.dtype)
    for j in range(g):
        out = out.at[j * c:(j + 1) * c, j * c:(j + 1) * c].set(pool_w[j])
    return out.astype(BF16)


def kernel(x, c, positions, w_ada, b_ada, norm1_g, norm2_g, w_in, w_out, pool_w, pool_scale,
           hgrn_norm_g, lb_logits, w_router, b_router, w1, b1, w2, b2, final_g):
    bsz, s, d = x.shape
    depth = w_ada.shape[0]
    mod = _ada_call(c, w_ada, b_ada)
    cos, sin = _rope_table_call(positions)
    h = x
    for l in range(depth):
        sh1, sc1, g1, sh2, sc2, g2 = [mod[l, :, j * d:(j + 1) * d].reshape(bsz, 1, d)
                                      for j in range(6)]
        q, k, v, qi, kiwi, rest = _inproj_call(
            h, sh1, sc1, norm1_g[l].reshape(1, d), _pack_w_in(w_in[l]), cos, sin)
        ya = _dsa_call(q, k, v, qi, kiwi)
        yb = _pool_call(rest, _block_diag(pool_w[l]), pool_scale[l].reshape(1, B_WIDTH))
        yc = _hgrn_call(rest, lb_logits, hgrn_norm_g[l], l)
        wr = jnp.pad(w_router[l], ((0, 0), (0, LANES - N_EXPERTS)))
        br = jnp.pad(b_router[l], (0, LANES - N_EXPERTS)).reshape(1, LANES)
        h1, u2, gates = _outproj_router_call(
            ya, yb, yc, h, g1, w_out[l].astype(BF16), sh2, sc2, norm2_g[l].reshape(1, d), wr, br)
        ne, _, ff2 = w1[l].shape
        h = _moe_call(
            u2, gates, h1, g2,
            w1[l][:, :, 0::2].astype(BF16), w1[l][:, :, 1::2].astype(BF16),
            b1[l][:, 0::2].reshape(ne, 1, ff2 // 2), b1[l][:, 1::2].reshape(ne, 1, ff2 // 2),
            w2[l].astype(BF16), b2[l].reshape(ne, 1, d),
            final_g.reshape(1, d), final_norm=(l == depth - 1))
    return h
```

```python
import functools
import math

import numpy as np
import jax
import jax.numpy as jnp
from jax import lax
from jax.experimental import pallas as pl
from jax.experimental.pallas import tpu as pltpu

F32 = jnp.float32
BF16 = jnp.bfloat16
I32 = jnp.int32

CHUNK = 64
EPS = 1e-6
NEG_BIG = -1e30
A_HEADS, A_KV_HEADS, HEAD_DIM = 8, 2, 64
IDX_HEADS, IDX_DIM, IDX_TOPK_MAX = 4, 64, 256
Q_BLOCK = 128
ROPE_THETA = 10000.0
A_WIDTH = A_HEADS * HEAD_DIM
POOL_WINDOWS = (2, 4, 8, 16)
POOL_GROUP = 64
B_WIDTH = len(POOL_WINDOWS) * POOL_GROUP
C_HEADS, C_KDIM, C_VDIM = 4, 64, 64
C_WIDTH = C_HEADS * C_VDIM
N_EXPERTS, TOP_K = 32, 4
SWIGLU_LIMIT, SWIGLU_ALPHA = 7.0, 1.702
KV_WIDTH = A_KV_HEADS * HEAD_DIM
IN_SPLITS = (A_WIDTH, KV_WIDTH, KV_WIDTH, IDX_HEADS * IDX_DIM, IDX_DIM, IDX_HEADS,
             B_WIDTH, C_HEADS * C_KDIM, C_HEADS * C_KDIM, C_WIDTH, C_WIDTH)

LANES = 128
INT_MIN = -(2 ** 31)
VMEM_LIMIT = 56 * 1024 * 1024

KIWI_W = LANES
REST_W = B_WIDTH + 4 * C_WIDTH
PACK_W = A_WIDTH + 2 * KV_WIDTH + IDX_HEADS * IDX_DIM + KIWI_W + REST_W
HGRN_LEVELS = (32, 16, 8, 4, 2, 1)


def _nt(a, b):
    return lax.dot_general(a, b, (((1,), (1,)), ((), ())), preferred_element_type=F32)


def _tn(a, b):
    return lax.dot_general(a, b, (((0,), (0,)), ((), ())), preferred_element_type=F32)


def _mm(a, b):
    return jnp.dot(a, b, preferred_element_type=F32)


def _cparams(sem):
    return pltpu.CompilerParams(dimension_semantics=sem, vmem_limit_bytes=VMEM_LIMIT)


def _ada_kernel(c_ref, w_ref, b_ref, o_ref):
    c = c_ref[...]
    ca = c * jax.nn.sigmoid(c)
    o_ref[...] = jnp.dot(ca, w_ref[...], precision=lax.Precision.HIGHEST,
                         preferred_element_type=F32) + b_ref[...]


def _ada_call(c, w_ada, b_ada):
    depth, d, d6 = w_ada.shape
    b = c.shape[0]
    nblk = d6 // d
    return pl.pallas_call(
        _ada_kernel,
        out_shape=jax.ShapeDtypeStruct((depth, b, d6), F32),
        grid=(depth, nblk),
        in_specs=[pl.BlockSpec((b, d), lambda l, j: (0, 0)),
                  pl.BlockSpec((None, d, d), lambda l, j: (l, 0, j)),
                  pl.BlockSpec((None, 1, d), lambda l, j: (l, 0, j))],
        out_specs=pl.BlockSpec((None, b, d), lambda l, j: (l, 0, j)),
        compiler_params=_cparams(("arbitrary", "arbitrary")),
        name="ada_mod",
    )(c, w_ada, b_ada.reshape(depth, 1, d6))


def _rope_table_kernel(pos_ref, inv_ref, sign_ref, cos_ref, sin_ref):
    ang = pos_ref[...].astype(F32) * inv_ref[...]
    cos_ref[...] = jnp.cos(ang)
    sin_ref[...] = jnp.sin(ang) * sign_ref[...]


def _rope_table_call(positions, ts=512):
    b, s = positions.shape
    half = HEAD_DIM // 2
    inv = jnp.power(jnp.float32(ROPE_THETA), -jnp.arange(0, HEAD_DIM, 2, dtype=F32) / HEAD_DIM)
    inv128 = jnp.tile(inv, LANES // half).reshape(1, LANES)
    sign128 = jnp.tile(jnp.concatenate([-jnp.ones((half,), F32), jnp.ones((half,), F32)]),
                       LANES // HEAD_DIM).reshape(1, LANES)
    spec = pl.BlockSpec((None, ts, LANES), lambda bi, i: (bi, i, 0))
    return pl.pallas_call(
        _rope_table_kernel,
        out_shape=(jax.ShapeDtypeStruct((b, s, LANES), F32),) * 2,
        grid=(b, s // ts),
        in_specs=[pl.BlockSpec((None, ts, 1), lambda bi, i: (bi, i, 0)),
                  pl.BlockSpec((1, LANES), lambda bi, i: (0, 0)),
                  pl.BlockSpec((1, LANES), lambda bi, i: (0, 0))],
        out_specs=(spec, spec),
        compiler_params=_cparams(("arbitrary", "arbitrary")),
        name="rope_table",
    )(positions.reshape(b, s, 1), inv128, sign128)


def _rope_tile(x, cos, sin_signed, first_half):
    partner = jnp.where(first_half, pltpu.roll(x, LANES - HEAD_DIM // 2, 1),
                        pltpu.roll(x, HEAD_DIM // 2, 1))
    return x * cos + partner * sin_signed


def _ada_norm(x, g, sc, sh):
    y = x * lax.rsqrt(jnp.mean(x * x, axis=-1, keepdims=True) + EPS)
    return (y * g) * (1.0 + sc) + sh


def _inproj_kernel(h_ref, sh_ref, sc_ref, g_ref, w_ref, cos_ref, sin_ref,
                   q_ref, k_ref, v_ref, qi_ref, kiwi_ref, rest_ref):
    u = _ada_norm(h_ref[...], g_ref[...], sc_ref[...], sh_ref[...]).astype(BF16)
    cos = cos_ref[...]
    sin = sin_ref[...]
    lane = lax.broadcasted_iota(I32, cos.shape, 1)
    first_half = (lane % HEAD_DIM) < (HEAD_DIM // 2)
    rope = lambda x: _rope_tile(x, cos, sin, first_half)

    off = 0
    q_scale = HEAD_DIM ** -0.5
    for j in range(A_WIDTH // LANES):
        z = _mm(u, w_ref[:, off + j * LANES: off + (j + 1) * LANES])
        q_ref[:, j * LANES:(j + 1) * LANES] = (rope(z) * q_scale).astype(BF16)
    off += A_WIDTH
    k_ref[...] = rope(_mm(u, w_ref[:, off:off + KV_WIDTH])).astype(BF16)
    off += KV_WIDTH
    v_ref[...] = _mm(u, w_ref[:, off:off + KV_WIDTH]).astype(BF16)
    off += KV_WIDTH
    for j in range(IDX_HEADS * IDX_DIM // LANES):
        z = _mm(u, w_ref[:, off + j * LANES: off + (j + 1) * LANES])
        qi_ref[:, j * LANES:(j + 1) * LANES] = rope(z)
    off += IDX_HEADS * IDX_DIM
    z = _mm(u, w_ref[:, off:off + KIWI_W])
    wi_scale = (IDX_HEADS * IDX_DIM) ** -0.5
    kiwi_ref[...] = jnp.where(lane < IDX_DIM, rope(z), z * wi_scale)
    off += KIWI_W
    rest_ref[...] = _mm(u, w_ref[:, off:off + REST_W])


def _inproj_call(h, sh, sc, g, w_pack, cos, sin, tm=256):
    b, s, d = h.shape
    tok = lambda w: pl.BlockSpec((None, tm, w), lambda bi, i: (bi, i, 0))
    per_b = pl.BlockSpec((None, 1, d), lambda bi, i: (bi, 0, 0))
    outs = [(A_WIDTH, BF16), (KV_WIDTH, BF16), (KV_WIDTH, BF16),
            (IDX_HEADS * IDX_DIM, F32), (KIWI_W, F32), (REST_W, F32)]
    return pl.pallas_call(
        _inproj_kernel,
        out_shape=tuple(jax.ShapeDtypeStruct((b, s, w), dt) for w, dt in outs),
        grid=(b, s // tm),
        in_specs=[tok(d), per_b, per_b,
                  pl.BlockSpec((1, d), lambda bi, i: (0, 0)),
                  pl.BlockSpec((d, PACK_W), lambda bi, i: (0, 0)),
                  tok(LANES), tok(LANES)],
        out_specs=tuple(tok(w) for w, _ in outs),
        compiler_params=_cparams(("arbitrary", "arbitrary")),
        name="inproj",
    )(h, sh, sc, g, w_pack, cos, sin)


def _dsa_kernel(q_ref, qi_ref, kwq_ref, k_ref, v_ref, kwk_ref, o_ref, keys_ref, bias_ref,
                *, kc, topk, idx_bits):
    qb = Q_BLOCK
    i = pl.program_id(1)
    t0 = i * qb
    nkc = (t0 + qb + kc - 1) // kc
    row = lax.broadcasted_iota(I32, (qb, 1), 0)
    limit = t0 + (row // CHUNK + 1) * CHUNK
    lane_kc = lax.broadcasted_iota(I32, (qb, kc), 1)
    lane_128 = lax.broadcasted_iota(I32, (qb, LANES), 1)
    group = A_HEADS // A_KV_HEADS

    qi = qi_ref[...]
    qi_stack = jnp.concatenate(
        [qi[:, h * IDX_DIM:(h + 1) * IDX_DIM] for h in range(IDX_HEADS)], axis=0).astype(BF16)
    wi = kwq_ref[:, IDX_DIM:IDX_DIM + IDX_HEADS]

    def score_body(c, carry):
        off = pl.multiple_of(c * kc, kc)
        ki = kwk_ref[pl.ds(off, kc), 0:IDX_DIM].astype(BF16)
        s = jnp.maximum(_nt(qi_stack, ki), 0.0)
        score = s[0:qb] * wi[:, 0:1]
        for h in range(1, IDX_HEADS):
            score = score + s[h * qb:(h + 1) * qb] * wi[:, h:h + 1]
        score = jnp.where(score == 0.0, 0.0, score)
        bits = lax.bitcast_convert_type(score, I32)
        key = jnp.where(bits < 0, bits ^ jnp.int32(0x7FFFFFFF), bits)
        keys_ref[c] = jnp.where(off + lane_kc < limit, key, jnp.int32(INT_MIN))
        return carry

    lax.fori_loop(0, nkc, score_body, 0)

    def count(pred):
        def body(c, acc):
            for j in range(kc // LANES):
                kk = keys_ref[c, :, j * LANES:(j + 1) * LANES]
                kidx = c * kc + j * LANES + lane_128
                acc = acc + jnp.where(pred(kk, kidx), 1.0, 0.0)
            return acc
        acc = lax.fori_loop(0, nkc, body, jnp.zeros((qb, LANES), F32))
        return jnp.sum(acc, axis=1, keepdims=True)

    def thr_body(j, thr):
        cand = thr + jnp.left_shift(jnp.int32(1), 31 - j)
        cnt = count(lambda kk, _: kk >= cand)
        return jnp.where(cnt >= topk, cand, thr)

    thr = lax.fori_loop(0, 32, thr_body, jnp.full((qb, 1), INT_MIN, I32))
    need = topk - count(lambda kk, _: kk > thr)

    def tie_body(j, jmax):
        cand = jmax + jnp.left_shift(jnp.int32(1), idx_bits - 1 - j)
        cnt = count(lambda kk, kidx: (kk == thr) & (kidx < cand))
        return jnp.where(cnt < need, cand, jmax)

    jmax = lax.fori_loop(0, idx_bits, tie_body, jnp.zeros((qb, 1), I32))

    def bias_body(c, carry):
        kk = keys_ref[c]
        kidx = c * kc + lane_kc
        sel = (kk > thr) | ((kk == thr) & (kidx <= jmax))
        bias_ref[c] = jnp.where(sel & (kidx < limit), 0.0, NEG_BIG)
        return carry

    lax.fori_loop(0, nkc, bias_body, 0)

    q = q_ref[...]
    for n in range(A_KV_HEADS):
        qn = jnp.concatenate(
            [q[:, (n * group + g) * HEAD_DIM:(n * group + g + 1) * HEAD_DIM] for g in range(group)],
            axis=0)

        def attn_body(c, carry, n=n, qn=qn):
            m, l, acc = carry
            off = pl.multiple_of(c * kc, kc)
            kch = k_ref[pl.ds(off, kc), n * HEAD_DIM:(n + 1) * HEAD_DIM]
            vch = v_ref[pl.ds(off, kc), n * HEAD_DIM:(n + 1) * HEAD_DIM]
            s = _nt(qn, kch).reshape(group, qb, kc) + bias_ref[c][None]
            s = s.reshape(group * qb, kc)
            m_new = jnp.maximum(m, jnp.max(s, axis=1, keepdims=True))
            alpha = jnp.exp(m - m_new)
            p = jnp.exp(s - m_new)
            l = alpha * l + jnp.sum(p, axis=1, keepdims=True)
            acc = alpha * acc + _mm(p.astype(BF16), vch)
            return m_new, l, acc

        init = (jnp.full((group * qb, 1), NEG_BIG, F32), jnp.zeros((group * qb, 1), F32),
                jnp.zeros((group * qb, HEAD_DIM), F32))
        _, l, acc = lax.fori_loop(0, nkc, attn_body, init)
        o = acc / l
        for g in range(group):
            hh = n * group + g
            o_ref[:, hh * HEAD_DIM:(hh + 1) * HEAD_DIM] = o[g * qb:(g + 1) * qb]


def _dsa_call(q, k, v, qi, kiwi, kc=512):
    b, s, _ = q.shape
    kc = min(kc, s)
    topk = min(IDX_TOPK_MAX, s // 4)
    idx_bits = max(1, (s - 1).bit_length())
    qblk = lambda w: pl.BlockSpec((None, Q_BLOCK, w), lambda bi, i: (bi, i, 0))
    seq = lambda w: pl.BlockSpec((None, s, w), lambda bi, i: (bi, 0, 0))
    return pl.pallas_call(
        functools.partial(_dsa_kernel, kc=kc, topk=topk, idx_bits=idx_bits),
        out_shape=jax.ShapeDtypeStruct((b, s, A_WIDTH), F32),
        grid=(b, s // Q_BLOCK),
        in_specs=[qblk(A_WIDTH), qblk(IDX_HEADS * IDX_DIM), qblk(KIWI_W),
                  seq(KV_WIDTH), seq(KV_WIDTH), seq(KIWI_W)],
        out_specs=qblk(A_WIDTH),
        scratch_shapes=[pltpu.VMEM((s // kc, Q_BLOCK, kc), I32),
                        pltpu.VMEM((s // kc, Q_BLOCK, kc), F32)],
        compiler_params=_cparams(("arbitrary", "arbitrary")),
        name="dsa",
    )(q, qi, kiwi, k, v, kiwi)


POOL_HALO = 32


def _pool_kernel(u_ref, w_ref, scale_ref, o_ref, x_buf, a_buf, b_buf, *, tm):
    hl = POOL_HALO
    rows = tm + hl
    first = pl.program_id(1) == 0

    @pl.when(first)
    def _():
        x_buf[0:hl, :] = jnp.zeros((hl, B_WIDTH), F32)

    x = u_ref[...]
    x_buf[hl:rows, :] = x
    a_buf[8:rows, :] = x_buf[8:rows, :] + x_buf[7:rows - 1, :]
    b_buf[16:rows, :] = a_buf[16:rows, :] + a_buf[14:rows - 2, :]
    w2 = a_buf[hl:rows, :]
    w4 = b_buf[hl:rows, :]
    a_buf[24:rows, :] = b_buf[24:rows, :] + b_buf[20:rows - 4, :]
    w8 = a_buf[hl:rows, :]
    b_buf[hl:rows, :] = a_buf[hl:rows, :] + a_buf[hl - 8:rows - 8, :]
    w16 = b_buf[hl:rows, :]
    x_buf[0:hl, :] = x_buf[tm:rows, :]

    lane = lax.broadcasted_iota(I32, (tm, B_WIDTH), 1)
    grp = lane // POOL_GROUP
    wsum = jnp.where(grp == 0, w2, jnp.where(grp == 1, w4, jnp.where(grp == 2, w8, w16)))
    win = jnp.where(grp == 0, 2, jnp.where(grp == 1, 4, jnp.where(grp == 2, 8, 16)))
    t = pl.program_id(1) * tm + lax.broadcasted_iota(I32, (tm, B_WIDTH), 0)
    cnt = jnp.minimum(t + 1, win).astype(F32)
    pooled = wsum / cnt - x
    y = _mm(pooled.astype(BF16), w_ref[...])
    o_ref[...] = y * scale_ref[...]


def _pool_call(rest, w_bd, scale, tm=512):
    b, s, _ = rest.shape
    tm = min(tm, s)
    rows = tm + POOL_HALO
    return pl.pallas_call(
        functools.partial(_pool_kernel, tm=tm),
        out_shape=jax.ShapeDtypeStruct((b, s, B_WIDTH), F32),
        grid=(b, s // tm),
        in_specs=[pl.BlockSpec((None, tm, B_WIDTH), lambda bi, i: (bi, i, 0)),
                  pl.BlockSpec((B_WIDTH, B_WIDTH), lambda bi, i: (0, 0)),
                  pl.BlockSpec((1, B_WIDTH), lambda bi, i: (0, 0))],
        out_specs=pl.BlockSpec((None, tm, B_WIDTH), lambda bi, i: (bi, i, 0)),
        scratch_shapes=[pltpu.VMEM((rows, B_WIDTH), F32)] * 3,
        compiler_params=_cparams(("arbitrary", "arbitrary")),
        name="pool",
    )(rest, w_bd, scale)


def _hgrn_consts():
    tril = np.tril(np.ones((CHUNK, CHUNK), np.float32))
    mats = [tril]
    r = np.arange(CHUNK)
    for h in HGRN_LEVELS:
        mats.append(tril[(r // (2 * h)) * (2 * h) + h - 1])
    return np.concatenate(mats, axis=0)


def _split3(x):
    hi = x.astype(BF16)
    r1 = x - hi.astype(F32)
    mid = r1.astype(BF16)
    lo = (r1 - mid.astype(F32)).astype(BF16)
    return hi, mid, lo


def _hgrn_kernel(q_ref, f_ref, i_ref, g_ref, lb_ref, ng_ref, cm_ref, o_ref, state_ref,
                 *, layer, tm):
    @pl.when(pl.program_id(1) == 0)
    def _():
        state_ref[...] = jnp.zeros(state_ref.shape, F32)

    lbl = lb_ref[...]
    e = jnp.exp(lbl - jnp.max(lbl, axis=0, keepdims=True))
    p = e / jnp.sum(e, axis=0, keepdims=True)
    cum = p[0:1]
    for l in range(1, layer + 1):
        cum = cum + p[l:l + 1]
    lb = jnp.clip(cum - p[0:1], 0.0, 1.0)

    cm = cm_ref[...]
    ng = ng_ref[...]
    row = lax.broadcasted_iota(I32, (CHUNK, 1), 0)
    tt = lax.broadcasted_iota(I32, (CHUNK, CHUNK), 0)
    ss = lax.broadcasted_iota(I32, (CHUNK, CHUNK), 1)
    w = C_WIDTH

    def chunk_body(ci, carry):
        r0 = pl.multiple_of(ci * CHUNK, CHUNK)
        z = f_ref[pl.ds(r0, CHUNK), :]
        f = lb + (1.0 - lb) * jax.nn.sigmoid(z)
        log_f = jnp.log(f)
        kin = (1.0 - lb) * jax.nn.sigmoid(-z)
        qx = q_ref[pl.ds(r0, CHUNK), :]
        qv = qx * jax.nn.sigmoid(qx)
        vv = i_ref[pl.ds(r0, CHUNK), :]
        hi, mid, lo = _split3(log_f)
        cs = _mm(cm, jnp.concatenate([hi, mid, lo], axis=1))
        cs = cs[:, 0:w] + cs[:, w:2 * w] + cs[:, 2 * w:3 * w]
        bcum = cs[0:CHUNK]
        b_last = bcum[CHUNK - 1:CHUNK]
        q_dec = (qv * jnp.exp(bcum)).astype(BF16)
        k_dec = (kin * jnp.exp(b_last - bcum)).astype(BF16)
        s_dec = jnp.exp(b_last)
        qb16, kb16, vb16 = qv.astype(BF16), kin.astype(BF16), vv.astype(BF16)
        q_lv, k_lv = [], []
        for li, h in enumerate(HGRN_LEVELS):
            ref = cs[(li + 1) * CHUNK:(li + 2) * CHUNK]
            odd = ((row // h) % 2) == 1
            q_lv.append((qv * jnp.exp(jnp.where(odd, bcum - ref, NEG_BIG))).astype(BF16))
            k_lv.append((kin * jnp.exp(jnp.where(odd, NEG_BIG, ref - bcum))).astype(BF16))
        outs = []
        for hd in range(C_HEADS):
            sl = slice(hd * C_KDIM, (hd + 1) * C_KDIM)
            attn = jnp.where(tt == ss, _nt(qb16[:, sl], kb16[:, sl]), 0.0)
            for li, h in enumerate(HGRN_LEVELS):
                a = _nt(q_lv[li][:, sl], k_lv[li][:, sl])
                attn = attn + jnp.where((tt // (2 * h)) == (ss // (2 * h)), a, 0.0)
            st = state_ref[hd]
            o = _nt(q_dec[:, sl], st.astype(BF16)) + _mm(attn.astype(BF16), vb16[:, sl])
            state_ref[hd] = st * s_dec[:, sl] + _tn(vb16[:, sl], k_dec[:, sl])
            o = o * lax.rsqrt(jnp.mean(o * o, axis=-1, keepdims=True) + EPS) * ng
            outs.append(o)
        gx = g_ref[pl.ds(r0, CHUNK), :]
        o_ref[pl.ds(r0, CHUNK), :] = jnp.concatenate(outs, axis=1) * (gx * jax.nn.sigmoid(gx))
        return carry

    lax.fori_loop(0, tm // CHUNK, chunk_body, 0)


def _hgrn_call(rest, lb_logits, norm_g, layer, tm=256):
    b, s, _ = rest.shape
    depth = lb_logits.shape[0]
    cm = jnp.asarray(_hgrn_consts(), BF16)
    col = lambda j: pl.BlockSpec((None, tm, C_WIDTH), lambda bi, i, j=j: (bi, i, j))
    return pl.pallas_call(
        functools.partial(_hgrn_kernel, layer=layer, tm=tm),
        out_shape=jax.ShapeDtypeStruct((b, s, C_WIDTH), F32),
        grid=(b, s // tm),
        in_specs=[col(1), col(2), col(3), col(4),
                  pl.BlockSpec((depth, C_WIDTH), lambda bi, i: (0, 0)),
                  pl.BlockSpec((1, C_VDIM), lambda bi, i: (0, 0)),
                  pl.BlockSpec(cm.shape, lambda bi, i: (0, 0))],
        out_specs=pl.BlockSpec((None, tm, C_WIDTH), lambda bi, i: (bi, i, 0)),
        scratch_shapes=[pltpu.VMEM((C_HEADS, C_VDIM, C_KDIM), F32)],
        compiler_params=_cparams(("arbitrary", "arbitrary")),
        name="hgrn2",
    )(rest, rest, rest, rest, lb_logits, norm_g.reshape(1, C_VDIM), cm)


RANK_BITS = 20


def _outproj_router_kernel(ya_ref, yb_ref, yc_ref, h_ref, g1_ref, wo_ref, sh_ref, sc_ref, g_ref,
                           wr_ref, br_ref, tri_ref, h1_ref, u_ref, code_ref, gk_ref, cnt_ref,
                           run_ref):
    @pl.when((pl.program_id(0) == 0) & (pl.program_id(1) == 0))
    def _():
        run_ref[...] = jnp.zeros(run_ref.shape, F32)

    y = _mm(ya_ref[...].astype(BF16), wo_ref[0:A_WIDTH, :])
    y = y + _mm(yb_ref[...].astype(BF16), wo_ref[A_WIDTH:A_WIDTH + B_WIDTH, :])
    y = y + _mm(yc_ref[...].astype(BF16), wo_ref[A_WIDTH + B_WIDTH:, :])
    h1 = h_ref[...] + g1_ref[...] * y
    h1_ref[...] = h1
    u = _ada_norm(h1, g_ref[...], sc_ref[...], sh_ref[...])
    u_ref[...] = u

    logits = jnp.dot(u, wr_ref[...], precision=lax.Precision.HIGHEST,
                     preferred_element_type=F32) + br_ref[...]
    lane = lax.broadcasted_iota(I32, logits.shape, 1).astype(F32)
    work = jnp.where(lane < N_EXPERTS, logits, -jnp.inf)
    picks, firsts, tops = [], [], []
    for k in range(TOP_K):
        m = jnp.max(work, axis=1, keepdims=True)
        first = jnp.min(jnp.where(work == m, lane, float(LANES)), axis=1, keepdims=True)
        pick = lane == first
        work = jnp.where(pick, -jnp.inf, work)
        picks.append(pick)
        firsts.append(first)
        tops.append(m)
    ex = [jnp.exp(m - tops[0]) for m in tops]
    den = ex[0] + ex[1] + ex[2] + ex[3]

    sel = picks[0] | picks[1] | picks[2] | picks[3]
    sel_f = jnp.where(sel, 1.0, 0.0)
    prefix = _mm(tri_ref[...], sel_f.astype(BF16)) + run_ref[...]
    run_ref[...] += jnp.sum(sel_f, axis=0, keepdims=True)
    cnt_ref[...] = run_ref[...]

    code = jnp.zeros(logits.shape, I32)
    gk = jnp.zeros(logits.shape, F32)
    for k in range(TOP_K):
        rank = jnp.sum(jnp.where(picks[k], prefix, 0.0), axis=1, keepdims=True)
        ck = (firsts[k].astype(I32) << RANK_BITS) | rank.astype(I32)
        code = jnp.where(lane == float(k), ck, code)
        gk = jnp.where(lane == float(k), ex[k] / den, gk)
    code_ref[...] = code
    gk_ref[...] = gk


def _outproj_router_call(ya, yb, yc, h, g1, w_out, sh, sc, g, w_router, b_router, tm=512):
    b, s, d = h.shape
    tm = min(tm, s)
    tri = jnp.asarray(np.tril(np.ones((tm, tm), np.float32), -1), BF16)
    tok = lambda w: pl.BlockSpec((None, tm, w), lambda bi, i: (bi, i, 0))
    per_b = pl.BlockSpec((None, 1, d), lambda bi, i: (bi, 0, 0))
    full = lambda a: pl.BlockSpec(a.shape, lambda bi, i: (0,) * a.ndim)
    return pl.pallas_call(
        _outproj_router_kernel,
        out_shape=(jax.ShapeDtypeStruct((b, s, d), F32), jax.ShapeDtypeStruct((b, s, d), F32),
                   jax.ShapeDtypeStruct((b, s, LANES), I32), jax.ShapeDtypeStruct((b, s, LANES), F32),
                   jax.ShapeDtypeStruct((1, LANES), F32)),
        grid=(b, s // tm),
        in_specs=[tok(A_WIDTH), tok(B_WIDTH), tok(C_WIDTH), tok(d), per_b, full(w_out),
                  per_b, per_b, full(g), full(w_router), full(b_router), full(tri)],
        out_specs=(tok(d), tok(d), tok(LANES), tok(LANES),
                   pl.BlockSpec((1, LANES), lambda bi, i: (0, 0))),
        scratch_shapes=[pltpu.VMEM((1, LANES), F32)],
        compiler_params=_cparams(("arbitrary", "arbitrary")),
        name="outproj_router",
    )(ya, yb, yc, h, g1, w_out, sh, sc, g, w_router, b_router, tri)


MOE_BLOCK = 512
PERM_W = 2 * LANES


def _row_dest(code_ref, start_ref, a):
    code = code_ref[a]
    return start_ref[code >> RANK_BITS] + (code & ((1 << RANK_BITS) - 1))


def _drain_rows(src_row, dst_row, sem, n):
    def body(t, c):
        pltpu.make_async_copy(src_row, dst_row, sem).wait()
        return c
    lax.fori_loop(0, n, body, 0, unroll=8)


def _dispatch_kernel(code_ref, start_ref, x_ref, zero_ref, xs_ref, sem, *, tm):
    del zero_ref
    base = pl.program_id(0) * (tm * TOP_K)

    def issue(t, c):
        for k in range(TOP_K):
            dst = _row_dest(code_ref, start_ref, base + t * TOP_K + k)
            pltpu.make_async_copy(x_ref.at[pl.ds(t, 1)], xs_ref.at[pl.ds(dst, 1)], sem).start()
        return c

    lax.fori_loop(0, tm, issue, 0, unroll=2)
    _drain_rows(x_ref.at[pl.ds(0, 1)], xs_ref.at[pl.ds(0, 1)], sem, tm * TOP_K)


def _dispatch_call(codes, starts, x, rows, tm=512):
    n, d = x.shape
    return pl.pallas_call(
        functools.partial(_dispatch_kernel, tm=tm),
        out_shape=jax.ShapeDtypeStruct((rows, d), F32),
        grid_spec=pltpu.PrefetchScalarGridSpec(
            num_scalar_prefetch=2, grid=(n // tm,),
            in_specs=[pl.BlockSpec((tm, d), lambda i, *_: (i, 0)),
                      pl.BlockSpec(memory_space=pl.ANY)],
            out_specs=pl.BlockSpec(memory_space=pl.ANY),
            scratch_shapes=[pltpu.SemaphoreType.DMA]),
        input_output_aliases={3: 0},
        compiler_params=_cparams(("arbitrary",)),
        name="moe_dispatch",
    )(codes, starts, x, jnp.zeros((rows, d), F32))


def _ffn_kernel(blk_e_ref, nb_ref, xs_ref, w1_ref, b1g_ref, b1l_ref, w2_ref, b2_ref, perm_ref,
                ys_ref, w1g_s, w1l_s, w2_s):
    i = pl.program_id(0)
    e = blk_e_ref[i]
    live = i < nb_ref[0]
    fresh = (i == 0) | (e != blk_e_ref[jnp.maximum(i - 1, 0)])

    @pl.when(live & fresh)
    def _():
        perm = perm_ref[...]
        for j in range(w1_ref.shape[1] // PERM_W):
            t = _mm(w1_ref[:, j * PERM_W:(j + 1) * PERM_W].astype(BF16), perm).astype(BF16)
            w1g_s[:, j * LANES:(j + 1) * LANES] = t[:, 0:LANES]
            w1l_s[:, j * LANES:(j + 1) * LANES] = t[:, LANES:PERM_W]
        w2_s[...] = w2_ref[...].astype(BF16)

    @pl.when(live)
    def _():
        x = xs_ref[...].astype(BF16)
        glu = jnp.minimum(_mm(x, w1g_s[...]) + b1g_ref[...], SWIGLU_LIMIT)
        lin = jnp.clip(_mm(x, w1l_s[...]) + b1l_ref[...], -SWIGLU_LIMIT, SWIGLU_LIMIT)
        act = glu * jax.nn.sigmoid(SWIGLU_ALPHA * glu) * (lin + 1.0)
        ys_ref[...] = _mm(act.astype(BF16), w2_s[...]) + b2_ref[...]

    @pl.when(jnp.logical_not(live))
    def _():
        ys_ref[...] = jnp.zeros(ys_ref.shape, F32)


def _ffn_call(blk_e, nb_used, xs, w1, b1g, b1l, w2, b2):
    rows, d = xs.shape
    ne, _, ff2 = w1.shape
    ff = ff2 // 2
    perm = np.zeros((PERM_W, PERM_W), np.float32)
    perm[2 * np.arange(LANES), np.arange(LANES)] = 1.0
    perm[2 * np.arange(LANES) + 1, LANES + np.arange(LANES)] = 1.0
    ex = lambda r, c: pl.BlockSpec((None, r, c), lambda i, be, nb: (be[i], 0, 0))
    return pl.pallas_call(
        _ffn_kernel,
        out_shape=jax.ShapeDtypeStruct((rows, d), F32),
        grid_spec=pltpu.PrefetchScalarGridSpec(
            num_scalar_prefetch=2, grid=(rows // MOE_BLOCK,),
            in_specs=[pl.BlockSpec((MOE_BLOCK, d), lambda i, *_: (i, 0)),
                      ex(d, ff2), ex(1, ff), ex(1, ff), ex(ff, d), ex(1, d),
                      pl.BlockSpec((PERM_W, PERM_W), lambda i, *_: (0, 0))],
            out_specs=pl.BlockSpec((MOE_BLOCK, d), lambda i, *_: (i, 0)),
            scratch_shapes=[pltpu.VMEM((d, ff), BF16), pltpu.VMEM((d, ff), BF16),
                            pltpu.VMEM((ff, d), BF16)]),
        compiler_params=_cparams(("arbitrary",)),
        name="moe_ffn",
    )(blk_e, nb_used, xs, w1, b1g, b1l, w2, b2, jnp.asarray(perm, BF16))


def _combine_kernel(code_ref, start_ref, gk_ref, h_ref, g2_ref, fg_ref, ys_ref, o_ref, buf, sem,
                    *, tm, final_norm):
    base = pl.program_id(0) * (tm * TOP_K)

    def issue(t, c):
        for k in range(TOP_K):
            src = _row_dest(code_ref, start_ref, base + t * TOP_K + k)
            pltpu.make_async_copy(ys_ref.at[pl.ds(src, 1)], buf.at[k, pl.ds(t, 1)], sem).start()
        return c

    lax.fori_loop(0, tm, issue, 0, unroll=2)
    _drain_rows(ys_ref.at[pl.ds(0, 1)], buf.at[0, pl.ds(0, 1)], sem, tm * TOP_K)

    gk = gk_ref[...]
    acc = buf[0] * gk[:, 0:1]
    for k in range(1, TOP_K):
        acc = acc + buf[k] * gk[:, k:k + 1]
    out = h_ref[...] + g2_ref[...] * acc
    if final_norm:
        out = out * lax.rsqrt(jnp.mean(out * out, axis=-1, keepdims=True) + EPS) * fg_ref[...]
    o_ref[...] = out


def _combine_call(codes, starts, gk, h1, g2, final_g, ys, seq, final_norm, tm=256):
    n, d = h1.shape
    return pl.pallas_call(
        functools.partial(_combine_kernel, tm=tm, final_norm=final_norm),
        out_shape=jax.ShapeDtypeStruct((n, d), F32),
        grid_spec=pltpu.PrefetchScalarGridSpec(
            num_scalar_prefetch=2, grid=(n // tm,),
            in_specs=[pl.BlockSpec((tm, LANES), lambda i, *_: (i, 0)),
                      pl.BlockSpec((tm, d), lambda i, *_: (i, 0)),
                      pl.BlockSpec((None, 1, d), lambda i, *_: ((i * tm) // seq, 0, 0)),
                      pl.BlockSpec((1, d), lambda i, *_: (0, 0)),
                      pl.BlockSpec(memory_space=pl.ANY)],
            out_specs=pl.BlockSpec((tm, d), lambda i, *_: (i, 0)),
            scratch_shapes=[pltpu.VMEM((TOP_K, tm, d), F32), pltpu.SemaphoreType.DMA]),
        compiler_params=_cparams(("arbitrary",)),
        name="moe_combine",
    )(codes, starts, gk, h1, g2, final_g, ys)


def _moe_call(u, code, gk, counts, h1, g2, w1, b1, w2, b2, final_g, final_norm):
    b, s, d = h1.shape
    n = b * s
    ne = w1.shape[0]
    rows = n * TOP_K + ne * MOE_BLOCK
    cnt = counts[0, :ne].astype(I32)
    padded = (cnt + MOE_BLOCK - 1) // MOE_BLOCK * MOE_BLOCK
    ends = jnp.cumsum(padded)
    starts = (ends - padded).astype(I32)
    blk_e = jnp.minimum(jnp.searchsorted(ends, jnp.arange(rows // MOE_BLOCK) * MOE_BLOCK,
                                         side='right'), ne - 1).astype(I32)
    nb_used = (ends[-1:] // MOE_BLOCK).astype(I32)
    codes = code[..., 0:TOP_K].reshape(n * TOP_K)

    xs = _dispatch_call(codes, starts, u.reshape(n, d), rows)
    ff = w1.shape[2] // 2
    ys = _ffn_call(blk_e, nb_used, xs, w1, b1[:, 0::2].reshape(ne, 1, ff),
                   b1[:, 1::2].reshape(ne, 1, ff), w2, b2.reshape(ne, 1, d))
    out = _combine_call(codes, starts, gk.reshape(n, LANES), h1.reshape(n, d), g2, final_g, ys, s,
                        final_norm)
    return out.reshape(b, s, d)


def _pack_w_in(w_in):
    d = w_in.shape[0]
    offs = np.cumsum((0,) + IN_SPLITS)
    head = w_in[:, :offs[5]]
    wi = w_in[:, offs[5]:offs[6]]
    pad = jnp.zeros((d, KIWI_W - IDX_DIM - IDX_HEADS), w_in.dtype)
    return jnp.concatenate([head, wi, pad, w_in[:, offs[6]:]], axis=1).astype(BF16)


def _block_diag(pool_w):
    g, c, _ = pool_w.shape
    out = jnp.zeros((g * c, g * c), pool_w.dtype)
    for j in range(g):
        out = out.at[j * c:(j + 1) * c, j * c:(j + 1) * c].set(pool_w[j])
    return out.astype(BF16)


def kernel(x, c, positions, w_ada, b_ada, norm1_g, norm2_g, w_in, w_out, pool_w, pool_scale,
           hgrn_norm_g, lb_logits, w_router, b_router, w1, b1, w2, b2, final_g):
    bsz, s, d = x.shape
    depth = w_ada.shape[0]
    mod = _ada_call(c, w_ada, b_ada)
    cos, sin = _rope_table_call(positions)
    h = x
    for l in range(depth):
        sh1, sc1, g1, sh2, sc2, g2 = [mod[l, :, j * d:(j + 1) * d].reshape(bsz, 1, d)
                                      for j in range(6)]
        q, k, v, qi, kiwi, rest = _inproj_call(
            h, sh1, sc1, norm1_g[l].reshape(1, d), _pack_w_in(w_in[l]), cos, sin)
        ya = _dsa_call(q, k, v, qi, kiwi)
        yb = _pool_call(rest, _block_diag(pool_w[l]), pool_scale[l].reshape(1, B_WIDTH))
        yc = _hgrn_call(rest, lb_logits, hgrn_norm_g[l], l)
        wr = jnp.pad(w_router[l], ((0, 0), (0, LANES - N_EXPERTS)))
        br = jnp.pad(b_router[l], (0, LANES - N_EXPERTS)).reshape(1, LANES)
        h1, u2, code, gk, counts = _outproj_router_call(
            ya, yb, yc, h, g1, w_out[l].astype(BF16), sh2, sc2, norm2_g[l].reshape(1, d), wr, br)
        h = _moe_call(u2, code, gk, counts, h1, g2, w1[l], b1[l], w2[l], b2[l],
                      final_g.reshape(1, d), final_norm=(l == depth - 1))
    return h
```

```python
import functools
import math

import numpy as np
import jax
import jax.numpy as jnp
from jax import lax
from jax.experimental import pallas as pl
from jax.experimental.pallas import tpu as pltpu

F32 = jnp.float32
BF16 = jnp.bfloat16
I32 = jnp.int32

CHUNK = 64
EPS = 1e-6
NEG_BIG = -1e30
A_HEADS, A_KV_HEADS, HEAD_DIM = 8, 2, 64
IDX_HEADS, IDX_DIM, IDX_TOPK_MAX = 4, 64, 256
Q_BLOCK = 128
ROPE_THETA = 10000.0
A_WIDTH = A_HEADS * HEAD_DIM
POOL_WINDOWS = (2, 4, 8, 16)
POOL_GROUP = 64
B_WIDTH = len(POOL_WINDOWS) * POOL_GROUP
C_HEADS, C_KDIM, C_VDIM = 4, 64, 64
C_WIDTH = C_HEADS * C_VDIM
N_EXPERTS, TOP_K = 32, 4
SWIGLU_LIMIT, SWIGLU_ALPHA = 7.0, 1.702
KV_WIDTH = A_KV_HEADS * HEAD_DIM
IN_SPLITS = (A_WIDTH, KV_WIDTH, KV_WIDTH, IDX_HEADS * IDX_DIM, IDX_DIM, IDX_HEADS,
             B_WIDTH, C_HEADS * C_KDIM, C_HEADS * C_KDIM, C_WIDTH, C_WIDTH)

LANES = 128
SUBLANES = 8
INT_MIN = -(2 ** 31)
VMEM_LIMIT = 56 * 1024 * 1024

KIWI_W = LANES
REST_W = B_WIDTH + 4 * C_WIDTH
PACK_W = A_WIDTH + 2 * KV_WIDTH + IDX_HEADS * IDX_DIM + KIWI_W + REST_W
HGRN_LEVELS = (32, 16, 8, 4, 2, 1)


def _nt(a, b):
    return lax.dot_general(a, b, (((1,), (1,)), ((), ())), preferred_element_type=F32)


def _tn(a, b):
    return lax.dot_general(a, b, (((0,), (0,)), ((), ())), preferred_element_type=F32)


def _mm(a, b):
    return jnp.dot(a, b, preferred_element_type=F32)


def _cparams(sem):
    return pltpu.CompilerParams(dimension_semantics=sem, vmem_limit_bytes=VMEM_LIMIT)


def _ada_kernel(c_ref, w_ref, b_ref, o_ref):
    c = c_ref[...]
    ca = c * jax.nn.sigmoid(c)
    o_ref[...] = jnp.dot(ca, w_ref[...], precision=lax.Precision.HIGHEST,
                         preferred_element_type=F32) + b_ref[...]


def _ada_call(c, w_ada, b_ada):
    depth, d, d6 = w_ada.shape
    b = c.shape[0]
    nblk = d6 // d
    return pl.pallas_call(
        _ada_kernel,
        out_shape=jax.ShapeDtypeStruct((depth, b, d6), F32),
        grid=(depth, nblk),
        in_specs=[pl.BlockSpec((b, d), lambda l, j: (0, 0)),
                  pl.BlockSpec((None, d, d), lambda l, j: (l, 0, j)),
                  pl.BlockSpec((None, 1, d), lambda l, j: (l, 0, j))],
        out_specs=pl.BlockSpec((None, b, d), lambda l, j: (l, 0, j)),
        compiler_params=_cparams(("arbitrary", "arbitrary")),
        name="ada_mod",
    )(c, w_ada, b_ada.reshape(depth, 1, d6))


def _rope_table_kernel(pos_ref, inv_ref, sign_ref, cos_ref, sin_ref):
    ang = pos_ref[...].astype(F32) * inv_ref[...]
    cos_ref[...] = jnp.cos(ang)
    sin_ref[...] = jnp.sin(ang) * sign_ref[...]


def _rope_table_call(positions, ts=512):
    b, s = positions.shape
    half = HEAD_DIM // 2
    inv = jnp.power(jnp.float32(ROPE_THETA), -jnp.arange(0, HEAD_DIM, 2, dtype=F32) / HEAD_DIM)
    inv128 = jnp.tile(inv, LANES // half).reshape(1, LANES)
    sign128 = jnp.tile(jnp.concatenate([-jnp.ones((half,), F32), jnp.ones((half,), F32)]),
                       LANES // HEAD_DIM).reshape(1, LANES)
    spec = pl.BlockSpec((None, ts, LANES), lambda bi, i: (bi, i, 0))
    return pl.pallas_call(
        _rope_table_kernel,
        out_shape=(jax.ShapeDtypeStruct((b, s, LANES), F32),) * 2,
        grid=(b, s // ts),
        in_specs=[pl.BlockSpec((None, ts, 1), lambda bi, i: (bi, i, 0)),
                  pl.BlockSpec((1, LANES), lambda bi, i: (0, 0)),
                  pl.BlockSpec((1, LANES), lambda bi, i: (0, 0))],
        out_specs=(spec, spec),
        compiler_params=_cparams(("arbitrary", "arbitrary")),
        name="rope_table",
    )(positions.reshape(b, s, 1), inv128, sign128)


def _rope_tile(x, cos, sin_signed, first_half):
    partner = jnp.where(first_half, pltpu.roll(x, LANES - HEAD_DIM // 2, 1),
                        pltpu.roll(x, HEAD_DIM // 2, 1))
    return x * cos + partner * sin_signed


def _ada_norm(x, g, sc, sh):
    y = x * lax.rsqrt(jnp.mean(x * x, axis=-1, keepdims=True) + EPS)
    return (y * g) * (1.0 + sc) + sh


VT_W = 256
WI_ROWS = SUBLANES


def _inproj_kernel(h_ref, sh_ref, sc_ref, g_ref, w_ref, cos_ref, sin_ref,
                   qt_ref, qit_ref, wit_ref, k_ref, ki_ref, vt_ref, rest_ref):
    u = _ada_norm(h_ref[...], g_ref[...], sc_ref[...], sh_ref[...]).astype(BF16)
    cos = cos_ref[...]
    sin = sin_ref[...]
    lane = lax.broadcasted_iota(I32, cos.shape, 1)
    first_half = (lane % HEAD_DIM) < (HEAD_DIM // 2)
    rope = lambda x: _rope_tile(x, cos, sin, first_half)

    off = 0
    q_scale = HEAD_DIM ** -0.5
    for j in range(A_WIDTH // LANES):
        z = _mm(u, w_ref[:, off + j * LANES: off + (j + 1) * LANES])
        qt_ref[j * LANES:(j + 1) * LANES, :] = (rope(z) * q_scale).T.astype(BF16)
    off += A_WIDTH
    k_ref[...] = rope(_mm(u, w_ref[:, off:off + KV_WIDTH])).astype(BF16)
    off += KV_WIDTH
    vt_ref[...] = _mm(u, w_ref[:, off:off + KV_WIDTH]).T.astype(BF16)
    off += KV_WIDTH
    for j in range(IDX_HEADS * IDX_DIM // LANES):
        z = _mm(u, w_ref[:, off + j * LANES: off + (j + 1) * LANES])
        qit_ref[j * LANES:(j + 1) * LANES, :] = rope(z).T.astype(BF16)
    off += IDX_HEADS * IDX_DIM
    z = _mm(u, w_ref[:, off:off + KIWI_W])
    ki_ref[...] = rope(z).astype(BF16)
    wi_scale = (IDX_HEADS * IDX_DIM) ** -0.5
    wit_ref[...] = (z * wi_scale).T[IDX_DIM:IDX_DIM + WI_ROWS, :]
    off += KIWI_W
    rest_ref[...] = _mm(u, w_ref[:, off:off + REST_W])


def _inproj_call(h, sh, sc, g, w_pack, cos, sin):
    b, s, d = h.shape
    tm = VT_W
    tok = lambda w: pl.BlockSpec((None, tm, w), lambda bi, i: (bi, i, 0))
    tr = lambda r: pl.BlockSpec((None, r, tm), lambda bi, i: (bi, 0, i))
    per_b = pl.BlockSpec((None, 1, d), lambda bi, i: (bi, 0, 0))
    sds = jax.ShapeDtypeStruct
    return pl.pallas_call(
        _inproj_kernel,
        out_shape=(sds((b, A_WIDTH, s), BF16), sds((b, IDX_HEADS * IDX_DIM, s), BF16),
                   sds((b, WI_ROWS, s), F32), sds((b, s, KV_WIDTH), BF16),
                   sds((b, s, KIWI_W), BF16), sds((b, s // tm, KV_WIDTH, tm), BF16),
                   sds((b, s, REST_W), F32)),
        grid=(b, s // tm),
        in_specs=[tok(d), per_b, per_b,
                  pl.BlockSpec((1, d), lambda bi, i: (0, 0)),
                  pl.BlockSpec((d, PACK_W), lambda bi, i: (0, 0)),
                  tok(LANES), tok(LANES)],
        out_specs=(tr(A_WIDTH), tr(IDX_HEADS * IDX_DIM), tr(WI_ROWS), tok(KV_WIDTH), tok(KIWI_W),
                   pl.BlockSpec((None, None, KV_WIDTH, tm), lambda bi, i: (bi, i, 0, 0)),
                   tok(REST_W)),
        compiler_params=_cparams(("arbitrary", "arbitrary")),
        name="inproj",
    )(h, sh, sc, g, w_pack, cos, sin)


CNT_ROWS = 32
ONES_ROWS = 16


def _dsa_kernel(qt_ref, qit_ref, wit_ref, k_ref, ki_ref, vt_ref, lower_ref, o_ref,
                keys_ref, bias_ref, s_ref, acc_ref, *, kc, topk):
    qb = Q_BLOCK
    i = pl.program_id(1)
    t0 = i * qb
    nkc = (t0 + qb + kc - 1) // kc
    lane = lax.broadcasted_iota(I32, (1, qb), 1)
    limit = t0 + (lane // CHUNK + 1) * CHUNK
    key_off = lax.broadcasted_iota(I32, (kc, qb), 0)
    group = A_HEADS // A_KV_HEADS

    qit = qit_ref[...]
    qi_stack = jnp.concatenate(
        [qit[h * IDX_DIM:(h + 1) * IDX_DIM, :] for h in range(IDX_HEADS)], axis=1)
    wit = wit_ref[...]

    def score_body(c, carry):
        off = pl.multiple_of(c * kc, kc)
        ki = ki_ref[pl.ds(off, kc), 0:IDX_DIM]
        s = jnp.maximum(_mm(ki, qi_stack), 0.0)
        score = s[:, 0:qb] * wit[0:1, :]
        for h in range(1, IDX_HEADS):
            score = score + s[:, h * qb:(h + 1) * qb] * wit[h:h + 1, :]
        score = jnp.where(score == 0.0, 0.0, score)
        bits = lax.bitcast_convert_type(score, I32)
        key = jnp.where(bits < 0, bits ^ jnp.int32(0x7FFFFFFF), bits)
        keys_ref[pl.ds(off, kc), :] = jnp.where(off + key_off < limit, key, jnp.int32(INT_MIN))
        return carry

    lax.fori_loop(0, nkc, score_body, 0)

    def count(pred):
        def body(c, acc):
            off = pl.multiple_of(c * kc, kc)
            hit = jnp.where(pred(keys_ref[pl.ds(off, kc), :]), 1.0, 0.0)
            return acc + jnp.sum(hit.reshape(kc // CNT_ROWS, CNT_ROWS, qb), axis=0)
        acc = lax.fori_loop(0, nkc, body, jnp.zeros((CNT_ROWS, qb), F32))
        return jnp.sum(acc, axis=0, keepdims=True)

    topk_f = float(topk)
    cnt0 = count(lambda kk: kk >= 0)
    cnt1 = count(lambda kk: kk >= 1)
    nonneg = cnt0 >= topk_f
    thr0 = jnp.where(nonneg, 0, INT_MIN).astype(I32)
    cthr0 = jnp.where(nonneg, cnt0, 2.0 * kc * (nkc + 1).astype(F32))
    done0 = jnp.where((nonneg & (cnt1 < topk_f)) | (cthr0 == topk_f) | (limit <= topk), 1.0, 0.0)
    steps = 4

    def thr_cond(state):
        g, _, _, done = state
        return (g * steps < 31) & (jnp.min(done) < 0.5)

    def thr_body(state):
        g, thr, cthr, done = state
        for jj in range(steps):
            b = 30 - (g * steps + jj)
            bit = jnp.where(b >= 0, jnp.left_shift(jnp.int32(1), jnp.maximum(b, 0)), 0)
            cand = thr + bit
            cnt = count(lambda kk: kk >= cand)
            take = (cnt >= topk_f) & (done < 0.5)
            thr = jnp.where(take, cand, thr)
            cthr = jnp.where(take, cnt, cthr)
            done = jnp.where(cthr == topk_f, 1.0, done)
        return g + 1, thr, cthr, done

    _, thr, _, _ = lax.while_loop(thr_cond, thr_body, (jnp.int32(0), thr0, cthr0, done0))

    need = topk_f - count(lambda kk: kk > thr)
    lower = lower_ref[...]

    def bias_body(c, seen):
        off = pl.multiple_of(c * kc, kc)
        kk = keys_ref[pl.ds(off, kc), :]
        tie = kk == thr
        tie_f = jnp.where(tie, 1.0, 0.0)
        rank = _mm(lower, tie_f.astype(BF16)) + seen
        sel = ((kk > thr) | (tie & (rank <= need))) & (off + key_off < limit)
        bias_ref[pl.ds(off, kc), :] = jnp.where(sel, 0.0, NEG_BIG)
        return seen + jnp.sum(tie_f, axis=0, keepdims=True)

    lax.fori_loop(0, nkc, bias_body, jnp.zeros((1, qb), F32))

    qt = qt_ref[...]
    q_n = [jnp.concatenate([qt[(n * group + g) * HEAD_DIM:(n * group + g + 1) * HEAD_DIM, :]
                            for g in range(group)], axis=1) for n in range(A_KV_HEADS)]

    def max_body(c, parts):
        out = list(parts)
        for j in range(kc // VT_W):
            off = pl.multiple_of(c * kc + j * VT_W, VT_W)
            bias = bias_ref[pl.ds(off, VT_W), :]
            for n in range(A_KV_HEADS):
                s = _mm(k_ref[pl.ds(off, VT_W), n * HEAD_DIM:(n + 1) * HEAD_DIM], q_n[n])
                s = jnp.concatenate([s[:, g * qb:(g + 1) * qb] + bias for g in range(group)], axis=1)
                s_ref[c * (kc // VT_W) + j, n] = s
                out[n] = jnp.maximum(out[n], jnp.max(
                    s.reshape(VT_W // SUBLANES, SUBLANES, group * qb), axis=0))
        return tuple(out)

    parts = lax.fori_loop(
        0, nkc, max_body,
        tuple(jnp.full((SUBLANES, group * qb), NEG_BIG, F32) for _ in range(A_KV_HEADS)))
    m_n = [jnp.max(p, axis=0, keepdims=True) for p in parts]

    acc_ref[...] = jnp.zeros(acc_ref.shape, F32)
    ones = jnp.ones((ONES_ROWS, VT_W), BF16)

    def pv_body(c, carry):
        for j in range(kc // VT_W):
            cv = c * (kc // VT_W) + j
            for n in range(A_KV_HEADS):
                pt = jnp.exp(s_ref[cv, n] - m_n[n]).astype(BF16)
                vt = jnp.concatenate([vt_ref[cv, n * HEAD_DIM:(n + 1) * HEAD_DIM, :], ones], axis=0)
                acc_ref[n] += _mm(vt, pt)
        return carry

    lax.fori_loop(0, nkc, pv_body, 0)
    for n in range(A_KV_HEADS):
        for g in range(group):
            hh = n * group + g
            cols = slice(g * qb, (g + 1) * qb)
            o_ref[hh * HEAD_DIM:(hh + 1) * HEAD_DIM, :] = (
                acc_ref[n, 0:HEAD_DIM, cols] / acc_ref[n, HEAD_DIM:HEAD_DIM + 1, cols])


def _dsa_call(qt, qit, wit, k, ki, vt, kc=512):
    b, _, s = qt.shape
    kc = min(kc, s)
    assert kc % VT_W == 0 and s % kc == 0
    topk = min(IDX_TOPK_MAX, s // 4)
    group = A_HEADS // A_KV_HEADS
    lower = jnp.asarray(np.tril(np.ones((kc, kc), np.float32)), BF16)
    qblk = lambda r: pl.BlockSpec((None, r, Q_BLOCK), lambda bi, i: (bi, 0, i))
    seq = lambda w: pl.BlockSpec((None, s, w), lambda bi, i: (bi, 0, 0))
    return pl.pallas_call(
        functools.partial(_dsa_kernel, kc=kc, topk=topk),
        out_shape=jax.ShapeDtypeStruct((b, A_WIDTH, s), F32),
        grid=(b, s // Q_BLOCK),
        in_specs=[qblk(A_WIDTH), qblk(IDX_HEADS * IDX_DIM), qblk(WI_ROWS),
                  seq(KV_WIDTH), seq(KIWI_W),
                  pl.BlockSpec((None, s // VT_W, KV_WIDTH, VT_W), lambda bi, i: (bi, 0, 0, 0)),
                  pl.BlockSpec((kc, kc), lambda bi, i: (0, 0))],
        out_specs=qblk(A_WIDTH),
        scratch_shapes=[pltpu.VMEM((s, Q_BLOCK), I32),
                        pltpu.VMEM((s, Q_BLOCK), F32),
                        pltpu.VMEM((s // VT_W, A_KV_HEADS, VT_W, group * Q_BLOCK), F32),
                        pltpu.VMEM((A_KV_HEADS, HEAD_DIM + ONES_ROWS, group * Q_BLOCK), F32)],
        compiler_params=_cparams(("arbitrary", "arbitrary")),
        name="dsa",
    )(qt, qit, wit, k, ki, vt, lower)


POOL_HALO = 32


def _pool_kernel(u_ref, w_ref, scale_ref, o_ref, x_buf, a_buf, b_buf, *, tm):
    hl = POOL_HALO
    rows = tm + hl
    first = pl.program_id(1) == 0

    @pl.when(first)
    def _():
        x_buf[0:hl, :] = jnp.zeros((hl, B_WIDTH), F32)

    x = u_ref[...]
    x_buf[hl:rows, :] = x
    a_buf[8:rows, :] = x_buf[8:rows, :] + x_buf[7:rows - 1, :]
    b_buf[16:rows, :] = a_buf[16:rows, :] + a_buf[14:rows - 2, :]
    w2 = a_buf[hl:rows, :]
    w4 = b_buf[hl:rows, :]
    a_buf[24:rows, :] = b_buf[24:rows, :] + b_buf[20:rows - 4, :]
    w8 = a_buf[hl:rows, :]
    b_buf[hl:rows, :] = a_buf[hl:rows, :] + a_buf[hl - 8:rows - 8, :]
    w16 = b_buf[hl:rows, :]
    x_buf[0:hl, :] = x_buf[tm:rows, :]

    lane = lax.broadcasted_iota(I32, (tm, B_WIDTH), 1)
    grp = lane // POOL_GROUP
    wsum = jnp.where(grp == 0, w2, jnp.where(grp == 1, w4, jnp.where(grp == 2, w8, w16)))
    win = jnp.where(grp == 0, 2, jnp.where(grp == 1, 4, jnp.where(grp == 2, 8, 16)))
    t = pl.program_id(1) * tm + lax.broadcasted_iota(I32, (tm, B_WIDTH), 0)
    cnt = jnp.minimum(t + 1, win).astype(F32)
    pooled = wsum / cnt - x
    y = _mm(pooled.astype(BF16), w_ref[...])
    o_ref[...] = y * scale_ref[...]


def _pool_call(rest, w_bd, scale, tm=512):
    b, s, _ = rest.shape
    tm = min(tm, s)
    rows = tm + POOL_HALO
    return pl.pallas_call(
        functools.partial(_pool_kernel, tm=tm),
        out_shape=jax.ShapeDtypeStruct((b, s, B_WIDTH), F32),
        grid=(b, s // tm),
        in_specs=[pl.BlockSpec((None, tm, B_WIDTH), lambda bi, i: (bi, i, 0)),
                  pl.BlockSpec((B_WIDTH, B_WIDTH), lambda bi, i: (0, 0)),
                  pl.BlockSpec((1, B_WIDTH), lambda bi, i: (0, 0))],
        out_specs=pl.BlockSpec((None, tm, B_WIDTH), lambda bi, i: (bi, i, 0)),
        scratch_shapes=[pltpu.VMEM((rows, B_WIDTH), F32)] * 3,
        compiler_params=_cparams(("arbitrary", "arbitrary")),
        name="pool",
    )(rest, w_bd, scale)


def _hgrn_consts():
    tril = np.tril(np.ones((CHUNK, CHUNK), np.float32))
    mats = [tril]
    r = np.arange(CHUNK)
    for h in HGRN_LEVELS:
        mats.append(tril[(r // (2 * h)) * (2 * h) + h - 1])
    return np.concatenate(mats, axis=0)


def _split3(x):
    hi = x.astype(BF16)
    r1 = x - hi.astype(F32)
    mid = r1.astype(BF16)
    lo = (r1 - mid.astype(F32)).astype(BF16)
    return hi, mid, lo


def _hgrn_kernel(q_ref, f_ref, i_ref, g_ref, lb_ref, ng_ref, cm_ref, o_ref, state_ref,
                 *, layer, tm):
    @pl.when(pl.program_id(1) == 0)
    def _():
        state_ref[...] = jnp.zeros(state_ref.shape, F32)

    lbl = lb_ref[...]
    e = jnp.exp(lbl - jnp.max(lbl, axis=0, keepdims=True))
    p = e / jnp.sum(e, axis=0, keepdims=True)
    cum = p[0:1]
    for l in range(1, layer + 1):
        cum = cum + p[l:l + 1]
    lb = jnp.clip(cum - p[0:1], 0.0, 1.0)

    cm = cm_ref[...]
    ng = ng_ref[...]
    row = lax.broadcasted_iota(I32, (CHUNK, 1), 0)
    tt = lax.broadcasted_iota(I32, (CHUNK, CHUNK), 0)
    ss = lax.broadcasted_iota(I32, (CHUNK, CHUNK), 1)
    w = C_WIDTH

    def chunk_body(ci, carry):
        r0 = pl.multiple_of(ci * CHUNK, CHUNK)
        z = f_ref[pl.ds(r0, CHUNK), :]
        f = lb + (1.0 - lb) * jax.nn.sigmoid(z)
        log_f = jnp.log(f)
        kin = (1.0 - lb) * jax.nn.sigmoid(-z)
        qx = q_ref[pl.ds(r0, CHUNK), :]
        qv = qx * jax.nn.sigmoid(qx)
        vv = i_ref[pl.ds(r0, CHUNK), :]
        hi, mid, lo = _split3(log_f)
        cs = _mm(cm, jnp.concatenate([hi, mid, lo], axis=1))
        cs = cs[:, 0:w] + cs[:, w:2 * w] + cs[:, 2 * w:3 * w]
        bcum = cs[0:CHUNK]
        b_last = bcum[CHUNK - 1:CHUNK]
        q_dec = (qv * jnp.exp(bcum)).astype(BF16)
        k_dec = (kin * jnp.exp(b_last - bcum)).astype(BF16)
        s_dec = jnp.exp(b_last)
        qb16, kb16, vb16 = qv.astype(BF16), kin.astype(BF16), vv.astype(BF16)
        q_lv, k_lv = [], []
        for li, h in enumerate(HGRN_LEVELS):
            ref = cs[(li + 1) * CHUNK:(li + 2) * CHUNK]
            odd = ((row // h) % 2) == 1
            q_lv.append((qv * jnp.exp(jnp.where(odd, bcum - ref, NEG_BIG))).astype(BF16))
            k_lv.append((kin * jnp.exp(jnp.where(odd, NEG_BIG, ref - bcum))).astype(BF16))
        outs = []
        for hd in range(C_HEADS):
            sl = slice(hd * C_KDIM, (hd + 1) * C_KDIM)
            attn = jnp.where(tt == ss, _nt(qb16[:, sl], kb16[:, sl]), 0.0)
            for li, h in enumerate(HGRN_LEVELS):
                a = _nt(q_lv[li][:, sl], k_lv[li][:, sl])
                attn = attn + jnp.where((tt // (2 * h)) == (ss // (2 * h)), a, 0.0)
            st = state_ref[hd]
            o = _nt(q_dec[:, sl], st.astype(BF16)) + _mm(attn.astype(BF16), vb16[:, sl])
            state_ref[hd] = st * s_dec[:, sl] + _tn(vb16[:, sl], k_dec[:, sl])
            o = o * lax.rsqrt(jnp.mean(o * o, axis=-1, keepdims=True) + EPS) * ng
            outs.append(o)
        gx = g_ref[pl.ds(r0, CHUNK), :]
        o_ref[pl.ds(r0, CHUNK), :] = jnp.concatenate(outs, axis=1) * (gx * jax.nn.sigmoid(gx))
        return carry

    lax.fori_loop(0, tm // CHUNK, chunk_body, 0)


def _hgrn_call(rest, lb_logits, norm_g, layer, tm=256):
    b, s, _ = rest.shape
    depth = lb_logits.shape[0]
    cm = jnp.asarray(_hgrn_consts(), BF16)
    col = lambda j: pl.BlockSpec((None, tm, C_WIDTH), lambda bi, i, j=j: (bi, i, j))
    return pl.pallas_call(
        functools.partial(_hgrn_kernel, layer=layer, tm=tm),
        out_shape=jax.ShapeDtypeStruct((b, s, C_WIDTH), F32),
        grid=(b, s // tm),
        in_specs=[col(1), col(2), col(3), col(4),
                  pl.BlockSpec((depth, C_WIDTH), lambda bi, i: (0, 0)),
                  pl.BlockSpec((1, C_VDIM), lambda bi, i: (0, 0)),
                  pl.BlockSpec(cm.shape, lambda bi, i: (0, 0))],
        out_specs=pl.BlockSpec((None, tm, C_WIDTH), lambda bi, i: (bi, i, 0)),
        scratch_shapes=[pltpu.VMEM((C_HEADS, C_VDIM, C_KDIM), F32)],
        compiler_params=_cparams(("arbitrary", "arbitrary")),
        name="hgrn2",
    )(rest, rest, rest, rest, lb_logits, norm_g.reshape(1, C_VDIM), cm)


RANK_BITS = 20
CODE_ROWS = 8


def _outproj_router_kernel(ya_ref, yb_ref, yc_ref, h_ref, g1_ref, wo_ref, sh_ref, sc_ref, g_ref,
                           wr_ref, br_ref, tri_ref, h1_ref, u_ref, code_ref, gk_ref, cnt_ref,
                           run_ref):
    @pl.when((pl.program_id(0) == 0) & (pl.program_id(1) == 0))
    def _():
        run_ref[...] = jnp.zeros(run_ref.shape, F32)

    y = _tn(ya_ref[...].astype(BF16), wo_ref[0:A_WIDTH, :])
    y = y + _mm(yb_ref[...].astype(BF16), wo_ref[A_WIDTH:A_WIDTH + B_WIDTH, :])
    y = y + _mm(yc_ref[...].astype(BF16), wo_ref[A_WIDTH + B_WIDTH:, :])
    h1 = h_ref[...] + g1_ref[...] * y
    h1_ref[...] = h1
    u = _ada_norm(h1, g_ref[...], sc_ref[...], sh_ref[...])
    u_ref[...] = u

    logits = jnp.dot(u, wr_ref[...], precision=lax.Precision.HIGHEST,
                     preferred_element_type=F32) + br_ref[...]
    lane = lax.broadcasted_iota(I32, logits.shape, 1).astype(F32)
    work = jnp.where(lane < N_EXPERTS, logits, -jnp.inf)
    picks, firsts, tops = [], [], []
    for k in range(TOP_K):
        m = jnp.max(work, axis=1, keepdims=True)
        first = jnp.min(jnp.where(work == m, lane, float(LANES)), axis=1, keepdims=True)
        pick = lane == first
        work = jnp.where(pick, -jnp.inf, work)
        picks.append(pick)
        firsts.append(first)
        tops.append(m)
    ex = [jnp.exp(m - tops[0]) for m in tops]
    den = ex[0] + ex[1] + ex[2] + ex[3]

    sel = picks[0] | picks[1] | picks[2] | picks[3]
    sel_f = jnp.where(sel, 1.0, 0.0)
    prefix = _mm(tri_ref[...], sel_f.astype(BF16)) + run_ref[...]
    run_ref[...] += jnp.sum(sel_f, axis=0, keepdims=True)
    cnt_ref[...] = run_ref[...]

    code = jnp.zeros(logits.shape, I32)
    gk = jnp.zeros(logits.shape, F32)
    for k in range(TOP_K):
        rank = jnp.sum(jnp.where(picks[k], prefix, 0.0), axis=1, keepdims=True)
        ck = (firsts[k].astype(I32) << RANK_BITS) | rank.astype(I32)
        code = jnp.where(lane == float(k), ck, code)
        gk = jnp.where(lane == float(k), ex[k] / den, gk)
    code_ref[...] = code.T[0:CODE_ROWS, :]
    gk_ref[...] = gk


def _outproj_router_call(ya, yb, yc, h, g1, w_out, sh, sc, g, w_router, b_router, tm=512):
    b, s, d = h.shape
    tm = min(tm, s)
    tri = jnp.asarray(np.tril(np.ones((tm, tm), np.float32), -1), BF16)
    tok = lambda w: pl.BlockSpec((None, tm, w), lambda bi, i: (bi, i, 0))
    per_b = pl.BlockSpec((None, 1, d), lambda bi, i: (bi, 0, 0))
    full = lambda a: pl.BlockSpec(a.shape, lambda bi, i: (0,) * a.ndim)
    return pl.pallas_call(
        _outproj_router_kernel,
        out_shape=(jax.ShapeDtypeStruct((b, s, d), F32), jax.ShapeDtypeStruct((b, s, d), F32),
                   jax.ShapeDtypeStruct((CODE_ROWS, b * s), I32),
                   jax.ShapeDtypeStruct((b, s, LANES), F32), jax.ShapeDtypeStruct((1, LANES), F32)),
        grid=(b, s // tm),
        in_specs=[pl.BlockSpec((None, A_WIDTH, tm), lambda bi, i: (bi, 0, i)),
                  tok(B_WIDTH), tok(C_WIDTH), tok(d), per_b, full(w_out),
                  per_b, per_b, full(g), full(w_router), full(b_router), full(tri)],
        out_specs=(tok(d), tok(d),
                   pl.BlockSpec((CODE_ROWS, tm), lambda bi, i: (0, bi * (s // tm) + i)),
                   tok(LANES), pl.BlockSpec((1, LANES), lambda bi, i: (0, 0))),
        scratch_shapes=[pltpu.VMEM((1, LANES), F32)],
        compiler_params=_cparams(("arbitrary", "arbitrary")),
        name="outproj_router",
    )(ya, yb, yc, h, g1, w_out, sh, sc, g, w_router, b_router, tri)


MOE_BLOCK = 512
PERM_W = 2 * LANES


def _row_dest(code_ref, start_ref, k, t):
    code = code_ref[k, t]
    return start_ref[code >> RANK_BITS] + (code & ((1 << RANK_BITS) - 1))


def _drain_rows(src_row, dst_row, sem, n):
    def body(t, c):
        pltpu.make_async_copy(src_row, dst_row, sem).wait()
        return c
    lax.fori_loop(0, n, body, 0, unroll=4)


ZERO_ROWS = MOE_BLOCK // 2


def _dispatch_kernel(code_ref, start_ref, padlo_ref, padhi_ref, x_ref, xs_ref, zbuf, sem, zsem,
                     *, tm):
    tok0 = pl.program_id(0) * tm

    @pl.when(pl.program_id(0) == 0)
    def _():
        zbuf[...] = jnp.zeros(zbuf.shape, F32)

        def pad_copies(e, fn):
            lo = padlo_ref[e]
            hi = padhi_ref[e]
            lo8 = jnp.minimum((lo + SUBLANES - 1) // SUBLANES * SUBLANES, hi)
            for j in range(SUBLANES - 1):

                @pl.when(lo + j < lo8)
                def _(j=j):
                    fn(pltpu.make_async_copy(zbuf.at[pl.ds(0, 1)], xs_ref.at[pl.ds(lo + j, 1)], zsem))
            n8 = hi - lo8
            size = ZERO_ROWS
            while size >= SUBLANES:
                off = pl.multiple_of(lo8 + (n8 & ~(2 * size - 1)), SUBLANES)

                @pl.when((n8 & size) != 0)
                def _(size=size, off=off):
                    fn(pltpu.make_async_copy(zbuf.at[pl.ds(0, size)], xs_ref.at[pl.ds(off, size)],
                                             zsem))
                size //= 2

        def tail_copies(fn):
            tail_lo = padhi_ref[N_EXPERTS - 1]

            def piece(j, c):
                off = pl.multiple_of(tail_lo + j * ZERO_ROWS, ZERO_ROWS)

                @pl.when(off < xs_ref.shape[0])
                def _():
                    fn(pltpu.make_async_copy(zbuf, xs_ref.at[pl.ds(off, ZERO_ROWS)], zsem))
                return c

            lax.fori_loop(0, N_EXPERTS * MOE_BLOCK // ZERO_ROWS, piece, 0)

        def start_e(e, c):
            pad_copies(e, lambda cp: cp.start())
            return c

        def wait_e(e, c):
            pad_copies(e, lambda cp: cp.wait())
            return c

        lax.fori_loop(0, N_EXPERTS, start_e, 0)
        tail_copies(lambda cp: cp.start())
        lax.fori_loop(0, N_EXPERTS, wait_e, 0)
        tail_copies(lambda cp: cp.wait())

    def issue(t, c):
        for k in range(TOP_K):
            dst = _row_dest(code_ref, start_ref, k, tok0 + t)
            pltpu.make_async_copy(x_ref.at[pl.ds(t, 1)], xs_ref.at[pl.ds(dst, 1)], sem).start()
        return c

    lax.fori_loop(0, tm, issue, 0, unroll=2)
    _drain_rows(x_ref.at[pl.ds(0, 1)], xs_ref.at[pl.ds(0, 1)], sem, tm * TOP_K)


def _dispatch_call(codes, starts, pad_lo, pad_hi, x, rows, tm=512):
    n, d = x.shape
    return pl.pallas_call(
        functools.partial(_dispatch_kernel, tm=tm),
        out_shape=jax.ShapeDtypeStruct((rows, d), F32),
        grid_spec=pltpu.PrefetchScalarGridSpec(
            num_scalar_prefetch=4, grid=(n // tm,),
            in_specs=[pl.BlockSpec((tm, d), lambda i, *_: (i, 0))],
            out_specs=pl.BlockSpec(memory_space=pl.ANY),
            scratch_shapes=[pltpu.VMEM((ZERO_ROWS, d), F32), pltpu.SemaphoreType.DMA,
                            pltpu.SemaphoreType.DMA]),
        compiler_params=_cparams(("arbitrary",)),
        name="moe_dispatch",
    )(codes, starts, pad_lo, pad_hi, x)


def _ffn_kernel(blk_e_ref, nb_ref, xs_ref, w1_ref, b1g_ref, b1l_ref, w2_ref, b2_ref, perm_ref,
                ys_ref, w1g_s, w1l_s, w2_s):
    i = pl.program_id(0)
    e = blk_e_ref[i]
    live = i < nb_ref[0]
    fresh = (i == 0) | (e != blk_e_ref[jnp.maximum(i - 1, 0)])

    @pl.when(live & fresh)
    def _():
        perm = perm_ref[...]
        for j in range(w1_ref.shape[1] // PERM_W):
            t = _mm(w1_ref[:, j * PERM_W:(j + 1) * PERM_W].astype(BF16), perm).astype(BF16)
            w1g_s[:, j * LANES:(j + 1) * LANES] = t[:, 0:LANES]
            w1l_s[:, j * LANES:(j + 1) * LANES] = t[:, LANES:PERM_W]
        w2_s[...] = w2_ref[...].astype(BF16)

    @pl.when(live)
    def _():
        x = xs_ref[...].astype(BF16)
        glu = jnp.minimum(_mm(x, w1g_s[...]) + b1g_ref[...], SWIGLU_LIMIT)
        lin = jnp.clip(_mm(x, w1l_s[...]) + b1l_ref[...], -SWIGLU_LIMIT, SWIGLU_LIMIT)
        act = glu * jax.nn.sigmoid(SWIGLU_ALPHA * glu) * (lin + 1.0)
        ys_ref[...] = _mm(act.astype(BF16), w2_s[...]) + b2_ref[...]

    @pl.when(jnp.logical_not(live))
    def _():
        ys_ref[...] = jnp.zeros(ys_ref.shape, F32)


def _ffn_call(blk_e, nb_used, xs, w1, b1g, b1l, w2, b2):
    rows, d = xs.shape
    ne, _, ff2 = w1.shape
    ff = ff2 // 2
    perm = np.zeros((PERM_W, PERM_W), np.float32)
    perm[2 * np.arange(LANES), np.arange(LANES)] = 1.0
    perm[2 * np.arange(LANES) + 1, LANES + np.arange(LANES)] = 1.0
    ex = lambda r, c: pl.BlockSpec((None, r, c), lambda i, be, nb: (be[i], 0, 0))
    return pl.pallas_call(
        _ffn_kernel,
        out_shape=jax.ShapeDtypeStruct((rows, d), F32),
        grid_spec=pltpu.PrefetchScalarGridSpec(
            num_scalar_prefetch=2, grid=(rows // MOE_BLOCK,),
            in_specs=[pl.BlockSpec((MOE_BLOCK, d),
                                   lambda i, be, nb: (jnp.minimum(i, jnp.maximum(nb[0] - 1, 0)), 0)),
                      ex(d, ff2), ex(1, ff), ex(1, ff), ex(ff, d), ex(1, d),
                      pl.BlockSpec((PERM_W, PERM_W), lambda i, *_: (0, 0))],
            out_specs=pl.BlockSpec((MOE_BLOCK, d), lambda i, *_: (i, 0)),
            scratch_shapes=[pltpu.VMEM((d, ff), BF16), pltpu.VMEM((d, ff), BF16),
                            pltpu.VMEM((ff, d), BF16)]),
        compiler_params=_cparams(("arbitrary",)),
        name="moe_ffn",
    )(blk_e, nb_used, xs, w1, b1g, b1l, w2, b2, jnp.asarray(perm, BF16))


def _combine_kernel(code_ref, start_ref, gk_ref, h_ref, g2_ref, fg_ref, ys_ref, o_ref, buf, sem,
                    *, tm, final_norm):
    tok0 = pl.program_id(0) * tm

    def issue(t, c):
        for k in range(TOP_K):
            src = _row_dest(code_ref, start_ref, k, tok0 + t)
            pltpu.make_async_copy(ys_ref.at[pl.ds(src, 1)], buf.at[k, pl.ds(t, 1)], sem).start()
        return c

    lax.fori_loop(0, tm, issue, 0, unroll=2)
    _drain_rows(ys_ref.at[pl.ds(0, 1)], buf.at[0, pl.ds(0, 1)], sem, tm * TOP_K)

    gk = gk_ref[...]
    acc = buf[0] * gk[:, 0:1]
    for k in range(1, TOP_K):
        acc = acc + buf[k] * gk[:, k:k + 1]
    out = h_ref[...] + g2_ref[...] * acc
    if final_norm:
        out = out * lax.rsqrt(jnp.mean(out * out, axis=-1, keepdims=True) + EPS) * fg_ref[...]
    o_ref[...] = out


def _combine_call(codes, starts, gk, h1, g2, final_g, ys, seq, final_norm, tm=256):
    n, d = h1.shape
    return pl.pallas_call(
        functools.partial(_combine_kernel, tm=tm, final_norm=final_norm),
        out_shape=jax.ShapeDtypeStruct((n, d), F32),
        grid_spec=pltpu.PrefetchScalarGridSpec(
            num_scalar_prefetch=2, grid=(n // tm,),
            in_specs=[pl.BlockSpec((tm, LANES), lambda i, *_: (i, 0)),
                      pl.BlockSpec((tm, d), lambda i, *_: (i, 0)),
                      pl.BlockSpec((None, 1, d), lambda i, *_: ((i * tm) // seq, 0, 0)),
                      pl.BlockSpec((1, d), lambda i, *_: (0, 0)),
                      pl.BlockSpec(memory_space=pl.ANY)],
            out_specs=pl.BlockSpec((tm, d), lambda i, *_: (i, 0)),
            scratch_shapes=[pltpu.VMEM((TOP_K, tm, d), F32), pltpu.SemaphoreType.DMA]),
        compiler_params=_cparams(("arbitrary",)),
        name="moe_combine",
    )(codes, starts, gk, h1, g2, final_g, ys)


def _moe_call(u, code, gk, counts, h1, g2, w1, b1, w2, b2, final_g, final_norm):
    b, s, d = h1.shape
    n = b * s
    ne = w1.shape[0]
    rows = n * TOP_K + ne * MOE_BLOCK
    cnt = counts[0, :ne].astype(I32)
    padded = (cnt + MOE_BLOCK - 1) // MOE_BLOCK * MOE_BLOCK
    ends = jnp.cumsum(padded)
    starts = (ends - padded).astype(I32)
    blk_start = jnp.arange(rows // MOE_BLOCK, dtype=I32) * MOE_BLOCK
    blk_e = jnp.minimum(jnp.sum((ends[None, :] <= blk_start[:, None]).astype(I32), axis=1), ne - 1)
    nb_used = (ends[-1:] // MOE_BLOCK).astype(I32)

    xs = _dispatch_call(code, starts, starts + cnt, ends.astype(I32), u.reshape(n, d), rows)
    ff = w1.shape[2] // 2
    ys = _ffn_call(blk_e, nb_used, xs, w1, b1[:, 0::2].reshape(ne, 1, ff),
                   b1[:, 1::2].reshape(ne, 1, ff), w2, b2.reshape(ne, 1, d))
    out = _combine_call(code, starts, gk.reshape(n, LANES), h1.reshape(n, d), g2, final_g, ys, s,
                        final_norm)
    return out.reshape(b, s, d)


def _pack_w_in(w_in):
    d = w_in.shape[0]
    offs = np.cumsum((0,) + IN_SPLITS)
    head = w_in[:, :offs[5]]
    wi = w_in[:, offs[5]:offs[6]]
    pad = jnp.zeros((d, KIWI_W - IDX_DIM - IDX_HEADS), w_in.dtype)
    return jnp.concatenate([head, wi, pad, w_in[:, offs[6]:]], axis=1).astype(BF16)


def _block_diag(pool_w):
    g, c, _ = pool_w.shape
    out = jnp.zeros((g * c, g * c), pool_w.dtype)
    for j in range(g):
        out = out.at[j * c:(j + 1) * c, j * c:(j + 1) * c].set(pool_w[j])
    return out.astype(BF16)


def kernel(x, c, positions, w_ada, b_ada, norm1_g, norm2_g, w_in, w_out, pool_w, pool_scale,
           hgrn_norm_g, lb_logits, w_router, b_router, w1, b1, w2, b2, final_g):
    bsz, s, d = x.shape
    depth = w_ada.shape[0]
    mod = _ada_call(c, w_ada, b_ada)
    cos, sin = _rope_table_call(positions)
    h = x
    for l in range(depth):
        sh1, sc1, g1, sh2, sc2, g2 = [mod[l, :, j * d:(j + 1) * d].reshape(bsz, 1, d)
                                      for j in range(6)]
        qt, qit, wit, k, ki, vt, rest = _inproj_call(
            h, sh1, sc1, norm1_g[l].reshape(1, d), _pack_w_in(w_in[l]), cos, sin)
        ya = _dsa_call(qt, qit, wit, k, ki, vt)
        yb = _pool_call(rest, _block_diag(pool_w[l]), pool_scale[l].reshape(1, B_WIDTH))
        yc = _hgrn_call(rest, lb_logits, hgrn_norm_g[l], l)
        wr = jnp.pad(w_router[l], ((0, 0), (0, LANES - N_EXPERTS)))
        br = jnp.pad(b_router[l], (0, LANES - N_EXPERTS)).reshape(1, LANES)
        h1, u2, code, gk, counts = _outproj_router_call(
            ya, yb, yc, h, g1, w_out[l].astype(BF16), sh2, sc2, norm2_g[l].reshape(1, d), wr, br)
        h = _moe_call(u2, code, gk, counts, h1, g2, w1[l], b1[l], w2[l], b2[l],
                      final_g.reshape(1, d), final_norm=(l == depth - 1))
    return h
```

```python
import functools
import math

import numpy as np
import jax
import jax.numpy as jnp
from jax import lax
from jax.experimental import pallas as pl
from jax.experimental.pallas import tpu as pltpu

F32 = jnp.float32
BF16 = jnp.bfloat16
I32 = jnp.int32

CHUNK = 64
EPS = 1e-6
NEG_BIG = -1e30
A_HEADS, A_KV_HEADS, HEAD_DIM = 8, 2, 64
IDX_HEADS, IDX_DIM, IDX_TOPK_MAX = 4, 64, 256
Q_BLOCK = 128
ROPE_THETA = 10000.0
A_WIDTH = A_HEADS * HEAD_DIM
POOL_WINDOWS = (2, 4, 8, 16)
POOL_GROUP = 64
B_WIDTH = len(POOL_WINDOWS) * POOL_GROUP
C_HEADS, C_KDIM, C_VDIM = 4, 64, 64
C_WIDTH = C_HEADS * C_VDIM
N_EXPERTS, TOP_K = 32, 4
SWIGLU_LIMIT, SWIGLU_ALPHA = 7.0, 1.702
KV_WIDTH = A_KV_HEADS * HEAD_DIM
IN_SPLITS = (A_WIDTH, KV_WIDTH, KV_WIDTH, IDX_HEADS * IDX_DIM, IDX_DIM, IDX_HEADS,
             B_WIDTH, C_HEADS * C_KDIM, C_HEADS * C_KDIM, C_WIDTH, C_WIDTH)

LANES = 128
SUBLANES = 8
INT_MIN = -(2 ** 31)
VMEM_LIMIT = 56 * 1024 * 1024

KIWI_W = LANES
REST_W = B_WIDTH + 4 * C_WIDTH
PACK_W = A_WIDTH + 2 * KV_WIDTH + IDX_HEADS * IDX_DIM + KIWI_W + REST_W
HGRN_LEVELS = (32, 16, 8, 4, 2, 1)


def _nt(a, b):
    return lax.dot_general(a, b, (((1,), (1,)), ((), ())), preferred_element_type=F32)


def _tn(a, b):
    return lax.dot_general(a, b, (((0,), (0,)), ((), ())), preferred_element_type=F32)


def _mm(a, b):
    return jnp.dot(a, b, preferred_element_type=F32)


def _cparams(sem):
    return pltpu.CompilerParams(dimension_semantics=sem, vmem_limit_bytes=VMEM_LIMIT)


def _ada_kernel(c_ref, w_ref, b_ref, o_ref):
    c = c_ref[...]
    ca = c * jax.nn.sigmoid(c)
    o_ref[...] = jnp.dot(ca, w_ref[...], precision=lax.Precision.HIGHEST,
                         preferred_element_type=F32) + b_ref[...]


def _ada_call(c, w_ada, b_ada):
    depth, d, d6 = w_ada.shape
    b = c.shape[0]
    nblk = d6 // d
    return pl.pallas_call(
        _ada_kernel,
        out_shape=jax.ShapeDtypeStruct((depth, b, d6), F32),
        grid=(depth, nblk),
        in_specs=[pl.BlockSpec((b, d), lambda l, j: (0, 0)),
                  pl.BlockSpec((None, d, d), lambda l, j: (l, 0, j)),
                  pl.BlockSpec((None, 1, d), lambda l, j: (l, 0, j))],
        out_specs=pl.BlockSpec((None, b, d), lambda l, j: (l, 0, j)),
        compiler_params=_cparams(("arbitrary", "arbitrary")),
        name="ada_mod",
    )(c, w_ada, b_ada.reshape(depth, 1, d6))


def _rope_table_kernel(pos_ref, inv_ref, sign_ref, cos_ref, sin_ref):
    ang = pos_ref[...].astype(F32) * inv_ref[...]
    cos_ref[...] = jnp.cos(ang)
    sin_ref[...] = jnp.sin(ang) * sign_ref[...]


def _rope_table_call(positions, ts=512):
    b, s = positions.shape
    half = HEAD_DIM // 2
    inv = jnp.power(jnp.float32(ROPE_THETA), -jnp.arange(0, HEAD_DIM, 2, dtype=F32) / HEAD_DIM)
    inv128 = jnp.tile(inv, LANES // half).reshape(1, LANES)
    sign128 = jnp.tile(jnp.concatenate([-jnp.ones((half,), F32), jnp.ones((half,), F32)]),
                       LANES // HEAD_DIM).reshape(1, LANES)
    spec = pl.BlockSpec((None, ts, LANES), lambda bi, i: (bi, i, 0))
    return pl.pallas_call(
        _rope_table_kernel,
        out_shape=(jax.ShapeDtypeStruct((b, s, LANES), F32),) * 2,
        grid=(b, s // ts),
        in_specs=[pl.BlockSpec((None, ts, 1), lambda bi, i: (bi, i, 0)),
                  pl.BlockSpec((1, LANES), lambda bi, i: (0, 0)),
                  pl.BlockSpec((1, LANES), lambda bi, i: (0, 0))],
        out_specs=(spec, spec),
        compiler_params=_cparams(("arbitrary", "arbitrary")),
        name="rope_table",
    )(positions.reshape(b, s, 1), inv128, sign128)


def _rope_tile(x, cos, sin_signed, first_half):
    partner = jnp.where(first_half, pltpu.roll(x, LANES - HEAD_DIM // 2, 1),
                        pltpu.roll(x, HEAD_DIM // 2, 1))
    return x * cos + partner * sin_signed


def _ada_norm(x, g, sc, sh):
    y = x * lax.rsqrt(jnp.mean(x * x, axis=-1, keepdims=True) + EPS)
    return (y * g) * (1.0 + sc) + sh


VT_W = 256
WI_ROWS = SUBLANES


def _inproj_kernel(h_ref, sh_ref, sc_ref, g_ref, w_ref, cos_ref, sin_ref,
                   qt_ref, qit_ref, wit_ref, k_ref, ki_ref, vt_ref, rest_ref):
    u = _ada_norm(h_ref[...], g_ref[...], sc_ref[...], sh_ref[...]).astype(BF16)
    cos = cos_ref[...]
    sin = sin_ref[...]
    lane = lax.broadcasted_iota(I32, cos.shape, 1)
    first_half = (lane % HEAD_DIM) < (HEAD_DIM // 2)
    rope = lambda x: _rope_tile(x, cos, sin, first_half)

    off = 0
    q_scale = HEAD_DIM ** -0.5
    for j in range(A_WIDTH // LANES):
        z = _mm(u, w_ref[:, off + j * LANES: off + (j + 1) * LANES])
        qt_ref[j * LANES:(j + 1) * LANES, :] = (rope(z) * q_scale).T.astype(BF16)
    off += A_WIDTH
    k_ref[...] = rope(_mm(u, w_ref[:, off:off + KV_WIDTH])).astype(BF16)
    off += KV_WIDTH
    vt_ref[...] = _mm(u, w_ref[:, off:off + KV_WIDTH]).T.astype(BF16)
    off += KV_WIDTH
    for j in range(IDX_HEADS * IDX_DIM // LANES):
        z = _mm(u, w_ref[:, off + j * LANES: off + (j + 1) * LANES])
        qit_ref[j * LANES:(j + 1) * LANES, :] = rope(z).T.astype(BF16)
    off += IDX_HEADS * IDX_DIM
    z = _mm(u, w_ref[:, off:off + KIWI_W])
    ki_ref[...] = rope(z).astype(BF16)
    wi_scale = (IDX_HEADS * IDX_DIM) ** -0.5
    wit_ref[...] = (z * wi_scale).T[IDX_DIM:IDX_DIM + WI_ROWS, :]
    off += KIWI_W
    rest_ref[...] = _mm(u, w_ref[:, off:off + REST_W])


def _inproj_call(h, sh, sc, g, w_pack, cos, sin):
    b, s, d = h.shape
    tm = VT_W
    tok = lambda w: pl.BlockSpec((None, tm, w), lambda bi, i: (bi, i, 0))
    tr = lambda r: pl.BlockSpec((None, r, tm), lambda bi, i: (bi, 0, i))
    per_b = pl.BlockSpec((None, 1, d), lambda bi, i: (bi, 0, 0))
    sds = jax.ShapeDtypeStruct
    return pl.pallas_call(
        _inproj_kernel,
        out_shape=(sds((b, A_WIDTH, s), BF16), sds((b, IDX_HEADS * IDX_DIM, s), BF16),
                   sds((b, WI_ROWS, s), F32), sds((b, s, KV_WIDTH), BF16),
                   sds((b, s, KIWI_W), BF16), sds((b, s // tm, KV_WIDTH, tm), BF16),
                   sds((b, s, REST_W), F32)),
        grid=(b, s // tm),
        in_specs=[tok(d), per_b, per_b,
                  pl.BlockSpec((1, d), lambda bi, i: (0, 0)),
                  pl.BlockSpec((d, PACK_W), lambda bi, i: (0, 0)),
                  tok(LANES), tok(LANES)],
        out_specs=(tr(A_WIDTH), tr(IDX_HEADS * IDX_DIM), tr(WI_ROWS), tok(KV_WIDTH), tok(KIWI_W),
                   pl.BlockSpec((None, None, KV_WIDTH, tm), lambda bi, i: (bi, i, 0, 0)),
                   tok(REST_W)),
        compiler_params=_cparams(("arbitrary", "arbitrary")),
        name="inproj",
    )(h, sh, sc, g, w_pack, cos, sin)


CNT_ROWS = 32
ONES_ROWS = 16


def _dsa_kernel(qt_ref, qit_ref, wit_ref, k_ref, ki_ref, vt_ref, lower_ref, o_ref,
                keys_ref, bias_ref, s_ref, acc_ref, *, kc, topk):
    qb = Q_BLOCK
    i = pl.program_id(1)
    t0 = i * qb
    nkc = (t0 + qb + kc - 1) // kc
    lane = lax.broadcasted_iota(I32, (1, qb), 1)
    limit = t0 + (lane // CHUNK + 1) * CHUNK
    key_off = lax.broadcasted_iota(I32, (kc, qb), 0)
    group = A_HEADS // A_KV_HEADS

    qit = qit_ref[...]
    qi_stack = jnp.concatenate(
        [qit[h * IDX_DIM:(h + 1) * IDX_DIM, :] for h in range(IDX_HEADS)], axis=1)
    wit = wit_ref[...]

    def score_body(c, carry):
        off = pl.multiple_of(c * kc, kc)
        ki = ki_ref[pl.ds(off, kc), 0:IDX_DIM]
        s = jnp.maximum(_mm(ki, qi_stack), 0.0)
        score = s[:, 0:qb] * wit[0:1, :]
        for h in range(1, IDX_HEADS):
            score = score + s[:, h * qb:(h + 1) * qb] * wit[h:h + 1, :]
        score = jnp.where(score == 0.0, 0.0, score)
        bits = lax.bitcast_convert_type(score, I32)
        key = jnp.where(bits < 0, bits ^ jnp.int32(0x7FFFFFFF), bits)
        keys_ref[pl.ds(off, kc), :] = jnp.where(off + key_off < limit, key, jnp.int32(INT_MIN))
        return carry

    lax.fori_loop(0, nkc, score_body, 0)

    def count(pred):
        def body(c, acc):
            off = pl.multiple_of(c * kc, kc)
            hit = jnp.where(pred(keys_ref[pl.ds(off, kc), :]), 1.0, 0.0)
            return acc + jnp.sum(hit.reshape(kc // CNT_ROWS, CNT_ROWS, qb), axis=0)
        acc = lax.fori_loop(0, nkc, body, jnp.zeros((CNT_ROWS, qb), F32))
        return jnp.sum(acc, axis=0, keepdims=True)

    topk_f = float(topk)
    cnt0 = count(lambda kk: kk >= 0)
    cnt1 = count(lambda kk: kk >= 1)
    nonneg = cnt0 >= topk_f
    thr0 = jnp.where(nonneg, 0, INT_MIN).astype(I32)
    cthr0 = jnp.where(nonneg, cnt0, 2.0 * kc * (nkc + 1).astype(F32))
    done0 = jnp.where((nonneg & (cnt1 < topk_f)) | (cthr0 == topk_f) | (limit <= topk), 1.0, 0.0)
    steps = 4

    def thr_cond(state):
        g, _, _, done = state
        return (g * steps < 31) & (jnp.min(done) < 0.5)

    def thr_body(state):
        g, thr, cthr, done = state
        for jj in range(steps):
            b = 30 - (g * steps + jj)
            bit = jnp.where(b >= 0, jnp.left_shift(jnp.int32(1), jnp.maximum(b, 0)), 0)
            cand = thr + bit
            cnt = count(lambda kk: kk >= cand)
            take = (cnt >= topk_f) & (done < 0.5)
            thr = jnp.where(take, cand, thr)
            cthr = jnp.where(take, cnt, cthr)
            done = jnp.where(cthr == topk_f, 1.0, done)
        return g + 1, thr, cthr, done

    _, thr, _, _ = lax.while_loop(thr_cond, thr_body, (jnp.int32(0), thr0, cthr0, done0))

    need = topk_f - count(lambda kk: kk > thr)
    lower = lower_ref[...]

    def bias_body(c, seen):
        off = pl.multiple_of(c * kc, kc)
        kk = keys_ref[pl.ds(off, kc), :]
        tie = kk == thr
        tie_f = jnp.where(tie, 1.0, 0.0)
        rank = _mm(lower, tie_f.astype(BF16)) + seen
        sel = ((kk > thr) | (tie & (rank <= need))) & (off + key_off < limit)
        bias_ref[pl.ds(off, kc), :] = jnp.where(sel, 0.0, NEG_BIG)
        return seen + jnp.sum(tie_f, axis=0, keepdims=True)

    lax.fori_loop(0, nkc, bias_body, jnp.zeros((1, qb), F32))

    qt = qt_ref[...]
    q_n = [jnp.concatenate([qt[(n * group + g) * HEAD_DIM:(n * group + g + 1) * HEAD_DIM, :]
                            for g in range(group)], axis=1) for n in range(A_KV_HEADS)]

    def max_body(c, parts):
        out = list(parts)
        for j in range(kc // VT_W):
            off = pl.multiple_of(c * kc + j * VT_W, VT_W)
            bias = bias_ref[pl.ds(off, VT_W), :]
            for n in range(A_KV_HEADS):
                s = _mm(k_ref[pl.ds(off, VT_W), n * HEAD_DIM:(n + 1) * HEAD_DIM], q_n[n])
                s = jnp.concatenate([s[:, g * qb:(g + 1) * qb] + bias for g in range(group)], axis=1)
                s_ref[c * (kc // VT_W) + j, n] = s
                out[n] = jnp.maximum(out[n], jnp.max(
                    s.reshape(VT_W // SUBLANES, SUBLANES, group * qb), axis=0))
        return tuple(out)

    parts = lax.fori_loop(
        0, nkc, max_body,
        tuple(jnp.full((SUBLANES, group * qb), NEG_BIG, F32) for _ in range(A_KV_HEADS)))
    m_n = [jnp.max(p, axis=0, keepdims=True) for p in parts]

    acc_ref[...] = jnp.zeros(acc_ref.shape, F32)
    ones = jnp.ones((ONES_ROWS, VT_W), BF16)

    def pv_body(c, carry):
        for j in range(kc // VT_W):
            cv = c * (kc // VT_W) + j
            for n in range(A_KV_HEADS):
                pt = jnp.exp(s_ref[cv, n] - m_n[n]).astype(BF16)
                vt = jnp.concatenate([vt_ref[cv, n * HEAD_DIM:(n + 1) * HEAD_DIM, :], ones], axis=0)
                acc_ref[n] += _mm(vt, pt)
        return carry

    lax.fori_loop(0, nkc, pv_body, 0)
    for n in range(A_KV_HEADS):
        for g in range(group):
            hh = n * group + g
            cols = slice(g * qb, (g + 1) * qb)
            o_ref[hh * HEAD_DIM:(hh + 1) * HEAD_DIM, :] = (
                acc_ref[n, 0:HEAD_DIM, cols] / acc_ref[n, HEAD_DIM:HEAD_DIM + 1, cols])


def _dsa_call(qt, qit, wit, k, ki, vt, kc=512):
    b, _, s = qt.shape
    kc = min(kc, s)
    assert kc % VT_W == 0 and s % kc == 0
    topk = min(IDX_TOPK_MAX, s // 4)
    group = A_HEADS // A_KV_HEADS
    lower = jnp.asarray(np.tril(np.ones((kc, kc), np.float32)), BF16)
    qblk = lambda r: pl.BlockSpec((None, r, Q_BLOCK), lambda bi, i: (bi, 0, i))
    seq = lambda w: pl.BlockSpec((None, s, w), lambda bi, i: (bi, 0, 0))
    return pl.pallas_call(
        functools.partial(_dsa_kernel, kc=kc, topk=topk),
        out_shape=jax.ShapeDtypeStruct((b, A_WIDTH, s), F32),
        grid=(b, s // Q_BLOCK),
        in_specs=[qblk(A_WIDTH), qblk(IDX_HEADS * IDX_DIM), qblk(WI_ROWS),
                  seq(KV_WIDTH), seq(KIWI_W),
                  pl.BlockSpec((None, s // VT_W, KV_WIDTH, VT_W), lambda bi, i: (bi, 0, 0, 0)),
                  pl.BlockSpec((kc, kc), lambda bi, i: (0, 0))],
        out_specs=qblk(A_WIDTH),
        scratch_shapes=[pltpu.VMEM((s, Q_BLOCK), I32),
                        pltpu.VMEM((s, Q_BLOCK), F32),
                        pltpu.VMEM((s // VT_W, A_KV_HEADS, VT_W, group * Q_BLOCK), F32),
                        pltpu.VMEM((A_KV_HEADS, HEAD_DIM + ONES_ROWS, group * Q_BLOCK), F32)],
        compiler_params=_cparams(("arbitrary", "arbitrary")),
        name="dsa",
    )(qt, qit, wit, k, ki, vt, lower)


POOL_HALO = 32


def _pool_kernel(u_ref, w_ref, scale_ref, o_ref, x_buf, a_buf, b_buf, *, tm):
    hl = POOL_HALO
    rows = tm + hl
    first = pl.program_id(1) == 0

    @pl.when(first)
    def _():
        x_buf[0:hl, :] = jnp.zeros((hl, B_WIDTH), F32)

    x = u_ref[...]
    x_buf[hl:rows, :] = x
    a_buf[8:rows, :] = x_buf[8:rows, :] + x_buf[7:rows - 1, :]
    b_buf[16:rows, :] = a_buf[16:rows, :] + a_buf[14:rows - 2, :]
    w2 = a_buf[hl:rows, :]
    w4 = b_buf[hl:rows, :]
    a_buf[24:rows, :] = b_buf[24:rows, :] + b_buf[20:rows - 4, :]
    w8 = a_buf[hl:rows, :]
    b_buf[hl:rows, :] = a_buf[hl:rows, :] + a_buf[hl - 8:rows - 8, :]
    w16 = b_buf[hl:rows, :]
    x_buf[0:hl, :] = x_buf[tm:rows, :]

    lane = lax.broadcasted_iota(I32, (tm, B_WIDTH), 1)
    grp = lane // POOL_GROUP
    wsum = jnp.where(grp == 0, w2, jnp.where(grp == 1, w4, jnp.where(grp == 2, w8, w16)))
    win = jnp.where(grp == 0, 2, jnp.where(grp == 1, 4, jnp.where(grp == 2, 8, 16)))
    t = pl.program_id(1) * tm + lax.broadcasted_iota(I32, (tm, B_WIDTH), 0)
    cnt = jnp.minimum(t + 1, win).astype(F32)
    pooled = wsum / cnt - x
    y = _mm(pooled.astype(BF16), w_ref[...])
    o_ref[...] = y * scale_ref[...]


def _pool_call(rest, w_bd, scale, tm=512):
    b, s, _ = rest.shape
    tm = min(tm, s)
    rows = tm + POOL_HALO
    return pl.pallas_call(
        functools.partial(_pool_kernel, tm=tm),
        out_shape=jax.ShapeDtypeStruct((b, s, B_WIDTH), F32),
        grid=(b, s // tm),
        in_specs=[pl.BlockSpec((None, tm, B_WIDTH), lambda bi, i: (bi, i, 0)),
                  pl.BlockSpec((B_WIDTH, B_WIDTH), lambda bi, i: (0, 0)),
                  pl.BlockSpec((1, B_WIDTH), lambda bi, i: (0, 0))],
        out_specs=pl.BlockSpec((None, tm, B_WIDTH), lambda bi, i: (bi, i, 0)),
        scratch_shapes=[pltpu.VMEM((rows, B_WIDTH), F32)] * 3,
        compiler_params=_cparams(("arbitrary", "arbitrary")),
        name="pool",
    )(rest, w_bd, scale)


def _hgrn_consts():
    tril = np.tril(np.ones((CHUNK, CHUNK), np.float32))
    mats = [tril]
    r = np.arange(CHUNK)
    for h in HGRN_LEVELS:
        mats.append(tril[(r // (2 * h)) * (2 * h) + h - 1])
    return np.concatenate(mats, axis=0)


def _split3(x):
    hi = x.astype(BF16)
    r1 = x - hi.astype(F32)
    mid = r1.astype(BF16)
    lo = (r1 - mid.astype(F32)).astype(BF16)
    return hi, mid, lo


def _hgrn_kernel(q_ref, f_ref, i_ref, g_ref, lb_ref, ng_ref, cm_ref, bd_ref, o_ref, state_ref,
                 *, layer, tm):
    @pl.when(pl.program_id(1) == 0)
    def _():
        state_ref[...] = jnp.zeros(state_ref.shape, F32)

    lbl = lb_ref[...]
    e = jnp.exp(lbl - jnp.max(lbl, axis=0, keepdims=True))
    p = e / jnp.sum(e, axis=0, keepdims=True)
    cum = p[0:1]
    for l in range(1, layer + 1):
        cum = cum + p[l:l + 1]
    lb = jnp.clip(cum - p[0:1], 0.0, 1.0)

    cm = cm_ref[...]
    ng = ng_ref[...]
    bd = bd_ref[...]
    bd_f = bd.astype(F32)
    row = lax.broadcasted_iota(I32, (CHUNK, 1), 0)
    tt = lax.broadcasted_iota(I32, (CHUNK, C_WIDTH), 0)
    ss = lax.broadcasted_iota(I32, (CHUNK, C_WIDTH), 1) % CHUNK
    lvl_mask = [tt == ss] + [(tt // (2 * h)) == (ss // (2 * h)) for h in HGRN_LEVELS]
    w = C_WIDTH
    heads = C_WIDTH // C_KDIM

    def expand(x16):
        return jnp.concatenate([x16] * heads, axis=0) * bd

    chunks = range(tm // CHUNK)
    rows = [slice(ci * CHUNK, (ci + 1) * CHUNK) for ci in chunks]
    z = f_ref[...]
    log_f = jnp.log(lb + (1.0 - lb) * jax.nn.sigmoid(z))
    kin = (1.0 - lb) * jax.nn.sigmoid(-z)
    qx = q_ref[...]
    qv = qx * jax.nn.sigmoid(qx)
    vb16 = i_ref[...].astype(BF16)
    hi, mid, lo = _split3(log_f)
    lf3 = jnp.concatenate([hi, mid, lo], axis=1)
    cs = [_mm(cm, lf3[r]) for r in rows]
    cs = [c[:, 0:w] + c[:, w:2 * w] + c[:, 2 * w:3 * w] for c in cs]
    odd = [((row // h) % 2) == 1 for h in HGRN_LEVELS]
    attn, q_dec, upd, s_dec = [], [], [], []
    for ci in chunks:
        r = rows[ci]
        bcum = cs[ci][0:CHUNK]
        b_last = bcum[CHUNK - 1:CHUNK]
        q_c, k_c = qv[r], kin[r]
        q_dec.append((q_c * jnp.exp(bcum)).astype(BF16))
        s_dec.append(jnp.exp(b_last))
        upd.append(_tn(vb16[r], (k_c * jnp.exp(b_last - bcum)).astype(BF16)) * bd_f)
        qs, ks = [q_c.astype(BF16)], [k_c.astype(BF16)]
        for li in range(len(HGRN_LEVELS)):
            ref = cs[ci][(li + 1) * CHUNK:(li + 2) * CHUNK]
            qs.append((q_c * jnp.exp(jnp.where(odd[li], bcum - ref, NEG_BIG))).astype(BF16))
            ks.append((k_c * jnp.exp(jnp.where(odd[li], NEG_BIG, ref - bcum))).astype(BF16))
        a = jnp.zeros((CHUNK, w), F32)
        for mask, ql, kl in zip(lvl_mask, qs, ks):
            a = a + jnp.where(mask, _nt(ql, expand(kl)), 0.0)
        attn.append(a)
    o_intra = [_mm(attn[ci].astype(BF16), expand(vb16[rows[ci]])) for ci in chunks]
    st = state_ref[...]
    outs = []
    for ci in chunks:
        outs.append(_nt(q_dec[ci], st.astype(BF16)) + o_intra[ci])
        st = st * s_dec[ci] + upd[ci]
    state_ref[...] = st
    o = jnp.concatenate(outs, axis=0)
    o2h, o2m, o2l = _split3(o * o)
    ms = (_mm(o2h, bd) + _mm(o2m, bd) + _mm(o2l, bd)) * (1.0 / C_VDIM)
    gx = g_ref[...]
    o_ref[...] = (o * lax.rsqrt(ms + EPS) * ng) * (gx * jax.nn.sigmoid(gx))


def _hgrn_call(rest, lb_logits, norm_g, layer, tm=256):
    b, s, _ = rest.shape
    depth = lb_logits.shape[0]
    cm = jnp.asarray(_hgrn_consts(), BF16)
    head_of = np.arange(C_WIDTH) // C_KDIM
    bd = jnp.asarray(head_of[:, None] == head_of[None, :], BF16)
    col = lambda j: pl.BlockSpec((None, tm, C_WIDTH), lambda bi, i, j=j: (bi, i, j))
    return pl.pallas_call(
        functools.partial(_hgrn_kernel, layer=layer, tm=tm),
        out_shape=jax.ShapeDtypeStruct((b, s, C_WIDTH), F32),
        grid=(b, s // tm),
        in_specs=[col(1), col(2), col(3), col(4),
                  pl.BlockSpec((depth, C_WIDTH), lambda bi, i: (0, 0)),
                  pl.BlockSpec((1, C_WIDTH), lambda bi, i: (0, 0)),
                  pl.BlockSpec(cm.shape, lambda bi, i: (0, 0)),
                  pl.BlockSpec((C_WIDTH, C_WIDTH), lambda bi, i: (0, 0))],
        out_specs=pl.BlockSpec((None, tm, C_WIDTH), lambda bi, i: (bi, i, 0)),
        scratch_shapes=[pltpu.VMEM((C_WIDTH, C_WIDTH), F32)],
        compiler_params=_cparams(("arbitrary", "arbitrary")),
        name="hgrn2",
    )(rest, rest, rest, rest, lb_logits, jnp.tile(norm_g, C_HEADS).reshape(1, C_WIDTH), cm, bd)


RANK_BITS = 20
CODE_ROWS = 8


def _outproj_router_kernel(ya_ref, yb_ref, yc_ref, h_ref, g1_ref, wo_ref, sh_ref, sc_ref, g_ref,
                           wr_ref, br_ref, tri_ref, h1_ref, u_ref, code_ref, gk_ref, cnt_ref,
                           run_ref):
    @pl.when((pl.program_id(0) == 0) & (pl.program_id(1) == 0))
    def _():
        run_ref[...] = jnp.zeros(run_ref.shape, F32)

    y = _tn(ya_ref[...].astype(BF16), wo_ref[0:A_WIDTH, :])
    y = y + _mm(yb_ref[...].astype(BF16), wo_ref[A_WIDTH:A_WIDTH + B_WIDTH, :])
    y = y + _mm(yc_ref[...].astype(BF16), wo_ref[A_WIDTH + B_WIDTH:, :])
    h1 = h_ref[...] + g1_ref[...] * y
    h1_ref[...] = h1
    u = _ada_norm(h1, g_ref[...], sc_ref[...], sh_ref[...])
    u_ref[...] = u

    logits = jnp.dot(u, wr_ref[...], precision=lax.Precision.HIGHEST,
                     preferred_element_type=F32) + br_ref[...]
    lane = lax.broadcasted_iota(I32, logits.shape, 1).astype(F32)
    work = jnp.where(lane < N_EXPERTS, logits, -jnp.inf)
    picks, firsts, tops = [], [], []
    for k in range(TOP_K):
        m = jnp.max(work, axis=1, keepdims=True)
        first = jnp.min(jnp.where(work == m, lane, float(LANES)), axis=1, keepdims=True)
        pick = lane == first
        work = jnp.where(pick, -jnp.inf, work)
        picks.append(pick)
        firsts.append(first)
        tops.append(m)
    ex = [jnp.exp(m - tops[0]) for m in tops]
    den = ex[0] + ex[1] + ex[2] + ex[3]

    sel = picks[0] | picks[1] | picks[2] | picks[3]
    sel_f = jnp.where(sel, 1.0, 0.0)
    prefix = _mm(tri_ref[...], sel_f.astype(BF16)) + run_ref[...]
    run_ref[...] += jnp.sum(sel_f, axis=0, keepdims=True)
    cnt_ref[...] = run_ref[...]

    code = jnp.zeros(logits.shape, I32)
    gk = jnp.zeros(logits.shape, F32)
    for k in range(TOP_K):
        rank = jnp.sum(jnp.where(picks[k], prefix, 0.0), axis=1, keepdims=True)
        ck = (firsts[k].astype(I32) << RANK_BITS) | rank.astype(I32)
        code = jnp.where(lane == float(k), ck, code)
        gk = jnp.where(lane == float(k), ex[k] / den, gk)
    code_ref[...] = code.T[0:CODE_ROWS, :]
    gk_ref[...] = gk


def _outproj_router_call(ya, yb, yc, h, g1, w_out, sh, sc, g, w_router, b_router, tm=512):
    b, s, d = h.shape
    tm = min(tm, s)
    tri = jnp.asarray(np.tril(np.ones((tm, tm), np.float32), -1), BF16)
    tok = lambda w: pl.BlockSpec((None, tm, w), lambda bi, i: (bi, i, 0))
    per_b = pl.BlockSpec((None, 1, d), lambda bi, i: (bi, 0, 0))
    full = lambda a: pl.BlockSpec(a.shape, lambda bi, i: (0,) * a.ndim)
    return pl.pallas_call(
        _outproj_router_kernel,
        out_shape=(jax.ShapeDtypeStruct((b, s, d), F32), jax.ShapeDtypeStruct((b, s, d), F32),
                   jax.ShapeDtypeStruct((CODE_ROWS, b * s), I32),
                   jax.ShapeDtypeStruct((b, s, LANES), F32), jax.ShapeDtypeStruct((1, LANES), F32)),
        grid=(b, s // tm),
        in_specs=[pl.BlockSpec((None, A_WIDTH, tm), lambda bi, i: (bi, 0, i)),
                  tok(B_WIDTH), tok(C_WIDTH), tok(d), per_b, full(w_out),
                  per_b, per_b, full(g), full(w_router), full(b_router), full(tri)],
        out_specs=(tok(d), tok(d),
                   pl.BlockSpec((CODE_ROWS, tm), lambda bi, i: (0, bi * (s // tm) + i)),
                   tok(LANES), pl.BlockSpec((1, LANES), lambda bi, i: (0, 0))),
        scratch_shapes=[pltpu.VMEM((1, LANES), F32)],
        compiler_params=_cparams(("arbitrary", "arbitrary")),
        name="outproj_router",
    )(ya, yb, yc, h, g1, w_out, sh, sc, g, w_router, b_router, tri)


MOE_BLOCK = 512
PERM_W = 2 * LANES


ROWS_TILE = 2048


def _rows_kernel(start_ref, code_ref, row_ref):
    code = code_ref[...]
    expert = code >> RANK_BITS
    base = jnp.zeros(code.shape, I32)
    for e in range(N_EXPERTS):
        base = jnp.where(expert == e, start_ref[e], base)
    row_ref[...] = base + (code & ((1 << RANK_BITS) - 1))


def _rows_call(starts, code):
    r, n = code.shape
    tile = min(ROWS_TILE, n)
    return pl.pallas_call(
        _rows_kernel,
        out_shape=jax.ShapeDtypeStruct((r, n), I32),
        grid_spec=pltpu.PrefetchScalarGridSpec(
            num_scalar_prefetch=1, grid=(n // tile,),
            in_specs=[pl.BlockSpec((r, tile), lambda i, *_: (0, i))],
            out_specs=pl.BlockSpec((r, tile), lambda i, *_: (0, i))),
        compiler_params=_cparams(("arbitrary",)),
        name="moe_rows",
    )(starts, code)


def _drain_rows(src_row, dst_row, sem, n):
    def body(t, c):
        pltpu.make_async_copy(src_row, dst_row, sem).wait()
        return c
    lax.fori_loop(0, n, body, 0, unroll=4)


ZERO_ROWS = MOE_BLOCK // 2


def _dispatch_kernel(row_ref, padlo_ref, padhi_ref, x_ref, xs_ref, zbuf, sem, zsem, *, tm):
    tok0 = pl.program_id(0) * tm
    n_tok = row_ref.shape[0] // TOP_K

    @pl.when(pl.program_id(0) == 0)
    def _():
        zbuf[...] = jnp.zeros(zbuf.shape, F32)

        def pad_copies(e, fn):
            lo = padlo_ref[e]
            hi = padhi_ref[e]
            lo8 = jnp.minimum((lo + SUBLANES - 1) // SUBLANES * SUBLANES, hi)
            for j in range(SUBLANES - 1):

                @pl.when(lo + j < lo8)
                def _(j=j):
                    fn(pltpu.make_async_copy(zbuf.at[pl.ds(0, 1)], xs_ref.at[pl.ds(lo + j, 1)], zsem))
            n8 = hi - lo8
            size = ZERO_ROWS
            while size >= SUBLANES:
                off = pl.multiple_of(lo8 + (n8 & ~(2 * size - 1)), SUBLANES)

                @pl.when((n8 & size) != 0)
                def _(size=size, off=off):
                    fn(pltpu.make_async_copy(zbuf.at[pl.ds(0, size)], xs_ref.at[pl.ds(off, size)],
                                             zsem))
                size //= 2

        def tail_copies(fn):
            tail_lo = padhi_ref[N_EXPERTS - 1]

            def piece(j, c):
                off = pl.multiple_of(tail_lo + j * ZERO_ROWS, ZERO_ROWS)

                @pl.when(off < xs_ref.shape[0])
                def _():
                    fn(pltpu.make_async_copy(zbuf, xs_ref.at[pl.ds(off, ZERO_ROWS)], zsem))
                return c

            lax.fori_loop(0, N_EXPERTS * MOE_BLOCK // ZERO_ROWS, piece, 0)

        def start_e(e, c):
            pad_copies(e, lambda cp: cp.start())
            return c

        def wait_e(e, c):
            pad_copies(e, lambda cp: cp.wait())
            return c

        lax.fori_loop(0, N_EXPERTS, start_e, 0)
        tail_copies(lambda cp: cp.start())
        lax.fori_loop(0, N_EXPERTS, wait_e, 0)
        tail_copies(lambda cp: cp.wait())

    def issue(t, c):
        for k in range(TOP_K):
            dst = row_ref[k * n_tok + tok0 + t]
            pltpu.make_async_copy(x_ref.at[pl.ds(t, 1)], xs_ref.at[pl.ds(dst, 1)], sem).start(
                priority=k % 2)
        return c

    lax.fori_loop(0, tm, issue, 0, unroll=2)
    _drain_rows(x_ref.at[pl.ds(0, 1)], xs_ref.at[pl.ds(0, 1)], sem, tm * TOP_K)


def _dispatch_call(row_ids, pad_lo, pad_hi, x, rows, tm=512):
    n, d = x.shape
    return pl.pallas_call(
        functools.partial(_dispatch_kernel, tm=tm),
        out_shape=jax.ShapeDtypeStruct((rows, d), F32),
        grid_spec=pltpu.PrefetchScalarGridSpec(
            num_scalar_prefetch=3, grid=(n // tm,),
            in_specs=[pl.BlockSpec((tm, d), lambda i, *_: (i, 0))],
            out_specs=pl.BlockSpec(memory_space=pl.ANY),
            scratch_shapes=[pltpu.VMEM((ZERO_ROWS, d), F32), pltpu.SemaphoreType.DMA,
                            pltpu.SemaphoreType.DMA]),
        compiler_params=_cparams(("arbitrary",)),
        name="moe_dispatch",
    )(row_ids, pad_lo, pad_hi, x)


def _ffn_kernel(blk_e_ref, nb_ref, xs_ref, w1_ref, b1g_ref, b1l_ref, w2_ref, b2_ref, perm_ref,
                ys_ref, w1g_s, w1l_s, w2_s):
    i = pl.program_id(0)
    e = blk_e_ref[i]
    live = i < nb_ref[0]
    fresh = (i == 0) | (e != blk_e_ref[jnp.maximum(i - 1, 0)])

    @pl.when(live & fresh)
    def _():
        perm = perm_ref[...]
        for j in range(w1_ref.shape[1] // PERM_W):
            t = _mm(w1_ref[:, j * PERM_W:(j + 1) * PERM_W].astype(BF16), perm).astype(BF16)
            w1g_s[:, j * LANES:(j + 1) * LANES] = t[:, 0:LANES]
            w1l_s[:, j * LANES:(j + 1) * LANES] = t[:, LANES:PERM_W]
        w2_s[...] = w2_ref[...].astype(BF16)

    @pl.when(live)
    def _():
        x = xs_ref[...].astype(BF16)
        glu = jnp.minimum(_mm(x, w1g_s[...]) + b1g_ref[...], SWIGLU_LIMIT)
        lin = jnp.clip(_mm(x, w1l_s[...]) + b1l_ref[...], -SWIGLU_LIMIT, SWIGLU_LIMIT)
        act = glu * jax.nn.sigmoid(SWIGLU_ALPHA * glu) * (lin + 1.0)
        ys_ref[...] = _mm(act.astype(BF16), w2_s[...]) + b2_ref[...]

    @pl.when(jnp.logical_not(live))
    def _():
        ys_ref[...] = jnp.zeros(ys_ref.shape, F32)


def _ffn_call(blk_e, nb_used, xs, w1, b1g, b1l, w2, b2, layer):
    rows, d = xs.shape
    _, ne, _, ff2 = w1.shape
    ff = ff2 // 2
    perm = np.zeros((PERM_W, PERM_W), np.float32)
    perm[2 * np.arange(LANES), np.arange(LANES)] = 1.0
    perm[2 * np.arange(LANES) + 1, LANES + np.arange(LANES)] = 1.0
    ex = lambda r, c: pl.BlockSpec((None, r, c), lambda i, be, nb: (be[i], 0, 0))
    exl = lambda r, c: pl.BlockSpec((None, None, r, c), lambda i, be, nb: (layer, be[i], 0, 0))
    return pl.pallas_call(
        _ffn_kernel,
        out_shape=jax.ShapeDtypeStruct((rows, d), F32),
        grid_spec=pltpu.PrefetchScalarGridSpec(
            num_scalar_prefetch=2, grid=(rows // MOE_BLOCK,),
            in_specs=[pl.BlockSpec((MOE_BLOCK, d),
                                   lambda i, be, nb: (jnp.minimum(i, jnp.maximum(nb[0] - 1, 0)), 0)),
                      exl(d, ff2), ex(1, ff), ex(1, ff), exl(ff, d), ex(1, d),
                      pl.BlockSpec((PERM_W, PERM_W), lambda i, *_: (0, 0))],
            out_specs=pl.BlockSpec((MOE_BLOCK, d), lambda i, *_: (i, 0)),
            scratch_shapes=[pltpu.VMEM((d, ff), BF16), pltpu.VMEM((d, ff), BF16),
                            pltpu.VMEM((ff, d), BF16)]),
        compiler_params=_cparams(("arbitrary",)),
        name="moe_ffn",
    )(blk_e, nb_used, xs, w1, b1g, b1l, w2, b2, jnp.asarray(perm, BF16))


def _combine_kernel(row_ref, gk_ref, h_ref, g2_ref, fg_ref, ys_ref, o_ref, buf, sem,
                    *, tm, final_norm):
    tok0 = pl.program_id(0) * tm
    n_tok = row_ref.shape[0] // TOP_K

    def issue(t, c):
        for k in range(TOP_K):
            src = row_ref[k * n_tok + tok0 + t]
            pltpu.make_async_copy(ys_ref.at[pl.ds(src, 1)], buf.at[k, pl.ds(t, 1)], sem).start(
                priority=k % 2)
        return c

    lax.fori_loop(0, tm, issue, 0, unroll=2)
    _drain_rows(ys_ref.at[pl.ds(0, 1)], buf.at[0, pl.ds(0, 1)], sem, tm * TOP_K)

    gk = gk_ref[...]
    acc = buf[0] * gk[:, 0:1]
    for k in range(1, TOP_K):
        acc = acc + buf[k] * gk[:, k:k + 1]
    out = h_ref[...] + g2_ref[...] * acc
    if final_norm:
        out = out * lax.rsqrt(jnp.mean(out * out, axis=-1, keepdims=True) + EPS) * fg_ref[...]
    o_ref[...] = out


def _combine_call(row_ids, gk, h1, g2, final_g, ys, seq, final_norm, tm=256):
    n, d = h1.shape
    return pl.pallas_call(
        functools.partial(_combine_kernel, tm=tm, final_norm=final_norm),
        out_shape=jax.ShapeDtypeStruct((n, d), F32),
        grid_spec=pltpu.PrefetchScalarGridSpec(
            num_scalar_prefetch=1, grid=(n // tm,),
            in_specs=[pl.BlockSpec((tm, LANES), lambda i, *_: (i, 0)),
                      pl.BlockSpec((tm, d), lambda i, *_: (i, 0)),
                      pl.BlockSpec((None, 1, d), lambda i, *_: ((i * tm) // seq, 0, 0)),
                      pl.BlockSpec((1, d), lambda i, *_: (0, 0)),
                      pl.BlockSpec(memory_space=pl.ANY)],
            out_specs=pl.BlockSpec((tm, d), lambda i, *_: (i, 0)),
            scratch_shapes=[pltpu.VMEM((TOP_K, tm, d), F32), pltpu.SemaphoreType.DMA]),
        compiler_params=_cparams(("arbitrary",)),
        name="moe_combine",
    )(row_ids, gk, h1, g2, final_g, ys)


def _moe_call(u, code, gk, counts, h1, g2, w1, b1, w2, b2, final_g, layer, final_norm):
    b, s, d = h1.shape
    n = b * s
    ne = w1.shape[1]
    rows = n * TOP_K + ne * MOE_BLOCK
    cnt = counts[0, :ne].astype(I32)
    padded = (cnt + MOE_BLOCK - 1) // MOE_BLOCK * MOE_BLOCK
    ends = jnp.cumsum(padded)
    starts = (ends - padded).astype(I32)
    blk_start = jnp.arange(rows // MOE_BLOCK, dtype=I32) * MOE_BLOCK
    blk_e = jnp.minimum(jnp.sum((ends[None, :] <= blk_start[:, None]).astype(I32), axis=1), ne - 1)
    nb_used = (ends[-1:] // MOE_BLOCK).astype(I32)

    row_ids = _rows_call(starts, code)[0:TOP_K].reshape(TOP_K * n)

    xs = _dispatch_call(row_ids, starts + cnt, ends.astype(I32), u.reshape(n, d), rows)
    ff = w1.shape[3] // 2
    ys = _ffn_call(blk_e, nb_used, xs, w1, b1[:, 0::2].reshape(ne, 1, ff),
                   b1[:, 1::2].reshape(ne, 1, ff), w2, b2.reshape(ne, 1, d), layer)
    out = _combine_call(row_ids, gk.reshape(n, LANES), h1.reshape(n, d), g2, final_g, ys, s,
                        final_norm)
    return out.reshape(b, s, d)


def _pack_w_in(w_in):
    d = w_in.shape[0]
    offs = np.cumsum((0,) + IN_SPLITS)
    head = w_in[:, :offs[5]]
    wi = w_in[:, offs[5]:offs[6]]
    pad = jnp.zeros((d, KIWI_W - IDX_DIM - IDX_HEADS), w_in.dtype)
    return jnp.concatenate([head, wi, pad, w_in[:, offs[6]:]], axis=1).astype(BF16)


def _block_diag(pool_w):
    g, c, _ = pool_w.shape
    out = jnp.zeros((g * c, g * c), pool_w.dtype)
    for j in range(g):
        out = out.at[j * c:(j + 1) * c, j * c:(j + 1) * c].set(pool_w[j])
    return out.astype(BF16)


def kernel(x, c, positions, w_ada, b_ada, norm1_g, norm2_g, w_in, w_out, pool_w, pool_scale,
           hgrn_norm_g, lb_logits, w_router, b_router, w1, b1, w2, b2, final_g):
    bsz, s, d = x.shape
    depth = w_ada.shape[0]
    mod = _ada_call(c, w_ada, b_ada)
    cos, sin = _rope_table_call(positions)
    h = x
    for l in range(depth):
        sh1, sc1, g1, sh2, sc2, g2 = [mod[l, :, j * d:(j + 1) * d].reshape(bsz, 1, d)
                                      for j in range(6)]
        qt, qit, wit, k, ki, vt, rest = _inproj_call(
            h, sh1, sc1, norm1_g[l].reshape(1, d), _pack_w_in(w_in[l]), cos, sin)
        ya = _dsa_call(qt, qit, wit, k, ki, vt)
        yb = _pool_call(rest, _block_diag(pool_w[l]), pool_scale[l].reshape(1, B_WIDTH))
        yc = _hgrn_call(rest, lb_logits, hgrn_norm_g[l], l)
        wr = jnp.pad(w_router[l], ((0, 0), (0, LANES - N_EXPERTS)))
        br = jnp.pad(b_router[l], (0, LANES - N_EXPERTS)).reshape(1, LANES)
        h1, u2, code, gk, counts = _outproj_router_call(
            ya, yb, yc, h, g1, w_out[l].astype(BF16), sh2, sc2, norm2_g[l].reshape(1, d), wr, br)
        h = _moe_call(u2, code, gk, counts, h1, g2, w1, b1[l], w2, b2[l],
                      final_g.reshape(1, d), l, final_norm=(l == depth - 1))
    return h
```

```python
import functools
import math

import numpy as np
import jax
import jax.numpy as jnp
from jax import lax
from jax.experimental import pallas as pl
from jax.experimental.pallas import tpu as pltpu

F32 = jnp.float32
BF16 = jnp.bfloat16
I32 = jnp.int32

CHUNK = 64
EPS = 1e-6
NEG_BIG = -1e30
A_HEADS, A_KV_HEADS, HEAD_DIM = 8, 2, 64
IDX_HEADS, IDX_DIM, IDX_TOPK_MAX = 4, 64, 256
Q_BLOCK = 128
ROPE_THETA = 10000.0
A_WIDTH = A_HEADS * HEAD_DIM
POOL_WINDOWS = (2, 4, 8, 16)
POOL_GROUP = 64
B_WIDTH = len(POOL_WINDOWS) * POOL_GROUP
C_HEADS, C_KDIM, C_VDIM = 4, 64, 64
C_WIDTH = C_HEADS * C_VDIM
N_EXPERTS, TOP_K = 32, 4
SWIGLU_LIMIT, SWIGLU_ALPHA = 7.0, 1.702
KV_WIDTH = A_KV_HEADS * HEAD_DIM
IN_SPLITS = (A_WIDTH, KV_WIDTH, KV_WIDTH, IDX_HEADS * IDX_DIM, IDX_DIM, IDX_HEADS,
             B_WIDTH, C_HEADS * C_KDIM, C_HEADS * C_KDIM, C_WIDTH, C_WIDTH)

LANES = 128
SUBLANES = 8
INT_MIN = -(2 ** 31)
VMEM_LIMIT = 56 * 1024 * 1024

KIWI_W = LANES
REST_W = B_WIDTH + 4 * C_WIDTH
PACK_W = A_WIDTH + 2 * KV_WIDTH + IDX_HEADS * IDX_DIM + KIWI_W + REST_W
HGRN_LEVELS = (32, 16, 8, 4, 2, 1)


def _nt(a, b):
    return lax.dot_general(a, b, (((1,), (1,)), ((), ())), preferred_element_type=F32)


def _tn(a, b):
    return lax.dot_general(a, b, (((0,), (0,)), ((), ())), preferred_element_type=F32)


def _mm(a, b):
    return jnp.dot(a, b, preferred_element_type=F32)


def _cparams(sem):
    return pltpu.CompilerParams(dimension_semantics=sem, vmem_limit_bytes=VMEM_LIMIT)


def _ada_kernel(c_ref, w_ref, b_ref, o_ref):
    c = c_ref[...]
    ca = c * jax.nn.sigmoid(c)
    o_ref[...] = jnp.dot(ca, w_ref[...], precision=lax.Precision.HIGHEST,
                         preferred_element_type=F32) + b_ref[...]


def _ada_call(c, w_ada, b_ada):
    depth, d, d6 = w_ada.shape
    b = c.shape[0]
    nblk = d6 // d
    return pl.pallas_call(
        _ada_kernel,
        out_shape=jax.ShapeDtypeStruct((depth, b, d6), F32),
        grid=(depth, nblk),
        in_specs=[pl.BlockSpec((b, d), lambda l, j: (0, 0)),
                  pl.BlockSpec((None, d, d), lambda l, j: (l, 0, j)),
                  pl.BlockSpec((None, 1, d), lambda l, j: (l, 0, j))],
        out_specs=pl.BlockSpec((None, b, d), lambda l, j: (l, 0, j)),
        compiler_params=_cparams(("arbitrary", "arbitrary")),
        name="ada_mod",
    )(c, w_ada, b_ada.reshape(depth, 1, d6))


def _rope_table_kernel(pos_ref, inv_ref, sign_ref, cos_ref, sin_ref):
    ang = pos_ref[...].astype(F32) * inv_ref[...]
    cos_ref[...] = jnp.cos(ang)
    sin_ref[...] = jnp.sin(ang) * sign_ref[...]


def _rope_table_call(positions, ts=512):
    b, s = positions.shape
    half = HEAD_DIM // 2
    inv = jnp.power(jnp.float32(ROPE_THETA), -jnp.arange(0, HEAD_DIM, 2, dtype=F32) / HEAD_DIM)
    inv128 = jnp.tile(inv, LANES // half).reshape(1, LANES)
    sign128 = jnp.tile(jnp.concatenate([-jnp.ones((half,), F32), jnp.ones((half,), F32)]),
                       LANES // HEAD_DIM).reshape(1, LANES)
    spec = pl.BlockSpec((None, ts, LANES), lambda bi, i: (bi, i, 0))
    return pl.pallas_call(
        _rope_table_kernel,
        out_shape=(jax.ShapeDtypeStruct((b, s, LANES), F32),) * 2,
        grid=(b, s // ts),
        in_specs=[pl.BlockSpec((None, ts, 1), lambda bi, i: (bi, i, 0)),
                  pl.BlockSpec((1, LANES), lambda bi, i: (0, 0)),
                  pl.BlockSpec((1, LANES), lambda bi, i: (0, 0))],
        out_specs=(spec, spec),
        compiler_params=_cparams(("arbitrary", "arbitrary")),
        name="rope_table",
    )(positions.reshape(b, s, 1), inv128, sign128)


def _rope_tile(x, cos, sin_signed, first_half):
    partner = jnp.where(first_half, pltpu.roll(x, LANES - HEAD_DIM // 2, 1),
                        pltpu.roll(x, HEAD_DIM // 2, 1))
    return x * cos + partner * sin_signed


def _ada_norm(x, g, sc, sh):
    y = x * lax.rsqrt(jnp.mean(x * x, axis=-1, keepdims=True) + EPS)
    return (y * g) * (1.0 + sc) + sh


VT_W = 256
INPROJ_TILE = 512
WI_ROWS = SUBLANES


def _inproj_kernel(h_ref, sh_ref, sc_ref, g_ref, w_ref, cos_ref, sin_ref,
                   qt_ref, qit_ref, wit_ref, k_ref, ki_ref, vt_ref, rest_ref):
    u = _ada_norm(h_ref[...], g_ref[...], sc_ref[...], sh_ref[...]).astype(BF16)
    cos = cos_ref[...]
    sin = sin_ref[...]
    lane = lax.broadcasted_iota(I32, cos.shape, 1)
    first_half = (lane % HEAD_DIM) < (HEAD_DIM // 2)
    rope = lambda x: _rope_tile(x, cos, sin, first_half)

    off = 0
    q_scale = HEAD_DIM ** -0.5
    for j in range(A_WIDTH // LANES):
        z = _mm(u, w_ref[:, off + j * LANES: off + (j + 1) * LANES])
        qt_ref[j * LANES:(j + 1) * LANES, :] = (rope(z) * q_scale).T.astype(BF16)
    off += A_WIDTH
    k_ref[...] = rope(_mm(u, w_ref[:, off:off + KV_WIDTH])).astype(BF16)
    off += KV_WIDTH
    vz = _mm(u, w_ref[:, off:off + KV_WIDTH])
    for j in range(vt_ref.shape[0]):
        vt_ref[j] = vz[j * VT_W:(j + 1) * VT_W].T.astype(BF16)
    off += KV_WIDTH
    for j in range(IDX_HEADS * IDX_DIM // LANES):
        z = _mm(u, w_ref[:, off + j * LANES: off + (j + 1) * LANES])
        qit_ref[j * LANES:(j + 1) * LANES, :] = rope(z).T.astype(BF16)
    off += IDX_HEADS * IDX_DIM
    z = _mm(u, w_ref[:, off:off + KIWI_W])
    ki_ref[...] = rope(z).astype(BF16)
    wi_scale = (IDX_HEADS * IDX_DIM) ** -0.5
    wit_ref[...] = (z * wi_scale).T[IDX_DIM:IDX_DIM + WI_ROWS, :]
    off += KIWI_W
    rest_ref[...] = _mm(u, w_ref[:, off:off + REST_W])


def _inproj_call(h, sh, sc, g, w_pack, cos, sin):
    b, s, d = h.shape
    tm = min(INPROJ_TILE, s)
    tok = lambda w: pl.BlockSpec((None, tm, w), lambda bi, i: (bi, i, 0))
    tr = lambda r: pl.BlockSpec((None, r, tm), lambda bi, i: (bi, 0, i))
    per_b = pl.BlockSpec((None, 1, d), lambda bi, i: (bi, 0, 0))
    sds = jax.ShapeDtypeStruct
    return pl.pallas_call(
        _inproj_kernel,
        out_shape=(sds((b, A_WIDTH, s), BF16), sds((b, IDX_HEADS * IDX_DIM, s), BF16),
                   sds((b, WI_ROWS, s), F32), sds((b, s, KV_WIDTH), BF16),
                   sds((b, s, KIWI_W), BF16), sds((b, s // VT_W, KV_WIDTH, VT_W), BF16),
                   sds((b, s, REST_W), F32)),
        grid=(b, s // tm),
        in_specs=[tok(d), per_b, per_b,
                  pl.BlockSpec((1, d), lambda bi, i: (0, 0)),
                  pl.BlockSpec((d, PACK_W), lambda bi, i: (0, 0)),
                  tok(LANES), tok(LANES)],
        out_specs=(tr(A_WIDTH), tr(IDX_HEADS * IDX_DIM), tr(WI_ROWS), tok(KV_WIDTH), tok(KIWI_W),
                   pl.BlockSpec((None, tm // VT_W, KV_WIDTH, VT_W), lambda bi, i: (bi, i, 0, 0)),
                   tok(REST_W)),
        compiler_params=_cparams(("arbitrary", "arbitrary")),
        name="inproj",
    )(h, sh, sc, g, w_pack, cos, sin)


CNT_ROWS = 32
CNT_ACCS = 4
ONES_ROWS = 16
TIE_BLOCK = 128


def _dsa_kernel(qt_ref, qit_ref, wit_ref, k_ref, ki_ref, vt_ref, lower_ref, o_ref,
                keys_ref, hi_ref, bias_ref, s_ref, acc_ref, *, kc, topk):
    qb = Q_BLOCK
    i = pl.program_id(1)
    t0 = i * qb
    nkc = (t0 + qb + kc - 1) // kc
    lane = lax.broadcasted_iota(I32, (1, qb), 1)
    limit = t0 + (lane // CHUNK + 1) * CHUNK
    key_off = lax.broadcasted_iota(I32, (kc, qb), 0)
    group = A_HEADS // A_KV_HEADS

    qit = qit_ref[...]
    qi_stack = jnp.concatenate(
        [qit[h * IDX_DIM:(h + 1) * IDX_DIM, :] for h in range(IDX_HEADS)], axis=1)
    wit = wit_ref[...]

    def score_body(c, carry):
        off = pl.multiple_of(c * kc, kc)
        ki = ki_ref[pl.ds(off, kc), 0:IDX_DIM]
        s = jnp.maximum(_mm(ki, qi_stack), 0.0)
        score = s[:, 0:qb] * wit[0:1, :]
        for h in range(1, IDX_HEADS):
            score = score + s[:, h * qb:(h + 1) * qb] * wit[h:h + 1, :]
        score = jnp.where(score == 0.0, 0.0, score)
        bits = lax.bitcast_convert_type(score, I32)
        key = jnp.where(bits < 0, bits ^ jnp.int32(0x7FFFFFFF), bits)
        adm = off + key_off < limit
        keys_ref[pl.ds(off, kc), :] = jnp.where(adm, key, jnp.int32(INT_MIN))
        top = lax.bitcast_convert_type(bits & jnp.int32(-65536), F32)
        hi_ref[pl.ds(off, kc), :] = jnp.where(adm, top, -jnp.inf).astype(BF16)
        return carry

    lax.fori_loop(0, nkc, score_body, 0)

    def count_in(ref, dtype, pred):
        def body(c, accs):
            off = pl.multiple_of(c * kc, kc)
            blk = ref[pl.ds(off, kc), :]
            accs = list(accs)
            for j in range(kc // CNT_ROWS):
                a = accs[j % CNT_ACCS]
                hit = pred(blk[j * CNT_ROWS:(j + 1) * CNT_ROWS])
                accs[j % CNT_ACCS] = jnp.where(hit, a + 1.0, a)
            return tuple(accs)
        accs = lax.fori_loop(0, nkc, body,
                             tuple(jnp.zeros((CNT_ROWS, qb), dtype) for _ in range(CNT_ACCS)))
        total = accs[0].astype(F32)
        for a in accs[1:]:
            total = total + a.astype(F32)
        return jnp.sum(total, axis=0, keepdims=True)

    def count_hi(cand):
        return count_in(hi_ref, BF16, lambda v: v >= cand)

    def count(pred):
        return count_in(keys_ref, F32, pred)

    topk_f = float(topk)
    cnt0 = count(lambda kk: kk >= 0)
    cnt1 = count(lambda kk: kk >= 1)
    nonneg = cnt0 >= topk_f
    thr0 = jnp.where(nonneg, 0, INT_MIN).astype(I32)
    cthr0 = jnp.where(nonneg, cnt0, 2.0 * kc * (nkc + 1).astype(F32))
    done0 = jnp.where((nonneg & (cnt1 < topk_f)) | (cthr0 == topk_f) | (limit <= topk), 1.0, 0.0)

    def advance(state, cand, cnt):
        thr, cthr, done = state
        take = (cnt >= topk_f) & (done < 0.5)
        thr = jnp.where(take, cand, thr)
        cthr = jnp.where(take, cnt, cthr)
        return thr, cthr, jnp.where(cthr == topk_f, 1.0, done)

    def hi_body(j, state):
        cand = state[0] + jnp.left_shift(jnp.int32(1), 30 - j)
        h = cand >> 16
        pattern = jnp.where(h >= 0, h, h ^ jnp.int32(0x7FFF)) & jnp.int32(0xFFFF)
        cand_hi = lax.bitcast_convert_type(pattern << 16, F32).astype(BF16)
        return advance(state, cand, count_hi(cand_hi))

    state16 = lax.fori_loop(0, 15, hi_body, (thr0, cthr0, done0))

    steps = 4

    def lo_cond(carry):
        g, state = carry
        return (g * steps < 16) & (jnp.min(state[2]) < 0.5)

    def lo_body(carry):
        g, state = carry
        for jj in range(steps):
            cand = state[0] + jnp.left_shift(jnp.int32(1), 15 - (g * steps + jj))
            state = advance(state, cand, count(lambda kk: kk >= cand))
        return g + 1, state

    _, (thr, _, _) = lax.while_loop(lo_cond, lo_body, (jnp.int32(0), state16))

    need = topk_f - count(lambda kk: kk > thr)
    lower = lower_ref[...]
    tie_off = lax.broadcasted_iota(I32, (TIE_BLOCK, qb), 0)

    def bias_body(c, seen):
        for j in range(kc // TIE_BLOCK):
            off = pl.multiple_of(c * kc + j * TIE_BLOCK, TIE_BLOCK)
            kk = keys_ref[pl.ds(off, TIE_BLOCK), :]
            tie = kk == thr
            tie_f = jnp.where(tie, 1.0, 0.0)
            rank = _mm(lower, tie_f.astype(BF16)) + seen
            sel = ((kk > thr) | (tie & (rank <= need))) & (off + tie_off < limit)
            bias_ref[pl.ds(off, TIE_BLOCK), :] = jnp.where(sel, 0.0, NEG_BIG)
            seen = seen + jnp.sum(tie_f, axis=0, keepdims=True)
        return seen

    lax.fori_loop(0, nkc, bias_body, jnp.zeros((1, qb), F32))

    qt = qt_ref[...]
    q_n = [jnp.concatenate([qt[(n * group + g) * HEAD_DIM:(n * group + g + 1) * HEAD_DIM, :]
                            for g in range(group)], axis=1) for n in range(A_KV_HEADS)]

    def max_body(c, parts):
        out = list(parts)
        for j in range(kc // VT_W):
            off = pl.multiple_of(c * kc + j * VT_W, VT_W)
            bias = bias_ref[pl.ds(off, VT_W), :]
            for n in range(A_KV_HEADS):
                s = _mm(k_ref[pl.ds(off, VT_W), n * HEAD_DIM:(n + 1) * HEAD_DIM], q_n[n])
                s = jnp.concatenate([s[:, g * qb:(g + 1) * qb] + bias for g in range(group)], axis=1)
                s_ref[c * (kc // VT_W) + j, n] = s
                out[n] = jnp.maximum(out[n], jnp.max(
                    s.reshape(VT_W // SUBLANES, SUBLANES, group * qb), axis=0))
        return tuple(out)

    parts = lax.fori_loop(
        0, nkc, max_body,
        tuple(jnp.full((SUBLANES, group * qb), NEG_BIG, F32) for _ in range(A_KV_HEADS)))
    m_n = [jnp.max(p, axis=0, keepdims=True) for p in parts]

    acc_ref[...] = jnp.zeros(acc_ref.shape, F32)
    ones = jnp.ones((ONES_ROWS, VT_W), BF16)

    def pv_body(c, carry):
        for j in range(kc // VT_W):
            cv = c * (kc // VT_W) + j
            for n in range(A_KV_HEADS):
                pt = jnp.exp(s_ref[cv, n] - m_n[n]).astype(BF16)
                vt = jnp.concatenate([vt_ref[cv, n * HEAD_DIM:(n + 1) * HEAD_DIM, :], ones], axis=0)
                acc_ref[n] += _mm(vt, pt)
        return carry

    lax.fori_loop(0, nkc, pv_body, 0)
    for n in range(A_KV_HEADS):
        for g in range(group):
            hh = n * group + g
            cols = slice(g * qb, (g + 1) * qb)
            o_ref[hh * HEAD_DIM:(hh + 1) * HEAD_DIM, :] = (
                acc_ref[n, 0:HEAD_DIM, cols] / acc_ref[n, HEAD_DIM:HEAD_DIM + 1, cols])


def _dsa_call(qt, qit, wit, k, ki, vt, kc=512):
    b, _, s = qt.shape
    kc = min(kc, s)
    assert kc % VT_W == 0 and s % kc == 0
    topk = min(IDX_TOPK_MAX, s // 4)
    group = A_HEADS // A_KV_HEADS
    lower = jnp.asarray(np.tril(np.ones((TIE_BLOCK, TIE_BLOCK), np.float32)), BF16)
    qblk = lambda r: pl.BlockSpec((None, r, Q_BLOCK), lambda bi, i: (bi, 0, i))
    seq = lambda w: pl.BlockSpec((None, s, w), lambda bi, i: (bi, 0, 0))
    return pl.pallas_call(
        functools.partial(_dsa_kernel, kc=kc, topk=topk),
        out_shape=jax.ShapeDtypeStruct((b, A_WIDTH, s), F32),
        grid=(b, s // Q_BLOCK),
        in_specs=[qblk(A_WIDTH), qblk(IDX_HEADS * IDX_DIM), qblk(WI_ROWS),
                  seq(KV_WIDTH), seq(KIWI_W),
                  pl.BlockSpec((None, s // VT_W, KV_WIDTH, VT_W), lambda bi, i: (bi, 0, 0, 0)),
                  pl.BlockSpec((TIE_BLOCK, TIE_BLOCK), lambda bi, i: (0, 0))],
        out_specs=qblk(A_WIDTH),
        scratch_shapes=[pltpu.VMEM((s, Q_BLOCK), I32),
                        pltpu.VMEM((s, Q_BLOCK), BF16),
                        pltpu.VMEM((s, Q_BLOCK), F32),
                        pltpu.VMEM((s // VT_W, A_KV_HEADS, VT_W, group * Q_BLOCK), F32),
                        pltpu.VMEM((A_KV_HEADS, HEAD_DIM + ONES_ROWS, group * Q_BLOCK), F32)],
        compiler_params=_cparams(("arbitrary", "arbitrary")),
        name="dsa",
    )(qt, qit, wit, k, ki, vt, lower)


POOL_HALO = 32


def _pool_kernel(u_ref, w_ref, scale_ref, o_ref, x_buf, a_buf, b_buf, *, tm):
    hl = POOL_HALO
    rows = tm + hl
    first = pl.program_id(1) == 0

    @pl.when(first)
    def _():
        x_buf[0:hl, :] = jnp.zeros((hl, B_WIDTH), F32)

    x = u_ref[...]
    x_buf[hl:rows, :] = x
    a_buf[8:rows, :] = x_buf[8:rows, :] + x_buf[7:rows - 1, :]
    b_buf[16:rows, :] = a_buf[16:rows, :] + a_buf[14:rows - 2, :]
    w2 = a_buf[hl:rows, :]
    w4 = b_buf[hl:rows, :]
    a_buf[24:rows, :] = b_buf[24:rows, :] + b_buf[20:rows - 4, :]
    w8 = a_buf[hl:rows, :]
    b_buf[hl:rows, :] = a_buf[hl:rows, :] + a_buf[hl - 8:rows - 8, :]
    w16 = b_buf[hl:rows, :]
    x_buf[0:hl, :] = x_buf[tm:rows, :]

    lane = lax.broadcasted_iota(I32, (tm, B_WIDTH), 1)
    grp = lane // POOL_GROUP
    wsum = jnp.where(grp == 0, w2, jnp.where(grp == 1, w4, jnp.where(grp == 2, w8, w16)))
    win = jnp.where(grp == 0, 2, jnp.where(grp == 1, 4, jnp.where(grp == 2, 8, 16)))
    t = pl.program_id(1) * tm + lax.broadcasted_iota(I32, (tm, B_WIDTH), 0)
    cnt = jnp.minimum(t + 1, win).astype(F32)
    pooled = wsum / cnt - x
    y = _mm(pooled.astype(BF16), w_ref[...])
    o_ref[...] = y * scale_ref[...]


def _pool_call(rest, w_bd, scale, tm=512):
    b, s, _ = rest.shape
    tm = min(tm, s)
    rows = tm + POOL_HALO
    return pl.pallas_call(
        functools.partial(_pool_kernel, tm=tm),
        out_shape=jax.ShapeDtypeStruct((b, s, B_WIDTH), F32),
        grid=(b, s // tm),
        in_specs=[pl.BlockSpec((None, tm, B_WIDTH), lambda bi, i: (bi, i, 0)),
                  pl.BlockSpec((B_WIDTH, B_WIDTH), lambda bi, i: (0, 0)),
                  pl.BlockSpec((1, B_WIDTH), lambda bi, i: (0, 0))],
        out_specs=pl.BlockSpec((None, tm, B_WIDTH), lambda bi, i: (bi, i, 0)),
        scratch_shapes=[pltpu.VMEM((rows, B_WIDTH), F32)] * 3,
        compiler_params=_cparams(("arbitrary", "arbitrary")),
        name="pool",
    )(rest, w_bd, scale)


def _hgrn_consts():
    tril = np.tril(np.ones((CHUNK, CHUNK), np.float32))
    mats = [tril]
    r = np.arange(CHUNK)
    for h in HGRN_LEVELS:
        mats.append(tril[(r // (2 * h)) * (2 * h) + h - 1])
    return np.concatenate(mats, axis=0)


def _split3(x):
    hi = x.astype(BF16)
    r1 = x - hi.astype(F32)
    mid = r1.astype(BF16)
    lo = (r1 - mid.astype(F32)).astype(BF16)
    return hi, mid, lo


def _hgrn_kernel(q_ref, f_ref, i_ref, g_ref, lb_ref, ng_ref, cm_ref, bd_ref, o_ref, state_ref,
                 *, layer, tm):
    @pl.when(pl.program_id(1) == 0)
    def _():
        state_ref[...] = jnp.zeros(state_ref.shape, F32)

    lbl = lb_ref[...]
    e = jnp.exp(lbl - jnp.max(lbl, axis=0, keepdims=True))
    p = e / jnp.sum(e, axis=0, keepdims=True)
    cum = p[0:1]
    for l in range(1, layer + 1):
        cum = cum + p[l:l + 1]
    lb = jnp.clip(cum - p[0:1], 0.0, 1.0)

    cm = cm_ref[...]
    ng = ng_ref[...]
    bd = bd_ref[...]
    bd_f = bd.astype(F32)
    row = lax.broadcasted_iota(I32, (CHUNK, 1), 0)
    tt = lax.broadcasted_iota(I32, (CHUNK, C_WIDTH), 0)
    ss = lax.broadcasted_iota(I32, (CHUNK, C_WIDTH), 1) % CHUNK
    lvl_mask = [tt == ss] + [(tt // (2 * h)) == (ss // (2 * h)) for h in HGRN_LEVELS]
    w = C_WIDTH
    heads = C_WIDTH // C_KDIM

    def expand(x16):
        return jnp.concatenate([x16] * heads, axis=0) * bd

    chunks = range(tm // CHUNK)
    rows = [slice(ci * CHUNK, (ci + 1) * CHUNK) for ci in chunks]
    z = f_ref[...]
    log_f = jnp.log(lb + (1.0 - lb) * jax.nn.sigmoid(z))
    kin = (1.0 - lb) * jax.nn.sigmoid(-z)
    qx = q_ref[...]
    qv = qx * jax.nn.sigmoid(qx)
    vb16 = i_ref[...].astype(BF16)
    hi, mid, lo = _split3(log_f)
    lf3 = jnp.concatenate([hi, mid, lo], axis=1)
    cs = [_mm(cm, lf3[r]) for r in rows]
    cs = [c[:, 0:w] + c[:, w:2 * w] + c[:, 2 * w:3 * w] for c in cs]
    odd = [((row // h) % 2) == 1 for h in HGRN_LEVELS]
    attn, q_dec, upd, s_dec = [], [], [], []
    for ci in chunks:
        r = rows[ci]
        bcum = cs[ci][0:CHUNK]
        b_last = bcum[CHUNK - 1:CHUNK]
        q_c, k_c = qv[r], kin[r]
        q_dec.append((q_c * jnp.exp(bcum)).astype(BF16))
        s_dec.append(jnp.exp(b_last))
        upd.append(_tn(vb16[r], (k_c * jnp.exp(b_last - bcum)).astype(BF16)) * bd_f)
        qs, ks = [q_c.astype(BF16)], [k_c.astype(BF16)]
        for li in range(len(HGRN_LEVELS)):
            ref = cs[ci][(li + 1) * CHUNK:(li + 2) * CHUNK]
            qs.append((q_c * jnp.exp(jnp.where(odd[li], bcum - ref, NEG_BIG))).astype(BF16))
            ks.append((k_c * jnp.exp(jnp.where(odd[li], NEG_BIG, ref - bcum))).astype(BF16))
        a = jnp.zeros((CHUNK, w), F32)
        for mask, ql, kl in zip(lvl_mask, qs, ks):
            a = a + jnp.where(mask, _nt(ql, expand(kl)), 0.0)
        attn.append(a)
    o_intra = [_mm(attn[ci].astype(BF16), expand(vb16[rows[ci]])) for ci in chunks]
    st = state_ref[...]
    outs = []
    for ci in chunks:
        outs.append(_nt(q_dec[ci], st.astype(BF16)) + o_intra[ci])
        st = st * s_dec[ci] + upd[ci]
    state_ref[...] = st
    o = jnp.concatenate(outs, axis=0)
    o2h, o2m, o2l = _split3(o * o)
    ms = (_mm(o2h, bd) + _mm(o2m, bd) + _mm(o2l, bd)) * (1.0 / C_VDIM)
    gx = g_ref[...]
    o_ref[...] = (o * lax.rsqrt(ms + EPS) * ng) * (gx * jax.nn.sigmoid(gx))


def _hgrn_call(rest, lb_logits, norm_g, layer, tm=256):
    b, s, _ = rest.shape
    depth = lb_logits.shape[0]
    cm = jnp.asarray(_hgrn_consts(), BF16)
    head_of = np.arange(C_WIDTH) // C_KDIM
    bd = jnp.asarray(head_of[:, None] == head_of[None, :], BF16)
    col = lambda j: pl.BlockSpec((None, tm, C_WIDTH), lambda bi, i, j=j: (bi, i, j))
    return pl.pallas_call(
        functools.partial(_hgrn_kernel, layer=layer, tm=tm),
        out_shape=jax.ShapeDtypeStruct((b, s, C_WIDTH), F32),
        grid=(b, s // tm),
        in_specs=[col(1), col(2), col(3), col(4),
                  pl.BlockSpec((depth, C_WIDTH), lambda bi, i: (0, 0)),
                  pl.BlockSpec((1, C_WIDTH), lambda bi, i: (0, 0)),
                  pl.BlockSpec(cm.shape, lambda bi, i: (0, 0)),
                  pl.BlockSpec((C_WIDTH, C_WIDTH), lambda bi, i: (0, 0))],
        out_specs=pl.BlockSpec((None, tm, C_WIDTH), lambda bi, i: (bi, i, 0)),
        scratch_shapes=[pltpu.VMEM((C_WIDTH, C_WIDTH), F32)],
        compiler_params=_cparams(("arbitrary", "arbitrary")),
        name="hgrn2",
    )(rest, rest, rest, rest, lb_logits, jnp.tile(norm_g, C_HEADS).reshape(1, C_WIDTH), cm, bd)


RANK_BITS = 20
CODE_ROWS = 8


def _outproj_router_kernel(ya_ref, yb_ref, yc_ref, h_ref, g1_ref, wo_ref, sh_ref, sc_ref, g_ref,
                           wr_ref, br_ref, tri_ref, h1_ref, u_ref, code_ref, gk_ref, cnt_ref,
                           run_ref):
    @pl.when((pl.program_id(0) == 0) & (pl.program_id(1) == 0))
    def _():
        run_ref[...] = jnp.zeros(run_ref.shape, F32)

    y = _tn(ya_ref[...].astype(BF16), wo_ref[0:A_WIDTH, :])
    y = y + _mm(yb_ref[...].astype(BF16), wo_ref[A_WIDTH:A_WIDTH + B_WIDTH, :])
    y = y + _mm(yc_ref[...].astype(BF16), wo_ref[A_WIDTH + B_WIDTH:, :])
    h1 = h_ref[...] + g1_ref[...] * y
    h1_ref[...] = h1
    u = _ada_norm(h1, g_ref[...], sc_ref[...], sh_ref[...])
    u_ref[...] = u

    u_hi = u.astype(BF16)
    u_lo = (u - u_hi.astype(F32)).astype(BF16)
    wr = wr_ref[...]
    wr_hi = wr.astype(BF16)
    wr_lo = (wr - wr_hi.astype(F32)).astype(BF16)
    logits = _mm(u_hi, wr_hi) + (_mm(u_hi, wr_lo) + _mm(u_lo, wr_hi)) + br_ref[...]
    lane = lax.broadcasted_iota(I32, logits.shape, 1).astype(F32)
    work = jnp.where(lane < N_EXPERTS, logits, -jnp.inf)
    picks, firsts, tops = [], [], []
    for k in range(TOP_K):
        m = jnp.max(work, axis=1, keepdims=True)
        first = jnp.min(jnp.where(work == m, lane, float(LANES)), axis=1, keepdims=True)
        pick = lane == first
        work = jnp.where(pick, -jnp.inf, work)
        picks.append(pick)
        firsts.append(first)
        tops.append(m)
    ex = [jnp.exp(m - tops[0]) for m in tops]
    den = ex[0] + ex[1] + ex[2] + ex[3]

    sel = picks[0] | picks[1] | picks[2] | picks[3]
    sel_f = jnp.where(sel, 1.0, 0.0)
    prefix = _mm(tri_ref[...], sel_f.astype(BF16)) + run_ref[...]
    run_ref[...] += jnp.sum(sel_f, axis=0, keepdims=True)
    cnt_ref[...] = run_ref[...]

    code = jnp.zeros(logits.shape, I32)
    gk = jnp.zeros(logits.shape, F32)
    for k in range(TOP_K):
        rank = jnp.sum(jnp.where(picks[k], prefix, 0.0), axis=1, keepdims=True)
        ck = (firsts[k].astype(I32) << RANK_BITS) | rank.astype(I32)
        code = jnp.where(lane == float(k), ck, code)
        gk = jnp.where(lane == float(k), ex[k] / den, gk)
    code_ref[...] = code.T[0:CODE_ROWS, :]
    gk_ref[...] = gk


def _outproj_router_call(ya, yb, yc, h, g1, w_out, sh, sc, g, w_router, b_router, tm=512):
    b, s, d = h.shape
    tm = min(tm, s)
    tri = jnp.asarray(np.tril(np.ones((tm, tm), np.float32), -1), BF16)
    tok = lambda w: pl.BlockSpec((None, tm, w), lambda bi, i: (bi, i, 0))
    per_b = pl.BlockSpec((None, 1, d), lambda bi, i: (bi, 0, 0))
    full = lambda a: pl.BlockSpec(a.shape, lambda bi, i: (0,) * a.ndim)
    return pl.pallas_call(
        _outproj_router_kernel,
        out_shape=(jax.ShapeDtypeStruct((b, s, d), F32), jax.ShapeDtypeStruct((b, s, d), F32),
                   jax.ShapeDtypeStruct((CODE_ROWS, b * s), I32),
                   jax.ShapeDtypeStruct((b, s, LANES), F32), jax.ShapeDtypeStruct((1, LANES), F32)),
        grid=(b, s // tm),
        in_specs=[pl.BlockSpec((None, A_WIDTH, tm), lambda bi, i: (bi, 0, i)),
                  tok(B_WIDTH), tok(C_WIDTH), tok(d), per_b, full(w_out),
                  per_b, per_b, full(g), full(w_router), full(b_router), full(tri)],
        out_specs=(tok(d), tok(d),
                   pl.BlockSpec((CODE_ROWS, tm), lambda bi, i: (0, bi * (s // tm) + i)),
                   tok(LANES), pl.BlockSpec((1, LANES), lambda bi, i: (0, 0))),
        scratch_shapes=[pltpu.VMEM((1, LANES), F32)],
        compiler_params=_cparams(("arbitrary", "arbitrary")),
        name="outproj_router",
    )(ya, yb, yc, h, g1, w_out, sh, sc, g, w_router, b_router, tri)


MOE_BLOCK = 512
PERM_W = 2 * LANES


ROWS_TILE = 2048


def _rows_kernel(start_ref, code_ref, row_ref):
    code = code_ref[...]
    expert = code >> RANK_BITS
    base = jnp.zeros(code.shape, I32)
    for e in range(N_EXPERTS):
        base = jnp.where(expert == e, start_ref[e], base)
    row_ref[...] = base + (code & ((1 << RANK_BITS) - 1))


def _rows_call(starts, code):
    r, n = code.shape
    tile = min(ROWS_TILE, n)
    return pl.pallas_call(
        _rows_kernel,
        out_shape=jax.ShapeDtypeStruct((r, n), I32),
        grid_spec=pltpu.PrefetchScalarGridSpec(
            num_scalar_prefetch=1, grid=(n // tile,),
            in_specs=[pl.BlockSpec((r, tile), lambda i, *_: (0, i))],
            out_specs=pl.BlockSpec((r, tile), lambda i, *_: (0, i))),
        compiler_params=_cparams(("arbitrary",)),
        name="moe_rows",
    )(starts, code)


def _drain_rows(src_row, dst_row, sem, n):
    def body(t, c):
        pltpu.make_async_copy(src_row, dst_row, sem).wait()
        return c
    lax.fori_loop(0, n, body, 0, unroll=4)


ZERO_ROWS = MOE_BLOCK // 2


def _dispatch_kernel(row_ref, padlo_ref, padhi_ref, x_ref, xs_ref, zbuf, sem, zsem, *, tm):
    tok0 = pl.program_id(0) * tm
    n_tok = row_ref.shape[0] // TOP_K

    @pl.when(pl.program_id(0) == 0)
    def _():
        zbuf[...] = jnp.zeros(zbuf.shape, F32)

        def pad_copies(e, fn):
            lo = padlo_ref[e]
            hi = padhi_ref[e]
            lo8 = jnp.minimum((lo + SUBLANES - 1) // SUBLANES * SUBLANES, hi)
            for j in range(SUBLANES - 1):

                @pl.when(lo + j < lo8)
                def _(j=j):
                    fn(pltpu.make_async_copy(zbuf.at[pl.ds(0, 1)], xs_ref.at[pl.ds(lo + j, 1)], zsem))
            n8 = hi - lo8
            size = ZERO_ROWS
            while size >= SUBLANES:
                off = pl.multiple_of(lo8 + (n8 & ~(2 * size - 1)), SUBLANES)

                @pl.when((n8 & size) != 0)
                def _(size=size, off=off):
                    fn(pltpu.make_async_copy(zbuf.at[pl.ds(0, size)], xs_ref.at[pl.ds(off, size)],
                                             zsem))
                size //= 2

        def tail_copies(fn):
            tail_lo = padhi_ref[N_EXPERTS - 1]

            def piece(j, c):
                off = pl.multiple_of(tail_lo + j * ZERO_ROWS, ZERO_ROWS)

                @pl.when(off < xs_ref.shape[0])
                def _():
                    fn(pltpu.make_async_copy(zbuf, xs_ref.at[pl.ds(off, ZERO_ROWS)], zsem))
                return c

            lax.fori_loop(0, N_EXPERTS * MOE_BLOCK // ZERO_ROWS, piece, 0)

        def start_e(e, c):
            pad_copies(e, lambda cp: cp.start())
            return c

        def wait_e(e, c):
            pad_copies(e, lambda cp: cp.wait())
            return c

        lax.fori_loop(0, N_EXPERTS, start_e, 0)
        tail_copies(lambda cp: cp.start())
        lax.fori_loop(0, N_EXPERTS, wait_e, 0)
        tail_copies(lambda cp: cp.wait())

    def issue(t8, c):
        base = pl.multiple_of(t8 * SUBLANES, SUBLANES)
        for j in range(SUBLANES):
            for k in range(TOP_K):
                dst = row_ref[k * n_tok + tok0 + base + j]
                pltpu.make_async_copy(x_ref.at[pl.ds(base + j, 1)], xs_ref.at[pl.ds(dst, 1)],
                                      sem).start(priority=k % 2)
        return c

    lax.fori_loop(0, tm // SUBLANES, issue, 0)
    _drain_rows(x_ref.at[pl.ds(0, 1)], xs_ref.at[pl.ds(0, 1)], sem, tm * TOP_K)


def _dispatch_call(row_ids, pad_lo, pad_hi, x, rows, tm=512):
    n, d = x.shape
    return pl.pallas_call(
        functools.partial(_dispatch_kernel, tm=tm),
        out_shape=jax.ShapeDtypeStruct((rows, d), F32),
        grid_spec=pltpu.PrefetchScalarGridSpec(
            num_scalar_prefetch=3, grid=(n // tm,),
            in_specs=[pl.BlockSpec((tm, d), lambda i, *_: (i, 0))],
            out_specs=pl.BlockSpec(memory_space=pl.ANY),
            scratch_shapes=[pltpu.VMEM((ZERO_ROWS, d), F32), pltpu.SemaphoreType.DMA,
                            pltpu.SemaphoreType.DMA]),
        compiler_params=_cparams(("arbitrary",)),
        name="moe_dispatch",
    )(row_ids, pad_lo, pad_hi, x)


def _ffn_kernel(blk_e_ref, nb_ref, xs_ref, w1_ref, b1g_ref, b1l_ref, w2_ref, b2_ref, perm_ref,
                ys_ref, w1g_s, w1l_s, w2_s):
    i = pl.program_id(0)
    e = blk_e_ref[i]
    live = i < nb_ref[0]
    fresh = (i == 0) | (e != blk_e_ref[jnp.maximum(i - 1, 0)])

    @pl.when(live & fresh)
    def _():
        perm = perm_ref[...]
        for j in range(w1_ref.shape[1] // PERM_W):
            t = _mm(w1_ref[:, j * PERM_W:(j + 1) * PERM_W].astype(BF16), perm).astype(BF16)
            w1g_s[:, j * LANES:(j + 1) * LANES] = t[:, 0:LANES]
            w1l_s[:, j * LANES:(j + 1) * LANES] = t[:, LANES:PERM_W]
        w2_s[...] = w2_ref[...].astype(BF16)

    @pl.when(live)
    def _():
        x = xs_ref[...].astype(BF16)
        glu = jnp.minimum(_mm(x, w1g_s[...]) + b1g_ref[...], SWIGLU_LIMIT)
        lin = jnp.clip(_mm(x, w1l_s[...]) + b1l_ref[...], -SWIGLU_LIMIT, SWIGLU_LIMIT)
        act = glu * jax.nn.sigmoid(SWIGLU_ALPHA * glu) * (lin + 1.0)
        ys_ref[...] = _mm(act.astype(BF16), w2_s[...]) + b2_ref[...]

    @pl.when(jnp.logical_not(live))
    def _():
        ys_ref[...] = jnp.zeros(ys_ref.shape, F32)


def _ffn_call(blk_e, nb_used, xs, w1, b1g, b1l, w2, b2, layer):
    rows, d = xs.shape
    _, ne, _, ff2 = w1.shape
    ff = ff2 // 2
    perm = np.zeros((PERM_W, PERM_W), np.float32)
    perm[2 * np.arange(LANES), np.arange(LANES)] = 1.0
    perm[2 * np.arange(LANES) + 1, LANES + np.arange(LANES)] = 1.0
    ex = lambda r, c: pl.BlockSpec((None, r, c), lambda i, be, nb: (be[i], 0, 0))
    exl = lambda r, c: pl.BlockSpec((None, None, r, c), lambda i, be, nb: (layer, be[i], 0, 0))
    return pl.pallas_call(
        _ffn_kernel,
        out_shape=jax.ShapeDtypeStruct((rows, d), F32),
        grid_spec=pltpu.PrefetchScalarGridSpec(
            num_scalar_prefetch=2, grid=(rows // MOE_BLOCK,),
            in_specs=[pl.BlockSpec((MOE_BLOCK, d),
                                   lambda i, be, nb: (jnp.minimum(i, jnp.maximum(nb[0] - 1, 0)), 0)),
                      exl(d, ff2), ex(1, ff), ex(1, ff), exl(ff, d), ex(1, d),
                      pl.BlockSpec((PERM_W, PERM_W), lambda i, *_: (0, 0))],
            out_specs=pl.BlockSpec((MOE_BLOCK, d), lambda i, *_: (i, 0)),
            scratch_shapes=[pltpu.VMEM((d, ff), BF16), pltpu.VMEM((d, ff), BF16),
                            pltpu.VMEM((ff, d), BF16)]),
        compiler_params=_cparams(("arbitrary",)),
        name="moe_ffn",
    )(blk_e, nb_used, xs, w1, b1g, b1l, w2, b2, jnp.asarray(perm, BF16))


def _combine_kernel(row_ref, gk_ref, h_ref, g2_ref, fg_ref, ys_ref, o_ref, buf, sem,
                    *, tm, final_norm):
    tok0 = pl.program_id(0) * tm
    n_tok = row_ref.shape[0] // TOP_K

    def issue(t8, c):
        base = pl.multiple_of(t8 * SUBLANES, SUBLANES)
        for j in range(SUBLANES):
            for k in range(TOP_K):
                src = row_ref[k * n_tok + tok0 + base + j]
                pltpu.make_async_copy(ys_ref.at[pl.ds(src, 1)], buf.at[k, pl.ds(base + j, 1)],
                                      sem).start(priority=k % 2)
        return c

    lax.fori_loop(0, tm // SUBLANES, issue, 0)
    _drain_rows(ys_ref.at[pl.ds(0, 1)], buf.at[0, pl.ds(0, 1)], sem, tm * TOP_K)

    gk = gk_ref[...]
    acc = buf[0] * gk[:, 0:1]
    for k in range(1, TOP_K):
        acc = acc + buf[k] * gk[:, k:k + 1]
    out = h_ref[...] + g2_ref[...] * acc
    if final_norm:
        out = out * lax.rsqrt(jnp.mean(out * out, axis=-1, keepdims=True) + EPS) * fg_ref[...]
    o_ref[...] = out


def _combine_call(row_ids, gk, h1, g2, final_g, ys, seq, final_norm, tm=256):
    n, d = h1.shape
    return pl.pallas_call(
        functools.partial(_combine_kernel, tm=tm, final_norm=final_norm),
        out_shape=jax.ShapeDtypeStruct((n, d), F32),
        grid_spec=pltpu.PrefetchScalarGridSpec(
            num_scalar_prefetch=1, grid=(n // tm,),
            in_specs=[pl.BlockSpec((tm, LANES), lambda i, *_: (i, 0)),
                      pl.BlockSpec((tm, d), lambda i, *_: (i, 0)),
                      pl.BlockSpec((None, 1, d), lambda i, *_: ((i * tm) // seq, 0, 0)),
                      pl.BlockSpec((1, d), lambda i, *_: (0, 0)),
                      pl.BlockSpec(memory_space=pl.ANY)],
            out_specs=pl.BlockSpec((tm, d), lambda i, *_: (i, 0)),
            scratch_shapes=[pltpu.VMEM((TOP_K, tm, d), F32), pltpu.SemaphoreType.DMA]),
        compiler_params=_cparams(("arbitrary",)),
        name="moe_combine",
    )(row_ids, gk, h1, g2, final_g, ys)


def _moe_call(u, code, gk, counts, h1, g2, w1, b1, w2, b2, final_g, layer, final_norm):
    b, s, d = h1.shape
    n = b * s
    ne = w1.shape[1]
    rows = n * TOP_K + ne * MOE_BLOCK
    cnt = counts[0, :ne].astype(I32)
    padded = (cnt + MOE_BLOCK - 1) // MOE_BLOCK * MOE_BLOCK
    ends = jnp.cumsum(padded)
    starts = (ends - padded).astype(I32)
    blk_start = jnp.arange(rows // MOE_BLOCK, dtype=I32) * MOE_BLOCK
    blk_e = jnp.minimum(jnp.sum((ends[None, :] <= blk_start[:, None]).astype(I32), axis=1), ne - 1)
    nb_used = (ends[-1:] // MOE_BLOCK).astype(I32)

    row_ids = _rows_call(starts, code)[0:TOP_K].reshape(TOP_K * n)

    xs = _dispatch_call(row_ids, starts + cnt, ends.astype(I32), u.reshape(n, d), rows)
    ff = w1.shape[3] // 2
    ys = _ffn_call(blk_e, nb_used, xs, w1, b1[:, 0::2].reshape(ne, 1, ff),
                   b1[:, 1::2].reshape(ne, 1, ff), w2, b2.reshape(ne, 1, d), layer)
    out = _combine_call(row_ids, gk.reshape(n, LANES), h1.reshape(n, d), g2, final_g, ys, s,
                        final_norm)
    return out.reshape(b, s, d)


def _pack_w_in(w_in):
    d = w_in.shape[0]
    offs = np.cumsum((0,) + IN_SPLITS)
    head = w_in[:, :offs[5]]
    wi = w_in[:, offs[5]:offs[6]]
    pad = jnp.zeros((d, KIWI_W - IDX_DIM - IDX_HEADS), w_in.dtype)
    return jnp.concatenate([head, wi, pad, w_in[:, offs[6]:]], axis=1).astype(BF16)


def _block_diag(pool_w):
    g, c, _ = pool_w.shape
    out = jnp.zeros((g * c, g * c), pool_w.dtype)
    for j in range(g):
        out = out.at[j * c:(j + 1) * c, j * c:(j + 1) * c].set(pool_w[j])
    return out.astype(BF16)


def kernel(x, c, positions, w_ada, b_ada, norm1_g, norm2_g, w_in, w_out, pool_w, pool_scale,
           hgrn_norm_g, lb_logits, w_router, b_router, w1, b1, w2, b2, final_g):
    bsz, s, d = x.shape
    depth = w_ada.shape[0]
    mod = _ada_call(c, w_ada, b_ada)
    cos, sin = _rope_table_call(positions)
    h = x
    for l in range(depth):
        sh1, sc1, g1, sh2, sc2, g2 = [mod[l, :, j * d:(j + 1) * d].reshape(bsz, 1, d)
                                      for j in range(6)]
        qt, qit, wit, k, ki, vt, rest = _inproj_call(
            h, sh1, sc1, norm1_g[l].reshape(1, d), _pack_w_in(w_in[l]), cos, sin)
        ya = _dsa_call(qt, qit, wit, k, ki, vt)
        yb = _pool_call(rest, _block_diag(pool_w[l]), pool_scale[l].reshape(1, B_WIDTH))
        yc = _hgrn_call(rest, lb_logits, hgrn_norm_g[l], l)
        wr = jnp.pad(w_router[l], ((0, 0), (0, LANES - N_EXPERTS)))
        br = jnp.pad(b_router[l], (0, LANES - N_EXPERTS)).reshape(1, LANES)
        h1, u2, code, gk, counts = _outproj_router_call(
            ya, yb, yc, h, g1, w_out[l].astype(BF16), sh2, sc2, norm2_g[l].reshape(1, d), wr, br)
        h = _moe_call(u2, code, gk, counts, h1, g2, w1, b1[l], w2, b2[l],
                      final_g.reshape(1, d), l, final_norm=(l == depth - 1))
    return h
```

```python
import functools
import math

import numpy as np
import jax
import jax.numpy as jnp
from jax import lax
from jax.experimental import pallas as pl
from jax.experimental.pallas import tpu as pltpu

F32 = jnp.float32
BF16 = jnp.bfloat16
I32 = jnp.int32

CHUNK = 64
EPS = 1e-6
NEG_BIG = -1e30
A_HEADS, A_KV_HEADS, HEAD_DIM = 8, 2, 64
IDX_HEADS, IDX_DIM, IDX_TOPK_MAX = 4, 64, 256
Q_BLOCK = 128
ROPE_THETA = 10000.0
A_WIDTH = A_HEADS * HEAD_DIM
POOL_WINDOWS = (2, 4, 8, 16)
POOL_GROUP = 64
B_WIDTH = len(POOL_WINDOWS) * POOL_GROUP
C_HEADS, C_KDIM, C_VDIM = 4, 64, 64
C_WIDTH = C_HEADS * C_VDIM
N_EXPERTS, TOP_K = 32, 4
SWIGLU_LIMIT, SWIGLU_ALPHA = 7.0, 1.702
KV_WIDTH = A_KV_HEADS * HEAD_DIM
IN_SPLITS = (A_WIDTH, KV_WIDTH, KV_WIDTH, IDX_HEADS * IDX_DIM, IDX_DIM, IDX_HEADS,
             B_WIDTH, C_HEADS * C_KDIM, C_HEADS * C_KDIM, C_WIDTH, C_WIDTH)

LANES = 128
SUBLANES = 8
INT_MIN = -(2 ** 31)
VMEM_LIMIT = 56 * 1024 * 1024

KIWI_W = LANES
REST_W = B_WIDTH + 4 * C_WIDTH
PACK_W = A_WIDTH + 2 * KV_WIDTH + IDX_HEADS * IDX_DIM + KIWI_W + REST_W
HGRN_LEVELS = (32, 16, 8, 4, 2, 1)


def _nt(a, b):
    return lax.dot_general(a, b, (((1,), (1,)), ((), ())), preferred_element_type=F32)


def _tn(a, b):
    return lax.dot_general(a, b, (((0,), (0,)), ((), ())), preferred_element_type=F32)


def _mm(a, b):
    return jnp.dot(a, b, preferred_element_type=F32)


def _cparams(sem):
    return pltpu.CompilerParams(dimension_semantics=sem, vmem_limit_bytes=VMEM_LIMIT)


def _ada_kernel(c_ref, w_ref, b_ref, o_ref):
    c = c_ref[...]
    ca = c * jax.nn.sigmoid(c)
    o_ref[...] = jnp.dot(ca, w_ref[...], precision=lax.Precision.HIGHEST,
                         preferred_element_type=F32) + b_ref[...]


def _ada_call(c, w_ada, b_ada):
    depth, d, d6 = w_ada.shape
    b = c.shape[0]
    nblk = d6 // d
    return pl.pallas_call(
        _ada_kernel,
        out_shape=jax.ShapeDtypeStruct((depth, b, d6), F32),
        grid=(depth, nblk),
        in_specs=[pl.BlockSpec((b, d), lambda l, j: (0, 0)),
                  pl.BlockSpec((None, d, d), lambda l, j: (l, 0, j)),
                  pl.BlockSpec((None, 1, d), lambda l, j: (l, 0, j))],
        out_specs=pl.BlockSpec((None, b, d), lambda l, j: (l, 0, j)),
        compiler_params=_cparams(("arbitrary", "arbitrary")),
        name="ada_mod",
    )(c, w_ada, b_ada.reshape(depth, 1, d6))


def _rope_table_kernel(pos_ref, inv_ref, sign_ref, cos_ref, sin_ref):
    ang = pos_ref[...].astype(F32) * inv_ref[...]
    cos_ref[...] = jnp.cos(ang)
    sin_ref[...] = jnp.sin(ang) * sign_ref[...]


def _rope_table_call(positions, ts=512):
    b, s = positions.shape
    half = HEAD_DIM // 2
    inv = jnp.power(jnp.float32(ROPE_THETA), -jnp.arange(0, HEAD_DIM, 2, dtype=F32) / HEAD_DIM)
    inv128 = jnp.tile(inv, LANES // half).reshape(1, LANES)
    sign128 = jnp.tile(jnp.concatenate([-jnp.ones((half,), F32), jnp.ones((half,), F32)]),
                       LANES // HEAD_DIM).reshape(1, LANES)
    spec = pl.BlockSpec((None, ts, LANES), lambda bi, i: (bi, i, 0))
    return pl.pallas_call(
        _rope_table_kernel,
        out_shape=(jax.ShapeDtypeStruct((b, s, LANES), F32),) * 2,
        grid=(b, s // ts),
        in_specs=[pl.BlockSpec((None, ts, 1), lambda bi, i: (bi, i, 0)),
                  pl.BlockSpec((1, LANES), lambda bi, i: (0, 0)),
                  pl.BlockSpec((1, LANES), lambda bi, i: (0, 0))],
        out_specs=(spec, spec),
        compiler_params=_cparams(("arbitrary", "arbitrary")),
        name="rope_table",
    )(positions.reshape(b, s, 1), inv128, sign128)


def _rope_tile(x, cos, sin_signed, first_half):
    partner = jnp.where(first_half, pltpu.roll(x, LANES - HEAD_DIM // 2, 1),
                        pltpu.roll(x, HEAD_DIM // 2, 1))
    return x * cos + partner * sin_signed


def _ada_norm(x, g, sc, sh):
    y = x * lax.rsqrt(jnp.mean(x * x, axis=-1, keepdims=True) + EPS)
    return (y * g) * (1.0 + sc) + sh


VT_W = 256
INPROJ_TILE = 512
WI_ROWS = SUBLANES


def _inproj_kernel(h_ref, sh_ref, sc_ref, g_ref, w_ref, cos_ref, sin_ref,
                   qt_ref, qit_ref, wit_ref, k_ref, ki_ref, vt_ref, rest_ref):
    u = _ada_norm(h_ref[...], g_ref[...], sc_ref[...], sh_ref[...]).astype(BF16)
    cos = cos_ref[...]
    sin = sin_ref[...]
    lane = lax.broadcasted_iota(I32, cos.shape, 1)
    first_half = (lane % HEAD_DIM) < (HEAD_DIM // 2)
    rope = lambda x: _rope_tile(x, cos, sin, first_half)

    off = 0
    q_scale = HEAD_DIM ** -0.5
    for j in range(A_WIDTH // LANES):
        z = _mm(u, w_ref[:, off + j * LANES: off + (j + 1) * LANES])
        qt_ref[j * LANES:(j + 1) * LANES, :] = (rope(z) * q_scale).T.astype(BF16)
    off += A_WIDTH
    k_ref[...] = rope(_mm(u, w_ref[:, off:off + KV_WIDTH])).astype(BF16)
    off += KV_WIDTH
    vz = _mm(u, w_ref[:, off:off + KV_WIDTH])
    for j in range(vt_ref.shape[0]):
        vt_ref[j] = vz[j * VT_W:(j + 1) * VT_W].T.astype(BF16)
    off += KV_WIDTH
    for j in range(IDX_HEADS * IDX_DIM // LANES):
        z = _mm(u, w_ref[:, off + j * LANES: off + (j + 1) * LANES])
        qit_ref[j * LANES:(j + 1) * LANES, :] = rope(z).T.astype(BF16)
    off += IDX_HEADS * IDX_DIM
    z = _mm(u, w_ref[:, off:off + KIWI_W])
    ki_ref[...] = rope(z).astype(BF16)
    wi_scale = (IDX_HEADS * IDX_DIM) ** -0.5
    wit_ref[...] = (z * wi_scale).T[IDX_DIM:IDX_DIM + WI_ROWS, :]
    off += KIWI_W
    rest_ref[...] = _mm(u, w_ref[:, off:off + REST_W])


def _inproj_call(h, sh, sc, g, w_pack, cos, sin):
    b, s, d = h.shape
    tm = min(INPROJ_TILE, s)
    tok = lambda w: pl.BlockSpec((None, tm, w), lambda bi, i: (bi, i, 0))
    tr = lambda r: pl.BlockSpec((None, r, tm), lambda bi, i: (bi, 0, i))
    per_b = pl.BlockSpec((None, 1, d), lambda bi, i: (bi, 0, 0))
    sds = jax.ShapeDtypeStruct
    return pl.pallas_call(
        _inproj_kernel,
        out_shape=(sds((b, A_WIDTH, s), BF16), sds((b, IDX_HEADS * IDX_DIM, s), BF16),
                   sds((b, WI_ROWS, s), F32), sds((b, s, KV_WIDTH), BF16),
                   sds((b, s, KIWI_W), BF16), sds((b, s // VT_W, KV_WIDTH, VT_W), BF16),
                   sds((b, s, REST_W), F32)),
        grid=(b, s // tm),
        in_specs=[tok(d), per_b, per_b,
                  pl.BlockSpec((1, d), lambda bi, i: (0, 0)),
                  pl.BlockSpec((d, PACK_W), lambda bi, i: (0, 0)),
                  tok(LANES), tok(LANES)],
        out_specs=(tr(A_WIDTH), tr(IDX_HEADS * IDX_DIM), tr(WI_ROWS), tok(KV_WIDTH), tok(KIWI_W),
                   pl.BlockSpec((None, tm // VT_W, KV_WIDTH, VT_W), lambda bi, i: (bi, i, 0, 0)),
                   tok(REST_W)),
        compiler_params=_cparams(("arbitrary", "arbitrary")),
        name="inproj",
    )(h, sh, sc, g, w_pack, cos, sin)


CNT_ROWS = 32
CNT_ACCS = 4
SEARCH_HEAD_BITS = 16
ONES_ROWS = 16
TIE_BLOCK = 128


def _dsa_kernel(qt_ref, qit_ref, wit_ref, k_ref, ki_ref, vt_ref, lower_ref, o_ref,
                keys_ref, bias_ref, s_ref, acc_ref, *, kc, topk):
    qb = Q_BLOCK
    i = pl.program_id(1)
    t0 = i * qb
    nkc = (t0 + qb + kc - 1) // kc
    lane = lax.broadcasted_iota(I32, (1, qb), 1)
    limit = t0 + (lane // CHUNK + 1) * CHUNK
    key_off = lax.broadcasted_iota(I32, (kc, qb), 0)
    group = A_HEADS // A_KV_HEADS

    qit = qit_ref[...]
    qi_stack = jnp.concatenate(
        [qit[h * IDX_DIM:(h + 1) * IDX_DIM, :] for h in range(IDX_HEADS)], axis=1)
    wit = wit_ref[...]

    def score_body(c, carry):
        off = pl.multiple_of(c * kc, kc)
        ki = ki_ref[pl.ds(off, kc), 0:IDX_DIM]
        s = jnp.maximum(_mm(ki, qi_stack), 0.0)
        score = s[:, 0:qb] * wit[0:1, :]
        for h in range(1, IDX_HEADS):
            score = score + s[:, h * qb:(h + 1) * qb] * wit[h:h + 1, :]
        score = jnp.where(score == 0.0, 0.0, score)
        bits = lax.bitcast_convert_type(score, I32)
        key = jnp.where(bits < 0, bits ^ jnp.int32(0x7FFFFFFF), bits)
        keys_ref[pl.ds(off, kc), :] = jnp.where(off + key_off < limit, key, jnp.int32(INT_MIN))
        return carry

    lax.fori_loop(0, nkc, score_body, 0)

    def count(pred):
        def body(c, accs):
            off = pl.multiple_of(c * kc, kc)
            blk = keys_ref[pl.ds(off, kc), :]
            accs = list(accs)
            for j in range(kc // CNT_ROWS):
                a = accs[j % CNT_ACCS]
                accs[j % CNT_ACCS] = jnp.where(pred(blk[j * CNT_ROWS:(j + 1) * CNT_ROWS]), a + 1.0, a)
            return tuple(accs)
        accs = lax.fori_loop(0, nkc, body,
                             tuple(jnp.zeros((CNT_ROWS, qb), F32) for _ in range(CNT_ACCS)))
        return jnp.sum((accs[0] + accs[1]) + (accs[2] + accs[3]), axis=0, keepdims=True)

    topk_f = float(topk)
    cnt0 = count(lambda kk: kk >= 0)
    cnt1 = count(lambda kk: kk >= 1)
    nonneg = cnt0 >= topk_f
    thr0 = jnp.where(nonneg, 0, INT_MIN).astype(I32)
    cthr0 = jnp.where(nonneg, cnt0, 2.0 * kc * (nkc + 1).astype(F32))
    done0 = jnp.where((nonneg & (cnt1 < topk_f)) | (cthr0 == topk_f) | (limit <= topk), 1.0, 0.0)

    def try_bit(state, b):
        thr, cthr, done = state
        cand = thr + jnp.left_shift(jnp.int32(1), b)
        cnt = count(lambda kk: kk >= cand)
        take = (cnt >= topk_f) & (done < 0.5)
        thr = jnp.where(take, cand, thr)
        cthr = jnp.where(take, cnt, cthr)
        return thr, cthr, jnp.where(cthr == topk_f, 1.0, done)

    state = lax.fori_loop(0, SEARCH_HEAD_BITS, lambda j, st: try_bit(st, 30 - j),
                          (thr0, cthr0, done0))
    steps = 3
    tail_bits = 31 - SEARCH_HEAD_BITS

    def thr_cond(carry):
        g, st = carry
        return (g * steps < tail_bits) & (jnp.min(st[2]) < 0.5)

    def thr_body(carry):
        g, st = carry
        for jj in range(steps):
            st = try_bit(st, tail_bits - 1 - (g * steps + jj))
        return g + 1, st

    _, (thr, _, _) = lax.while_loop(thr_cond, thr_body, (jnp.int32(0), state))

    need = topk_f - count(lambda kk: kk > thr)
    lower = lower_ref[...]
    tie_off = lax.broadcasted_iota(I32, (TIE_BLOCK, qb), 0)

    def bias_body(c, seen):
        for j in range(kc // TIE_BLOCK):
            off = pl.multiple_of(c * kc + j * TIE_BLOCK, TIE_BLOCK)
            kk = keys_ref[pl.ds(off, TIE_BLOCK), :]
            tie = kk == thr
            tie_f = jnp.where(tie, 1.0, 0.0)
            rank = _mm(lower, tie_f.astype(BF16)) + seen
            sel = ((kk > thr) | (tie & (rank <= need))) & (off + tie_off < limit)
            bias_ref[pl.ds(off, TIE_BLOCK), :] = jnp.where(sel, 0.0, NEG_BIG)
            seen = seen + jnp.sum(tie_f, axis=0, keepdims=True)
        return seen

    lax.fori_loop(0, nkc, bias_body, jnp.zeros((1, qb), F32))

    qt = qt_ref[...]
    q_n = [jnp.concatenate([qt[(n * group + g) * HEAD_DIM:(n * group + g + 1) * HEAD_DIM, :]
                            for g in range(group)], axis=1) for n in range(A_KV_HEADS)]

    def max_body(c, parts):
        out = list(parts)
        for j in range(kc // VT_W):
            off = pl.multiple_of(c * kc + j * VT_W, VT_W)
            bias = bias_ref[pl.ds(off, VT_W), :]
            for n in range(A_KV_HEADS):
                s = _mm(k_ref[pl.ds(off, VT_W), n * HEAD_DIM:(n + 1) * HEAD_DIM], q_n[n])
                s = jnp.concatenate([s[:, g * qb:(g + 1) * qb] + bias for g in range(group)], axis=1)
                s_ref[c * (kc // VT_W) + j, n] = s
                out[n] = jnp.maximum(out[n], jnp.max(
                    s.reshape(VT_W // SUBLANES, SUBLANES, group * qb), axis=0))
        return tuple(out)

    parts = lax.fori_loop(
        0, nkc, max_body,
        tuple(jnp.full((SUBLANES, group * qb), NEG_BIG, F32) for _ in range(A_KV_HEADS)))
    m_n = [jnp.max(p, axis=0, keepdims=True) for p in parts]

    acc_ref[...] = jnp.zeros(acc_ref.shape, F32)
    ones = jnp.ones((ONES_ROWS, VT_W), BF16)

    def pv_body(c, carry):
        for j in range(kc // VT_W):
            cv = c * (kc // VT_W) + j
            for n in range(A_KV_HEADS):
                pt = jnp.exp(s_ref[cv, n] - m_n[n]).astype(BF16)
                vt = jnp.concatenate([vt_ref[cv, n * HEAD_DIM:(n + 1) * HEAD_DIM, :], ones], axis=0)
                acc_ref[n] += _mm(vt, pt)
        return carry

    lax.fori_loop(0, nkc, pv_body, 0)
    for n in range(A_KV_HEADS):
        for g in range(group):
            hh = n * group + g
            cols = slice(g * qb, (g + 1) * qb)
            o_ref[hh * HEAD_DIM:(hh + 1) * HEAD_DIM, :] = (
                acc_ref[n, 0:HEAD_DIM, cols] / acc_ref[n, HEAD_DIM:HEAD_DIM + 1, cols])


def _dsa_call(qt, qit, wit, k, ki, vt, kc=512):
    b, _, s = qt.shape
    kc = min(kc, s)
    assert kc % VT_W == 0 and s % kc == 0
    topk = min(IDX_TOPK_MAX, s // 4)
    group = A_HEADS // A_KV_HEADS
    lower = jnp.asarray(np.tril(np.ones((TIE_BLOCK, TIE_BLOCK), np.float32)), BF16)
    qblk = lambda r: pl.BlockSpec((None, r, Q_BLOCK), lambda bi, i: (bi, 0, i))
    seq = lambda w: pl.BlockSpec((None, s, w), lambda bi, i: (bi, 0, 0))
    return pl.pallas_call(
        functools.partial(_dsa_kernel, kc=kc, topk=topk),
        out_shape=jax.ShapeDtypeStruct((b, A_WIDTH, s), F32),
        grid=(b, s // Q_BLOCK),
        in_specs=[qblk(A_WIDTH), qblk(IDX_HEADS * IDX_DIM), qblk(WI_ROWS),
                  seq(KV_WIDTH), seq(KIWI_W),
                  pl.BlockSpec((None, s // VT_W, KV_WIDTH, VT_W), lambda bi, i: (bi, 0, 0, 0)),
                  pl.BlockSpec((TIE_BLOCK, TIE_BLOCK), lambda bi, i: (0, 0))],
        out_specs=qblk(A_WIDTH),
        scratch_shapes=[pltpu.VMEM((s, Q_BLOCK), I32),
                        pltpu.VMEM((s, Q_BLOCK), F32),
                        pltpu.VMEM((s // VT_W, A_KV_HEADS, VT_W, group * Q_BLOCK), F32),
                        pltpu.VMEM((A_KV_HEADS, HEAD_DIM + ONES_ROWS, group * Q_BLOCK), F32)],
        compiler_params=_cparams(("arbitrary", "arbitrary")),
        name="dsa",
    )(qt, qit, wit, k, ki, vt, lower)


POOL_HALO = 32


def _pool_kernel(u_ref, w_ref, scale_ref, o_ref, x_buf, a_buf, b_buf, *, tm):
    hl = POOL_HALO
    rows = tm + hl
    first = pl.program_id(1) == 0

    @pl.when(first)
    def _():
        x_buf[0:hl, :] = jnp.zeros((hl, B_WIDTH), F32)

    x = u_ref[...]
    x_buf[hl:rows, :] = x
    a_buf[8:rows, :] = x_buf[8:rows, :] + x_buf[7:rows - 1, :]
    b_buf[16:rows, :] = a_buf[16:rows, :] + a_buf[14:rows - 2, :]
    w2 = a_buf[hl:rows, :]
    w4 = b_buf[hl:rows, :]
    a_buf[24:rows, :] = b_buf[24:rows, :] + b_buf[20:rows - 4, :]
    w8 = a_buf[hl:rows, :]
    b_buf[hl:rows, :] = a_buf[hl:rows, :] + a_buf[hl - 8:rows - 8, :]
    w16 = b_buf[hl:rows, :]
    x_buf[0:hl, :] = x_buf[tm:rows, :]

    lane = lax.broadcasted_iota(I32, (tm, B_WIDTH), 1)
    grp = lane // POOL_GROUP
    wsum = jnp.where(grp == 0, w2, jnp.where(grp == 1, w4, jnp.where(grp == 2, w8, w16)))
    win = jnp.where(grp == 0, 2, jnp.where(grp == 1, 4, jnp.where(grp == 2, 8, 16)))
    t = pl.program_id(1) * tm + lax.broadcasted_iota(I32, (tm, B_WIDTH), 0)
    cnt = jnp.minimum(t + 1, win).astype(F32)
    pooled = wsum / cnt - x
    y = _mm(pooled.astype(BF16), w_ref[...])
    o_ref[...] = y * scale_ref[...]


def _pool_call(rest, w_bd, scale, tm=512):
    b, s, _ = rest.shape
    tm = min(tm, s)
    rows = tm + POOL_HALO
    return pl.pallas_call(
        functools.partial(_pool_kernel, tm=tm),
        out_shape=jax.ShapeDtypeStruct((b, s, B_WIDTH), F32),
        grid=(b, s // tm),
        in_specs=[pl.BlockSpec((None, tm, B_WIDTH), lambda bi, i: (bi, i, 0)),
                  pl.BlockSpec((B_WIDTH, B_WIDTH), lambda bi, i: (0, 0)),
                  pl.BlockSpec((1, B_WIDTH), lambda bi, i: (0, 0))],
        out_specs=pl.BlockSpec((None, tm, B_WIDTH), lambda bi, i: (bi, i, 0)),
        scratch_shapes=[pltpu.VMEM((rows, B_WIDTH), F32)] * 3,
        compiler_params=_cparams(("arbitrary", "arbitrary")),
        name="pool",
    )(rest, w_bd, scale)


def _hgrn_consts():
    tril = np.tril(np.ones((CHUNK, CHUNK), np.float32))
    mats = [tril]
    r = np.arange(CHUNK)
    for h in HGRN_LEVELS:
        mats.append(tril[(r // (2 * h)) * (2 * h) + h - 1])
    return np.concatenate(mats, axis=0)


def _split3(x):
    hi = x.astype(BF16)
    r1 = x - hi.astype(F32)
    mid = r1.astype(BF16)
    lo = (r1 - mid.astype(F32)).astype(BF16)
    return hi, mid, lo


def _hgrn_kernel(q_ref, f_ref, i_ref, g_ref, lb_ref, ng_ref, cm_ref, bd_ref, o_ref, state_ref,
                 *, layer, tm):
    @pl.when(pl.program_id(1) == 0)
    def _():
        state_ref[...] = jnp.zeros(state_ref.shape, F32)

    lbl = lb_ref[...]
    e = jnp.exp(lbl - jnp.max(lbl, axis=0, keepdims=True))
    p = e / jnp.sum(e, axis=0, keepdims=True)
    cum = p[0:1]
    for l in range(1, layer + 1):
        cum = cum + p[l:l + 1]
    lb = jnp.clip(cum - p[0:1], 0.0, 1.0)

    cm = cm_ref[...]
    ng = ng_ref[...]
    bd = bd_ref[...]
    bd_f = bd.astype(F32)
    row = lax.broadcasted_iota(I32, (CHUNK, 1), 0)
    tt = lax.broadcasted_iota(I32, (CHUNK, C_WIDTH), 0)
    ss = lax.broadcasted_iota(I32, (CHUNK, C_WIDTH), 1) % CHUNK
    lvl_mask = [tt == ss] + [(tt // (2 * h)) == (ss // (2 * h)) for h in HGRN_LEVELS]
    w = C_WIDTH
    heads = C_WIDTH // C_KDIM

    def expand(x16):
        return jnp.concatenate([x16] * heads, axis=0) * bd

    chunks = range(tm // CHUNK)
    rows = [slice(ci * CHUNK, (ci + 1) * CHUNK) for ci in chunks]
    z = f_ref[...]
    log_f = jnp.log(lb + (1.0 - lb) * jax.nn.sigmoid(z))
    kin = (1.0 - lb) * jax.nn.sigmoid(-z)
    qx = q_ref[...]
    qv = qx * jax.nn.sigmoid(qx)
    vb16 = i_ref[...].astype(BF16)
    hi, mid, lo = _split3(log_f)
    lf3 = jnp.concatenate([hi, mid, lo], axis=1)
    cs = [_mm(cm, lf3[r]) for r in rows]
    cs = [c[:, 0:w] + c[:, w:2 * w] + c[:, 2 * w:3 * w] for c in cs]
    odd = [((row // h) % 2) == 1 for h in HGRN_LEVELS]
    attn, q_dec, upd, s_dec = [], [], [], []
    for ci in chunks:
        r = rows[ci]
        bcum = cs[ci][0:CHUNK]
        b_last = bcum[CHUNK - 1:CHUNK]
        q_c, k_c = qv[r], kin[r]
        q_dec.append((q_c * jnp.exp(bcum)).astype(BF16))
        s_dec.append(jnp.exp(b_last))
        upd.append(_tn(vb16[r], (k_c * jnp.exp(b_last - bcum)).astype(BF16)) * bd_f)
        qs, ks = [q_c.astype(BF16)], [k_c.astype(BF16)]
        for li in range(len(HGRN_LEVELS)):
            ref = cs[ci][(li + 1) * CHUNK:(li + 2) * CHUNK]
            qs.append((q_c * jnp.exp(jnp.where(odd[li], bcum - ref, NEG_BIG))).astype(BF16))
            ks.append((k_c * jnp.exp(jnp.where(odd[li], NEG_BIG, ref - bcum))).astype(BF16))
        a = jnp.zeros((CHUNK, w), F32)
        for mask, ql, kl in zip(lvl_mask, qs, ks):
            a = a + jnp.where(mask, _nt(ql, expand(kl)), 0.0)
        attn.append(a)
    o_intra = [_mm(attn[ci].astype(BF16), expand(vb16[rows[ci]])) for ci in chunks]
    st = state_ref[...]
    outs = []
    for ci in chunks:
        outs.append(_nt(q_dec[ci], st.astype(BF16)) + o_intra[ci])
        st = st * s_dec[ci] + upd[ci]
    state_ref[...] = st
    o = jnp.concatenate(outs, axis=0)
    o2h, o2m, o2l = _split3(o * o)
    ms = (_mm(o2h, bd) + _mm(o2m, bd) + _mm(o2l, bd)) * (1.0 / C_VDIM)
    gx = g_ref[...]
    o_ref[...] = (o * lax.rsqrt(ms + EPS) * ng) * (gx * jax.nn.sigmoid(gx))


def _hgrn_call(rest, lb_logits, norm_g, layer, tm=256):
    b, s, _ = rest.shape
    depth = lb_logits.shape[0]
    cm = jnp.asarray(_hgrn_consts(), BF16)
    head_of = np.arange(C_WIDTH) // C_KDIM
    bd = jnp.asarray(head_of[:, None] == head_of[None, :], BF16)
    col = lambda j: pl.BlockSpec((None, tm, C_WIDTH), lambda bi, i, j=j: (bi, i, j))
    return pl.pallas_call(
        functools.partial(_hgrn_kernel, layer=layer, tm=tm),
        out_shape=jax.ShapeDtypeStruct((b, s, C_WIDTH), F32),
        grid=(b, s // tm),
        in_specs=[col(1), col(2), col(3), col(4),
                  pl.BlockSpec((depth, C_WIDTH), lambda bi, i: (0, 0)),
                  pl.BlockSpec((1, C_WIDTH), lambda bi, i: (0, 0)),
                  pl.BlockSpec(cm.shape, lambda bi, i: (0, 0)),
                  pl.BlockSpec((C_WIDTH, C_WIDTH), lambda bi, i: (0, 0))],
        out_specs=pl.BlockSpec((None, tm, C_WIDTH), lambda bi, i: (bi, i, 0)),
        scratch_shapes=[pltpu.VMEM((C_WIDTH, C_WIDTH), F32)],
        compiler_params=_cparams(("arbitrary", "arbitrary")),
        name="hgrn2",
    )(rest, rest, rest, rest, lb_logits, jnp.tile(norm_g, C_HEADS).reshape(1, C_WIDTH), cm, bd)


RANK_BITS = 20
CODE_ROWS = 8


def _outproj_router_kernel(ya_ref, yb_ref, yc_ref, h_ref, g1_ref, wo_ref, sh_ref, sc_ref, g_ref,
                           wr_ref, br_ref, tri_ref, h1_ref, u_ref, code_ref, gk_ref, cnt_ref,
                           run_ref):
    @pl.when((pl.program_id(0) == 0) & (pl.program_id(1) == 0))
    def _():
        run_ref[...] = jnp.zeros(run_ref.shape, F32)

    y = _tn(ya_ref[...].astype(BF16), wo_ref[0:A_WIDTH, :])
    y = y + _mm(yb_ref[...].astype(BF16), wo_ref[A_WIDTH:A_WIDTH + B_WIDTH, :])
    y = y + _mm(yc_ref[...].astype(BF16), wo_ref[A_WIDTH + B_WIDTH:, :])
    h1 = h_ref[...] + g1_ref[...] * y
    h1_ref[...] = h1
    u = _ada_norm(h1, g_ref[...], sc_ref[...], sh_ref[...])
    u_ref[...] = u

    u_hi = u.astype(BF16)
    u_lo = (u - u_hi.astype(F32)).astype(BF16)
    wr = wr_ref[...]
    wr_hi = wr.astype(BF16)
    wr_lo = (wr - wr_hi.astype(F32)).astype(BF16)
    logits = _mm(u_hi, wr_hi) + (_mm(u_hi, wr_lo) + _mm(u_lo, wr_hi)) + br_ref[...]
    lane = lax.broadcasted_iota(I32, logits.shape, 1).astype(F32)
    work = jnp.where(lane < N_EXPERTS, logits, -jnp.inf)
    picks, firsts, tops = [], [], []
    for k in range(TOP_K):
        m = jnp.max(work, axis=1, keepdims=True)
        first = jnp.min(jnp.where(work == m, lane, float(LANES)), axis=1, keepdims=True)
        pick = lane == first
        work = jnp.where(pick, -jnp.inf, work)
        picks.append(pick)
        firsts.append(first)
        tops.append(m)
    ex = [jnp.exp(m - tops[0]) for m in tops]
    den = ex[0] + ex[1] + ex[2] + ex[3]

    sel = picks[0] | picks[1] | picks[2] | picks[3]
    sel_f = jnp.where(sel, 1.0, 0.0)
    prefix = _mm(tri_ref[...], sel_f.astype(BF16)) + run_ref[...]
    run_ref[...] += jnp.sum(sel_f, axis=0, keepdims=True)
    cnt_ref[...] = run_ref[...]

    code = jnp.zeros(logits.shape, I32)
    gk = jnp.zeros(logits.shape, F32)
    for k in range(TOP_K):
        rank = jnp.sum(jnp.where(picks[k], prefix, 0.0), axis=1, keepdims=True)
        ck = (firsts[k].astype(I32) << RANK_BITS) | rank.astype(I32)
        code = jnp.where(lane == float(k), ck, code)
        gk = jnp.where(lane == float(k), ex[k] / den, gk)
    code_ref[...] = code.T[0:CODE_ROWS, :]
    gk_ref[...] = gk


def _outproj_router_call(ya, yb, yc, h, g1, w_out, sh, sc, g, w_router, b_router, tm=512):
    b, s, d = h.shape
    tm = min(tm, s)
    tri = jnp.asarray(np.tril(np.ones((tm, tm), np.float32), -1), BF16)
    tok = lambda w: pl.BlockSpec((None, tm, w), lambda bi, i: (bi, i, 0))
    per_b = pl.BlockSpec((None, 1, d), lambda bi, i: (bi, 0, 0))
    full = lambda a: pl.BlockSpec(a.shape, lambda bi, i: (0,) * a.ndim)
    return pl.pallas_call(
        _outproj_router_kernel,
        out_shape=(jax.ShapeDtypeStruct((b, s, d), F32), jax.ShapeDtypeStruct((b, s, d), F32),
                   jax.ShapeDtypeStruct((CODE_ROWS, b * s), I32),
                   jax.ShapeDtypeStruct((b, s, LANES), F32), jax.ShapeDtypeStruct((1, LANES), F32)),
        grid=(b, s // tm),
        in_specs=[pl.BlockSpec((None, A_WIDTH, tm), lambda bi, i: (bi, 0, i)),
                  tok(B_WIDTH), tok(C_WIDTH), tok(d), per_b, full(w_out),
                  per_b, per_b, full(g), full(w_router), full(b_router), full(tri)],
        out_specs=(tok(d), tok(d),
                   pl.BlockSpec((CODE_ROWS, tm), lambda bi, i: (0, bi * (s // tm) + i)),
                   tok(LANES), pl.BlockSpec((1, LANES), lambda bi, i: (0, 0))),
        scratch_shapes=[pltpu.VMEM((1, LANES), F32)],
        compiler_params=_cparams(("arbitrary", "arbitrary")),
        name="outproj_router",
    )(ya, yb, yc, h, g1, w_out, sh, sc, g, w_router, b_router, tri)


MOE_BLOCK = 512
PERM_W = 2 * LANES


ROWS_TILE = 2048


def _rows_kernel(start_ref, code_ref, row_ref):
    code = code_ref[...]
    expert = code >> RANK_BITS
    base = jnp.zeros(code.shape, I32)
    for e in range(N_EXPERTS):
        base = jnp.where(expert == e, start_ref[e], base)
    row_ref[...] = base + (code & ((1 << RANK_BITS) - 1))


def _rows_call(starts, code):
    r, n = code.shape
    tile = min(ROWS_TILE, n)
    return pl.pallas_call(
        _rows_kernel,
        out_shape=jax.ShapeDtypeStruct((r, n), I32),
        grid_spec=pltpu.PrefetchScalarGridSpec(
            num_scalar_prefetch=1, grid=(n // tile,),
            in_specs=[pl.BlockSpec((r, tile), lambda i, *_: (0, i))],
            out_specs=pl.BlockSpec((r, tile), lambda i, *_: (0, i))),
        compiler_params=_cparams(("arbitrary",)),
        name="moe_rows",
    )(starts, code)


def _drain_rows(src_row, dst_row, sem, n):
    def body(t, c):
        pltpu.make_async_copy(src_row, dst_row, sem).wait()
        return c
    lax.fori_loop(0, n, body, 0, unroll=4)


ZERO_ROWS = MOE_BLOCK // 2


def _dispatch_kernel(row_ref, padlo_ref, padhi_ref, x_ref, xs_ref, zbuf, sem, zsem, *, tm):
    tok0 = pl.program_id(0) * tm
    n_tok = row_ref.shape[0] // TOP_K

    @pl.when(pl.program_id(0) == 0)
    def _():
        zbuf[...] = jnp.zeros(zbuf.shape, F32)

        def pad_copies(e, fn):
            lo = padlo_ref[e]
            hi = padhi_ref[e]
            lo8 = jnp.minimum((lo + SUBLANES - 1) // SUBLANES * SUBLANES, hi)
            for j in range(SUBLANES - 1):

                @pl.when(lo + j < lo8)
                def _(j=j):
                    fn(pltpu.make_async_copy(zbuf.at[pl.ds(0, 1)], xs_ref.at[pl.ds(lo + j, 1)], zsem))
            n8 = hi - lo8
            size = ZERO_ROWS
            while size >= SUBLANES:
                off = pl.multiple_of(lo8 + (n8 & ~(2 * size - 1)), SUBLANES)

                @pl.when((n8 & size) != 0)
                def _(size=size, off=off):
                    fn(pltpu.make_async_copy(zbuf.at[pl.ds(0, size)], xs_ref.at[pl.ds(off, size)],
                                             zsem))
                size //= 2

        def tail_copies(fn):
            tail_lo = padhi_ref[N_EXPERTS - 1]

            def piece(j, c):
                off = pl.multiple_of(tail_lo + j * ZERO_ROWS, ZERO_ROWS)

                @pl.when(off < xs_ref.shape[0])
                def _():
                    fn(pltpu.make_async_copy(zbuf, xs_ref.at[pl.ds(off, ZERO_ROWS)], zsem))
                return c

            lax.fori_loop(0, N_EXPERTS * MOE_BLOCK // ZERO_ROWS, piece, 0)

        def start_e(e, c):
            pad_copies(e, lambda cp: cp.start())
            return c

        def wait_e(e, c):
            pad_copies(e, lambda cp: cp.wait())
            return c

        lax.fori_loop(0, N_EXPERTS, start_e, 0)
        tail_copies(lambda cp: cp.start())
        lax.fori_loop(0, N_EXPERTS, wait_e, 0)
        tail_copies(lambda cp: cp.wait())

    def issue(t8, c):
        base = pl.multiple_of(t8 * SUBLANES, SUBLANES)
        for j in range(SUBLANES):
            for k in range(TOP_K):
                dst = row_ref[k * n_tok + tok0 + base + j]
                pltpu.make_async_copy(x_ref.at[pl.ds(base + j, 1)], xs_ref.at[pl.ds(dst, 1)],
                                      sem).start(priority=k % 2)
        return c

    lax.fori_loop(0, tm // SUBLANES, issue, 0)
    _drain_rows(x_ref.at[pl.ds(0, 1)], xs_ref.at[pl.ds(0, 1)], sem, tm * TOP_K)


def _dispatch_call(row_ids, pad_lo, pad_hi, x, rows, tm=512):
    n, d = x.shape
    return pl.pallas_call(
        functools.partial(_dispatch_kernel, tm=tm),
        out_shape=jax.ShapeDtypeStruct((rows, d), F32),
        grid_spec=pltpu.PrefetchScalarGridSpec(
            num_scalar_prefetch=3, grid=(n // tm,),
            in_specs=[pl.BlockSpec((tm, d), lambda i, *_: (i, 0))],
            out_specs=pl.BlockSpec(memory_space=pl.ANY),
            scratch_shapes=[pltpu.VMEM((ZERO_ROWS, d), F32), pltpu.SemaphoreType.DMA,
                            pltpu.SemaphoreType.DMA]),
        compiler_params=_cparams(("arbitrary",)),
        name="moe_dispatch",
    )(row_ids, pad_lo, pad_hi, x)


def _ffn_kernel(blk_e_ref, nb_ref, xs_ref, w1_ref, b1g_ref, b1l_ref, w2_ref, b2_ref, perm_ref,
                ys_ref, w1g_s, w1l_s, w2_s):
    i = pl.program_id(0)
    e = blk_e_ref[i]
    live = i < nb_ref[0]
    fresh = (i == 0) | (e != blk_e_ref[jnp.maximum(i - 1, 0)])

    @pl.when(live & fresh)
    def _():
        perm = perm_ref[...]
        for j in range(w1_ref.shape[1] // PERM_W):
            t = _mm(w1_ref[:, j * PERM_W:(j + 1) * PERM_W].astype(BF16), perm).astype(BF16)
            w1g_s[:, j * LANES:(j + 1) * LANES] = t[:, 0:LANES]
            w1l_s[:, j * LANES:(j + 1) * LANES] = t[:, LANES:PERM_W]
        w2_s[...] = w2_ref[...].astype(BF16)

    @pl.when(live)
    def _():
        x = xs_ref[...].astype(BF16)
        glu = jnp.minimum(_mm(x, w1g_s[...]) + b1g_ref[...], SWIGLU_LIMIT)
        lin = jnp.clip(_mm(x, w1l_s[...]) + b1l_ref[...], -SWIGLU_LIMIT, SWIGLU_LIMIT)
        act = glu * jax.nn.sigmoid(SWIGLU_ALPHA * glu) * (lin + 1.0)
        ys_ref[...] = _mm(act.astype(BF16), w2_s[...]) + b2_ref[...]

    @pl.when(jnp.logical_not(live))
    def _():
        ys_ref[...] = jnp.zeros(ys_ref.shape, F32)


def _ffn_call(blk_e, nb_used, xs, w1, b1g, b1l, w2, b2, layer):
    rows, d = xs.shape
    _, ne, _, ff2 = w1.shape
    ff = ff2 // 2
    perm = np.zeros((PERM_W, PERM_W), np.float32)
    perm[2 * np.arange(LANES), np.arange(LANES)] = 1.0
    perm[2 * np.arange(LANES) + 1, LANES + np.arange(LANES)] = 1.0
    ex = lambda r, c: pl.BlockSpec((None, r, c), lambda i, be, nb: (be[i], 0, 0))
    exl = lambda r, c: pl.BlockSpec((None, None, r, c), lambda i, be, nb: (layer, be[i], 0, 0))
    return pl.pallas_call(
        _ffn_kernel,
        out_shape=jax.ShapeDtypeStruct((rows, d), F32),
        grid_spec=pltpu.PrefetchScalarGridSpec(
            num_scalar_prefetch=2, grid=(rows // MOE_BLOCK,),
            in_specs=[pl.BlockSpec((MOE_BLOCK, d),
                                   lambda i, be, nb: (jnp.minimum(i, jnp.maximum(nb[0] - 1, 0)), 0)),
                      exl(d, ff2), ex(1, ff), ex(1, ff), exl(ff, d), ex(1, d),
                      pl.BlockSpec((PERM_W, PERM_W), lambda i, *_: (0, 0))],
            out_specs=pl.BlockSpec((MOE_BLOCK, d), lambda i, *_: (i, 0)),
            scratch_shapes=[pltpu.VMEM((d, ff), BF16), pltpu.VMEM((d, ff), BF16),
                            pltpu.VMEM((ff, d), BF16)]),
        compiler_params=_cparams(("arbitrary",)),
        name="moe_ffn",
    )(blk_e, nb_used, xs, w1, b1g, b1l, w2, b2, jnp.asarray(perm, BF16))


def _combine_kernel(row_ref, gk_ref, h_ref, g2_ref, fg_ref, ys_ref, o_ref, buf, sem,
                    *, tm, final_norm):
    tok0 = pl.program_id(0) * tm
    n_tok = row_ref.shape[0] // TOP_K

    def issue(t8, c):
        base = pl.multiple_of(t8 * SUBLANES, SUBLANES)
        for j in range(SUBLANES):
            for k in range(TOP_K):
                src = row_ref[k * n_tok + tok0 + base + j]
                pltpu.make_async_copy(ys_ref.at[pl.ds(src, 1)], buf.at[k, pl.ds(base + j, 1)],
                                      sem).start(priority=k % 2)
        return c

    lax.fori_loop(0, tm // SUBLANES, issue, 0)
    _drain_rows(ys_ref.at[pl.ds(0, 1)], buf.at[0, pl.ds(0, 1)], sem, tm * TOP_K)

    gk = gk_ref[...]
    acc = buf[0] * gk[:, 0:1]
    for k in range(1, TOP_K):
        acc = acc + buf[k] * gk[:, k:k + 1]
    out = h_ref[...] + g2_ref[...] * acc
    if final_norm:
        out = out * lax.rsqrt(jnp.mean(out * out, axis=-1, keepdims=True) + EPS) * fg_ref[...]
    o_ref[...] = out


def _combine_call(row_ids, gk, h1, g2, final_g, ys, seq, final_norm, tm=256):
    n, d = h1.shape
    return pl.pallas_call(
        functools.partial(_combine_kernel, tm=tm, final_norm=final_norm),
        out_shape=jax.ShapeDtypeStruct((n, d), F32),
        grid_spec=pltpu.PrefetchScalarGridSpec(
            num_scalar_prefetch=1, grid=(n // tm,),
            in_specs=[pl.BlockSpec((tm, LANES), lambda i, *_: (i, 0)),
                      pl.BlockSpec((tm, d), lambda i, *_: (i, 0)),
                      pl.BlockSpec((None, 1, d), lambda i, *_: ((i * tm) // seq, 0, 0)),
                      pl.BlockSpec((1, d), lambda i, *_: (0, 0)),
                      pl.BlockSpec(memory_space=pl.ANY)],
            out_specs=pl.BlockSpec((tm, d), lambda i, *_: (i, 0)),
            scratch_shapes=[pltpu.VMEM((TOP_K, tm, d), F32), pltpu.SemaphoreType.DMA]),
        compiler_params=_cparams(("arbitrary",)),
        name="moe_combine",
    )(row_ids, gk, h1, g2, final_g, ys)


def _moe_call(u, code, gk, counts, h1, g2, w1, b1, w2, b2, final_g, layer, final_norm):
    b, s, d = h1.shape
    n = b * s
    ne = w1.shape[1]
    rows = n * TOP_K + ne * MOE_BLOCK
    cnt = counts[0, :ne].astype(I32)
    padded = (cnt + MOE_BLOCK - 1) // MOE_BLOCK * MOE_BLOCK
    ends = jnp.cumsum(padded)
    starts = (ends - padded).astype(I32)
    blk_start = jnp.arange(rows // MOE_BLOCK, dtype=I32) * MOE_BLOCK
    blk_e = jnp.minimum(jnp.sum((ends[None, :] <= blk_start[:, None]).astype(I32), axis=1), ne - 1)
    nb_used = (ends[-1:] // MOE_BLOCK).astype(I32)

    row_ids = _rows_call(starts, code)[0:TOP_K].reshape(TOP_K * n)

    xs = _dispatch_call(row_ids, starts + cnt, ends.astype(I32), u.reshape(n, d), rows)
    ff = w1.shape[3] // 2
    ys = _ffn_call(blk_e, nb_used, xs, w1, b1[:, 0::2].reshape(ne, 1, ff),
                   b1[:, 1::2].reshape(ne, 1, ff), w2, b2.reshape(ne, 1, d), layer)
    out = _combine_call(row_ids, gk.reshape(n, LANES), h1.reshape(n, d), g2, final_g, ys, s,
                        final_norm)
    return out.reshape(b, s, d)


def _pack_w_in(w_in):
    d = w_in.shape[0]
    offs = np.cumsum((0,) + IN_SPLITS)
    head = w_in[:, :offs[5]]
    wi = w_in[:, offs[5]:offs[6]]
    pad = jnp.zeros((d, KIWI_W - IDX_DIM - IDX_HEADS), w_in.dtype)
    return jnp.concatenate([head, wi, pad, w_in[:, offs[6]:]], axis=1).astype(BF16)


def _block_diag(pool_w):
    g, c, _ = pool_w.shape
    out = jnp.zeros((g * c, g * c), pool_w.dtype)
    for j in range(g):
        out = out.at[j * c:(j + 1) * c, j * c:(j + 1) * c].set(pool_w[j])
    return out.astype(BF16)


def kernel(x, c, positions, w_ada, b_ada, norm1_g, norm2_g, w_in, w_out, pool_w, pool_scale,
           hgrn_norm_g, lb_logits, w_router, b_router, w1, b1, w2, b2, final_g):
    bsz, s, d = x.shape
    depth = w_ada.shape[0]
    mod = _ada_call(c, w_ada, b_ada)
    cos, sin = _rope_table_call(positions)
    h = x
    for l in range(depth):
        sh1, sc1, g1, sh2, sc2, g2 = [mod[l, :, j * d:(j + 1) * d].reshape(bsz, 1, d)
                                      for j in range(6)]
        qt, qit, wit, k, ki, vt, rest = _inproj_call(
            h, sh1, sc1, norm1_g[l].reshape(1, d), _pack_w_in(w_in[l]), cos, sin)
        ya = _dsa_call(qt, qit, wit, k, ki, vt)
        yb = _pool_call(rest, _block_diag(pool_w[l]), pool_scale[l].reshape(1, B_WIDTH))
        yc = _hgrn_call(rest, lb_logits, hgrn_norm_g[l], l)
        wr = jnp.pad(w_router[l], ((0, 0), (0, LANES - N_EXPERTS)))
        br = jnp.pad(b_router[l], (0, LANES - N_EXPERTS)).reshape(1, LANES)
        h1, u2, code, gk, counts = _outproj_router_call(
            ya, yb, yc, h, g1, w_out[l].astype(BF16), sh2, sc2, norm2_g[l].reshape(1, d), wr, br)
        h = _moe_call(u2, code, gk, counts, h1, g2, w1, b1[l], w2, b2[l],
                      final_g.reshape(1, d), l, final_norm=(l == depth - 1))
    return h
```

```python
import functools
import math

import numpy as np
import jax
import jax.numpy as jnp
from jax import lax
from jax.experimental import pallas as pl
from jax.experimental.pallas import tpu as pltpu

F32 = jnp.float32
BF16 = jnp.bfloat16
I32 = jnp.int32

CHUNK = 64
EPS = 1e-6
NEG_BIG = -1e30
A_HEADS, A_KV_HEADS, HEAD_DIM = 8, 2, 64
IDX_HEADS, IDX_DIM, IDX_TOPK_MAX = 4, 64, 256
Q_BLOCK = 128
ROPE_THETA = 10000.0
A_WIDTH = A_HEADS * HEAD_DIM
POOL_WINDOWS = (2, 4, 8, 16)
POOL_GROUP = 64
B_WIDTH = len(POOL_WINDOWS) * POOL_GROUP
C_HEADS, C_KDIM, C_VDIM = 4, 64, 64
C_WIDTH = C_HEADS * C_VDIM
N_EXPERTS, TOP_K = 32, 4
SWIGLU_LIMIT, SWIGLU_ALPHA = 7.0, 1.702
KV_WIDTH = A_KV_HEADS * HEAD_DIM
IN_SPLITS = (A_WIDTH, KV_WIDTH, KV_WIDTH, IDX_HEADS * IDX_DIM, IDX_DIM, IDX_HEADS,
             B_WIDTH, C_HEADS * C_KDIM, C_HEADS * C_KDIM, C_WIDTH, C_WIDTH)

LANES = 128
SUBLANES = 8
INT_MIN = -(2 ** 31)
VMEM_LIMIT = 56 * 1024 * 1024

KIWI_W = LANES
REST_W = B_WIDTH + 4 * C_WIDTH
PACK_W = A_WIDTH + 2 * KV_WIDTH + IDX_HEADS * IDX_DIM + KIWI_W + REST_W
HGRN_LEVELS = (32, 16, 8, 4, 2, 1)


def _nt(a, b):
    return lax.dot_general(a, b, (((1,), (1,)), ((), ())), preferred_element_type=F32)


def _tn(a, b):
    return lax.dot_general(a, b, (((0,), (0,)), ((), ())), preferred_element_type=F32)


def _mm(a, b):
    return jnp.dot(a, b, preferred_element_type=F32)


def _cparams(sem):
    return pltpu.CompilerParams(dimension_semantics=sem, vmem_limit_bytes=VMEM_LIMIT)


def _ada_kernel(c_ref, w_ref, b_ref, o_ref):
    c = c_ref[...]
    ca = c * jax.nn.sigmoid(c)
    o_ref[...] = jnp.dot(ca, w_ref[...], precision=lax.Precision.HIGHEST,
                         preferred_element_type=F32) + b_ref[...]


def _ada_call(c, w_ada, b_ada):
    depth, d, d6 = w_ada.shape
    b = c.shape[0]
    nblk = d6 // d
    return pl.pallas_call(
        _ada_kernel,
        out_shape=jax.ShapeDtypeStruct((depth, b, d6), F32),
        grid=(depth, nblk),
        in_specs=[pl.BlockSpec((b, d), lambda l, j: (0, 0)),
                  pl.BlockSpec((None, d, d), lambda l, j: (l, 0, j)),
                  pl.BlockSpec((None, 1, d), lambda l, j: (l, 0, j))],
        out_specs=pl.BlockSpec((None, b, d), lambda l, j: (l, 0, j)),
        compiler_params=_cparams(("arbitrary", "arbitrary")),
        name="ada_mod",
    )(c, w_ada, b_ada.reshape(depth, 1, d6))


def _rope_table_kernel(pos_ref, inv_ref, sign_ref, cos_ref, sin_ref):
    ang = pos_ref[...].astype(F32) * inv_ref[...]
    cos_ref[...] = jnp.cos(ang)
    sin_ref[...] = jnp.sin(ang) * sign_ref[...]


def _rope_table_call(positions, ts=512):
    b, s = positions.shape
    half = HEAD_DIM // 2
    inv = jnp.power(jnp.float32(ROPE_THETA), -jnp.arange(0, HEAD_DIM, 2, dtype=F32) / HEAD_DIM)
    inv128 = jnp.tile(inv, LANES // half).reshape(1, LANES)
    sign128 = jnp.tile(jnp.concatenate([-jnp.ones((half,), F32), jnp.ones((half,), F32)]),
                       LANES // HEAD_DIM).reshape(1, LANES)
    spec = pl.BlockSpec((None, ts, LANES), lambda bi, i: (bi, i, 0))
    return pl.pallas_call(
        _rope_table_kernel,
        out_shape=(jax.ShapeDtypeStruct((b, s, LANES), F32),) * 2,
        grid=(b, s // ts),
        in_specs=[pl.BlockSpec((None, ts, 1), lambda bi, i: (bi, i, 0)),
                  pl.BlockSpec((1, LANES), lambda bi, i: (0, 0)),
                  pl.BlockSpec((1, LANES), lambda bi, i: (0, 0))],
        out_specs=(spec, spec),
        compiler_params=_cparams(("arbitrary", "arbitrary")),
        name="rope_table",
    )(positions.reshape(b, s, 1), inv128, sign128)


def _rope_tile(x, cos, sin_signed, first_half):
    partner = jnp.where(first_half, pltpu.roll(x, LANES - HEAD_DIM // 2, 1),
                        pltpu.roll(x, HEAD_DIM // 2, 1))
    return x * cos + partner * sin_signed


def _ada_norm(x, g, sc, sh):
    y = x * lax.rsqrt(jnp.mean(x * x, axis=-1, keepdims=True) + EPS)
    return (y * g) * (1.0 + sc) + sh


VT_W = 256
INPROJ_TILE = 512
WI_ROWS = SUBLANES


def _inproj_kernel(h_ref, sh_ref, sc_ref, g_ref, w_ref, cos_ref, sin_ref,
                   qt_ref, qit_ref, wit_ref, k_ref, ki_ref, vt_ref, rest_ref):
    u = _ada_norm(h_ref[...], g_ref[...], sc_ref[...], sh_ref[...]).astype(BF16)
    cos = cos_ref[...]
    sin = sin_ref[...]
    lane = lax.broadcasted_iota(I32, cos.shape, 1)
    first_half = (lane % HEAD_DIM) < (HEAD_DIM // 2)
    rope = lambda x: _rope_tile(x, cos, sin, first_half)

    off = 0
    q_scale = HEAD_DIM ** -0.5
    for j in range(A_WIDTH // LANES):
        z = _mm(u, w_ref[:, off + j * LANES: off + (j + 1) * LANES])
        qt_ref[j * LANES:(j + 1) * LANES, :] = (rope(z) * q_scale).T.astype(BF16)
    off += A_WIDTH
    k_ref[...] = rope(_mm(u, w_ref[:, off:off + KV_WIDTH])).astype(BF16)
    off += KV_WIDTH
    vz = _mm(u, w_ref[:, off:off + KV_WIDTH])
    for j in range(vt_ref.shape[0]):
        vt_ref[j] = vz[j * VT_W:(j + 1) * VT_W].T.astype(BF16)
    off += KV_WIDTH
    for j in range(IDX_HEADS * IDX_DIM // LANES):
        z = _mm(u, w_ref[:, off + j * LANES: off + (j + 1) * LANES])
        qit_ref[j * LANES:(j + 1) * LANES, :] = rope(z).T.astype(BF16)
    off += IDX_HEADS * IDX_DIM
    z = _mm(u, w_ref[:, off:off + KIWI_W])
    ki_ref[...] = rope(z).astype(BF16)
    wi_scale = (IDX_HEADS * IDX_DIM) ** -0.5
    wit_ref[...] = (z * wi_scale).T[IDX_DIM:IDX_DIM + WI_ROWS, :]
    off += KIWI_W
    rest_ref[...] = _mm(u, w_ref[:, off:off + REST_W])


def _inproj_call(h, sh, sc, g, w_pack, cos, sin):
    b, s, d = h.shape
    tm = min(INPROJ_TILE, s)
    tok = lambda w: pl.BlockSpec((None, tm, w), lambda bi, i: (bi, i, 0))
    tr = lambda r: pl.BlockSpec((None, r, tm), lambda bi, i: (bi, 0, i))
    per_b = pl.BlockSpec((None, 1, d), lambda bi, i: (bi, 0, 0))
    sds = jax.ShapeDtypeStruct
    return pl.pallas_call(
        _inproj_kernel,
        out_shape=(sds((b, A_WIDTH, s), BF16), sds((b, IDX_HEADS * IDX_DIM, s), BF16),
                   sds((b, WI_ROWS, s), F32), sds((b, s, KV_WIDTH), BF16),
                   sds((b, s, KIWI_W), BF16), sds((b, s // VT_W, KV_WIDTH, VT_W), BF16),
                   sds((b, s, REST_W), F32)),
        grid=(b, s // tm),
        in_specs=[tok(d), per_b, per_b,
                  pl.BlockSpec((1, d), lambda bi, i: (0, 0)),
                  pl.BlockSpec((d, PACK_W), lambda bi, i: (0, 0)),
                  tok(LANES), tok(LANES)],
        out_specs=(tr(A_WIDTH), tr(IDX_HEADS * IDX_DIM), tr(WI_ROWS), tok(KV_WIDTH), tok(KIWI_W),
                   pl.BlockSpec((None, tm // VT_W, KV_WIDTH, VT_W), lambda bi, i: (bi, i, 0, 0)),
                   tok(REST_W)),
        compiler_params=_cparams(("arbitrary", "arbitrary")),
        name="inproj",
    )(h, sh, sc, g, w_pack, cos, sin)


CNT_ROWS = 32
CNT_ACCS = 4
SEARCH_HEAD_BITS = 16
ONES_ROWS = 16
TIE_BLOCK = 128
ATTN_TRIP = 2


def _dsa_kernel(qt_ref, qit_ref, wit_ref, k_ref, ki_ref, vt_ref, lower_ref, o_ref,
                keys_ref, bias_ref, s_ref, acc_ref, *, kc, topk):
    qb = Q_BLOCK
    i = pl.program_id(1)
    t0 = i * qb
    nkc = (t0 + qb + kc - 1) // kc
    lane = lax.broadcasted_iota(I32, (1, qb), 1)
    limit = t0 + (lane // CHUNK + 1) * CHUNK
    key_off = lax.broadcasted_iota(I32, (kc, qb), 0)
    group = A_HEADS // A_KV_HEADS

    qit = qit_ref[...]
    qi_stack = jnp.concatenate(
        [qit[h * IDX_DIM:(h + 1) * IDX_DIM, :] for h in range(IDX_HEADS)], axis=1)
    wit = wit_ref[...]

    def score_body(c, carry):
        off = pl.multiple_of(c * kc, kc)
        ki = ki_ref[pl.ds(off, kc), 0:IDX_DIM]
        s = jnp.maximum(_mm(ki, qi_stack), 0.0)
        score = s[:, 0:qb] * wit[0:1, :]
        for h in range(1, IDX_HEADS):
            score = score + s[:, h * qb:(h + 1) * qb] * wit[h:h + 1, :]
        score = jnp.where(score == 0.0, 0.0, score)
        bits = lax.bitcast_convert_type(score, I32)
        key = jnp.where(bits < 0, bits ^ jnp.int32(0x7FFFFFFF), bits)
        keys_ref[pl.ds(off, kc), :] = jnp.where(off + key_off < limit, key, jnp.int32(INT_MIN))
        return carry

    lax.fori_loop(0, nkc, score_body, 0)

    def count(pred):
        def body(c, accs):
            off = pl.multiple_of(c * kc, kc)
            blk = keys_ref[pl.ds(off, kc), :]
            accs = list(accs)
            for j in range(kc // CNT_ROWS):
                a = accs[j % CNT_ACCS]
                accs[j % CNT_ACCS] = jnp.where(pred(blk[j * CNT_ROWS:(j + 1) * CNT_ROWS]), a + 1.0, a)
            return tuple(accs)
        accs = lax.fori_loop(0, nkc, body,
                             tuple(jnp.zeros((CNT_ROWS, qb), F32) for _ in range(CNT_ACCS)))
        return jnp.sum((accs[0] + accs[1]) + (accs[2] + accs[3]), axis=0, keepdims=True)

    topk_f = float(topk)
    cnt0 = count(lambda kk: kk >= 0)
    cnt1 = count(lambda kk: kk >= 1)
    nonneg = cnt0 >= topk_f
    thr0 = jnp.where(nonneg, 0, INT_MIN).astype(I32)
    cthr0 = jnp.where(nonneg, cnt0, 2.0 * kc * (nkc + 1).astype(F32))
    done0 = jnp.where((nonneg & (cnt1 < topk_f)) | (cthr0 == topk_f) | (limit <= topk), 1.0, 0.0)

    def try_bit(state, b):
        thr, cthr, done = state
        cand = thr + jnp.left_shift(jnp.int32(1), b)
        cnt = count(lambda kk: kk >= cand)
        take = (cnt >= topk_f) & (done < 0.5)
        thr = jnp.where(take, cand, thr)
        cthr = jnp.where(take, cnt, cthr)
        return thr, cthr, jnp.where(cthr == topk_f, 1.0, done)

    state = lax.fori_loop(0, SEARCH_HEAD_BITS, lambda j, st: try_bit(st, 30 - j),
                          (thr0, cthr0, done0))
    steps = 3
    tail_bits = 31 - SEARCH_HEAD_BITS

    def thr_cond(carry):
        g, st = carry
        return (g * steps < tail_bits) & (jnp.min(st[2]) < 0.5)

    def thr_body(carry):
        g, st = carry
        for jj in range(steps):
            st = try_bit(st, tail_bits - 1 - (g * steps + jj))
        return g + 1, st

    _, (thr, _, _) = lax.while_loop(thr_cond, thr_body, (jnp.int32(0), state))

    need = topk_f - count(lambda kk: kk > thr)
    lower = lower_ref[...]
    tie_off = lax.broadcasted_iota(I32, (TIE_BLOCK, qb), 0)

    def bias_body(c, seen):
        offs = [pl.multiple_of(c * kc + j * TIE_BLOCK, TIE_BLOCK) for j in range(kc // TIE_BLOCK)]
        kks = [keys_ref[pl.ds(off, TIE_BLOCK), :] for off in offs]
        ties = [jnp.where(kk == thr, 1.0, 0.0) for kk in kks]
        ranks = [_mm(lower, t.astype(BF16)) for t in ties]
        for off, kk, tie_f, rank in zip(offs, kks, ties, ranks):
            sel = ((kk > thr) | ((kk == thr) & (rank + seen <= need))) & (off + tie_off < limit)
            bias_ref[pl.ds(off, TIE_BLOCK), :] = jnp.where(sel, 0.0, NEG_BIG)
            seen = seen + jnp.sum(tie_f, axis=0, keepdims=True)
        return seen

    lax.fori_loop(0, nkc, bias_body, jnp.zeros((1, qb), F32))

    qt = qt_ref[...]
    q_n = [jnp.concatenate([qt[(n * group + g) * HEAD_DIM:(n * group + g + 1) * HEAD_DIM, :]
                            for g in range(group)], axis=1) for n in range(A_KV_HEADS)]

    per = kc // VT_W

    def max_trip(c0, chunks, parts):
        out = list(parts)
        for j in range(chunks * per):
            cv = c0 * per + j
            off = pl.multiple_of(cv * VT_W, VT_W)
            bias = bias_ref[pl.ds(off, VT_W), :]
            for n in range(A_KV_HEADS):
                s = _mm(k_ref[pl.ds(off, VT_W), n * HEAD_DIM:(n + 1) * HEAD_DIM], q_n[n])
                s = jnp.concatenate([s[:, g * qb:(g + 1) * qb] + bias for g in range(group)], axis=1)
                s_ref[cv, n] = s
                out[n] = jnp.maximum(out[n], jnp.max(
                    s.reshape(VT_W // SUBLANES, SUBLANES, group * qb), axis=0))
        return tuple(out)

    n_trips = nkc // ATTN_TRIP
    odd = nkc - n_trips * ATTN_TRIP
    parts = tuple(jnp.full((SUBLANES, group * qb), NEG_BIG, F32) for _ in range(A_KV_HEADS))
    parts = lax.fori_loop(0, n_trips, lambda t, p: max_trip(t * ATTN_TRIP, ATTN_TRIP, p), parts)
    parts = lax.cond(odd > 0, lambda p: max_trip(nkc - 1, 1, p), lambda p: p, parts)
    m_n = [jnp.max(p, axis=0, keepdims=True) for p in parts]

    acc_ref[...] = jnp.zeros(acc_ref.shape, F32)
    ones = jnp.ones((ONES_ROWS, VT_W), BF16)

    def pv_trip(c0, chunks):
        for j in range(chunks * per):
            cv = c0 * per + j
            for n in range(A_KV_HEADS):
                pt = jnp.exp(s_ref[cv, n] - m_n[n]).astype(BF16)
                vt = jnp.concatenate([vt_ref[cv, n * HEAD_DIM:(n + 1) * HEAD_DIM, :], ones], axis=0)
                acc_ref[n] += _mm(vt, pt)

    def pv_body(t, carry):
        pv_trip(t * ATTN_TRIP, ATTN_TRIP)
        return carry

    lax.fori_loop(0, n_trips, pv_body, 0)

    @pl.when(odd > 0)
    def _():
        pv_trip(nkc - 1, 1)

    for n in range(A_KV_HEADS):
        for g in range(group):
            hh = n * group + g
            cols = slice(g * qb, (g + 1) * qb)
            o_ref[hh * HEAD_DIM:(hh + 1) * HEAD_DIM, :] = (
                acc_ref[n, 0:HEAD_DIM, cols] / acc_ref[n, HEAD_DIM:HEAD_DIM + 1, cols])


def _dsa_call(qt, qit, wit, k, ki, vt, kc=512):
    b, _, s = qt.shape
    kc = min(kc, s)
    assert kc % VT_W == 0 and s % kc == 0
    topk = min(IDX_TOPK_MAX, s // 4)
    group = A_HEADS // A_KV_HEADS
    lower = jnp.asarray(np.tril(np.ones((TIE_BLOCK, TIE_BLOCK), np.float32)), BF16)
    qblk = lambda r: pl.BlockSpec((None, r, Q_BLOCK), lambda bi, i: (bi, 0, i))
    seq = lambda w: pl.BlockSpec((None, s, w), lambda bi, i: (bi, 0, 0))
    return pl.pallas_call(
        functools.partial(_dsa_kernel, kc=kc, topk=topk),
        out_shape=jax.ShapeDtypeStruct((b, A_WIDTH, s), F32),
        grid=(b, s // Q_BLOCK),
        in_specs=[qblk(A_WIDTH), qblk(IDX_HEADS * IDX_DIM), qblk(WI_ROWS),
                  seq(KV_WIDTH), seq(KIWI_W),
                  pl.BlockSpec((None, s // VT_W, KV_WIDTH, VT_W), lambda bi, i: (bi, 0, 0, 0)),
                  pl.BlockSpec((TIE_BLOCK, TIE_BLOCK), lambda bi, i: (0, 0))],
        out_specs=qblk(A_WIDTH),
        scratch_shapes=[pltpu.VMEM((s, Q_BLOCK), I32),
                        pltpu.VMEM((s, Q_BLOCK), F32),
                        pltpu.VMEM((s // VT_W, A_KV_HEADS, VT_W, group * Q_BLOCK), F32),
                        pltpu.VMEM((A_KV_HEADS, HEAD_DIM + ONES_ROWS, group * Q_BLOCK), F32)],
        compiler_params=_cparams(("arbitrary", "arbitrary")),
        name="dsa",
    )(qt, qit, wit, k, ki, vt, lower)


POOL_HALO = 32


def _pool_kernel(u_ref, w_ref, scale_ref, o_ref, x_buf, a_buf, b_buf, *, tm):
    hl = POOL_HALO
    rows = tm + hl
    first = pl.program_id(1) == 0

    @pl.when(first)
    def _():
        x_buf[0:hl, :] = jnp.zeros((hl, B_WIDTH), F32)

    x = u_ref[...]
    x_buf[hl:rows, :] = x
    a_buf[8:rows, :] = x_buf[8:rows, :] + x_buf[7:rows - 1, :]
    b_buf[16:rows, :] = a_buf[16:rows, :] + a_buf[14:rows - 2, :]
    w2 = a_buf[hl:rows, :]
    w4 = b_buf[hl:rows, :]
    a_buf[24:rows, :] = b_buf[24:rows, :] + b_buf[20:rows - 4, :]
    w8 = a_buf[hl:rows, :]
    b_buf[hl:rows, :] = a_buf[hl:rows, :] + a_buf[hl - 8:rows - 8, :]
    w16 = b_buf[hl:rows, :]
    x_buf[0:hl, :] = x_buf[tm:rows, :]

    lane = lax.broadcasted_iota(I32, (tm, B_WIDTH), 1)
    grp = lane // POOL_GROUP
    wsum = jnp.where(grp == 0, w2, jnp.where(grp == 1, w4, jnp.where(grp == 2, w8, w16)))
    win = jnp.where(grp == 0, 2, jnp.where(grp == 1, 4, jnp.where(grp == 2, 8, 16)))
    t = pl.program_id(1) * tm + lax.broadcasted_iota(I32, (tm, B_WIDTH), 0)
    cnt = jnp.minimum(t + 1, win).astype(F32)
    pooled = wsum / cnt - x
    y = _mm(pooled.astype(BF16), w_ref[...])
    o_ref[...] = y * scale_ref[...]


def _pool_call(rest, w_bd, scale, tm=512):
    b, s, _ = rest.shape
    tm = min(tm, s)
    rows = tm + POOL_HALO
    return pl.pallas_call(
        functools.partial(_pool_kernel, tm=tm),
        out_shape=jax.ShapeDtypeStruct((b, s, B_WIDTH), F32),
        grid=(b, s // tm),
        in_specs=[pl.BlockSpec((None, tm, B_WIDTH), lambda bi, i: (bi, i, 0)),
                  pl.BlockSpec((B_WIDTH, B_WIDTH), lambda bi, i: (0, 0)),
                  pl.BlockSpec((1, B_WIDTH), lambda bi, i: (0, 0))],
        out_specs=pl.BlockSpec((None, tm, B_WIDTH), lambda bi, i: (bi, i, 0)),
        scratch_shapes=[pltpu.VMEM((rows, B_WIDTH), F32)] * 3,
        compiler_params=_cparams(("arbitrary", "arbitrary")),
        name="pool",
    )(rest, w_bd, scale)


def _hgrn_consts():
    tril = np.tril(np.ones((CHUNK, CHUNK), np.float32))
    mats = [tril]
    r = np.arange(CHUNK)
    for h in HGRN_LEVELS:
        mats.append(tril[(r // (2 * h)) * (2 * h) + h - 1])
    return np.concatenate(mats, axis=0)


def _split3(x):
    hi = x.astype(BF16)
    r1 = x - hi.astype(F32)
    mid = r1.astype(BF16)
    lo = (r1 - mid.astype(F32)).astype(BF16)
    return hi, mid, lo


def _hgrn_kernel(q_ref, f_ref, i_ref, g_ref, lb_ref, ng_ref, cm_ref, bd_ref, o_ref, state_ref,
                 *, layer, tm):
    @pl.when(pl.program_id(1) == 0)
    def _():
        state_ref[...] = jnp.zeros(state_ref.shape, F32)

    lbl = lb_ref[...]
    e = jnp.exp(lbl - jnp.max(lbl, axis=0, keepdims=True))
    p = e / jnp.sum(e, axis=0, keepdims=True)
    cum = p[0:1]
    for l in range(1, layer + 1):
        cum = cum + p[l:l + 1]
    lb = jnp.clip(cum - p[0:1], 0.0, 1.0)

    cm = cm_ref[...]
    ng = ng_ref[...]
    bd = bd_ref[...]
    bd_f = bd.astype(F32)
    row = lax.broadcasted_iota(I32, (CHUNK, 1), 0)
    tt = lax.broadcasted_iota(I32, (CHUNK, C_WIDTH), 0)
    ss = lax.broadcasted_iota(I32, (CHUNK, C_WIDTH), 1) % CHUNK
    lvl_mask = [tt == ss] + [(tt // (2 * h)) == (ss // (2 * h)) for h in HGRN_LEVELS]
    w = C_WIDTH
    heads = C_WIDTH // C_KDIM

    def expand(x16):
        return jnp.concatenate([x16] * heads, axis=0) * bd

    chunks = range(tm // CHUNK)
    rows = [slice(ci * CHUNK, (ci + 1) * CHUNK) for ci in chunks]
    z = f_ref[...]
    log_f = jnp.log(lb + (1.0 - lb) * jax.nn.sigmoid(z))
    kin = (1.0 - lb) * jax.nn.sigmoid(-z)
    qx = q_ref[...]
    qv = qx * jax.nn.sigmoid(qx)
    vb16 = i_ref[...].astype(BF16)
    hi, mid, lo = _split3(log_f)
    lf3 = jnp.concatenate([hi, mid, lo], axis=1)
    cs = [_mm(cm, lf3[r]) for r in rows]
    cs = [c[:, 0:w] + c[:, w:2 * w] + c[:, 2 * w:3 * w] for c in cs]
    odd = [((row // h) % 2) == 1 for h in HGRN_LEVELS]
    attn, q_dec, upd, s_dec = [], [], [], []
    for ci in chunks:
        r = rows[ci]
        bcum = cs[ci][0:CHUNK]
        b_last = bcum[CHUNK - 1:CHUNK]
        q_c, k_c = qv[r], kin[r]
        q_dec.append((q_c * jnp.exp(bcum)).astype(BF16))
        s_dec.append(jnp.exp(b_last))
        upd.append(_tn(vb16[r], (k_c * jnp.exp(b_last - bcum)).astype(BF16)) * bd_f)
        qs, ks = [q_c.astype(BF16)], [k_c.astype(BF16)]
        for li in range(len(HGRN_LEVELS)):
            ref = cs[ci][(li + 1) * CHUNK:(li + 2) * CHUNK]
            qs.append((q_c * jnp.exp(jnp.where(odd[li], bcum - ref, NEG_BIG))).astype(BF16))
            ks.append((k_c * jnp.exp(jnp.where(odd[li], NEG_BIG, ref - bcum))).astype(BF16))
        a = jnp.zeros((CHUNK, w), F32)
        for mask, ql, kl in zip(lvl_mask, qs, ks):
            a = a + jnp.where(mask, _nt(ql, expand(kl)), 0.0)
        attn.append(a)
    o_intra = [_mm(attn[ci].astype(BF16), expand(vb16[rows[ci]])) for ci in chunks]
    st = state_ref[...]
    outs = []
    for ci in chunks:
        outs.append(_nt(q_dec[ci], st.astype(BF16)) + o_intra[ci])
        st = st * s_dec[ci] + upd[ci]
    state_ref[...] = st
    o = jnp.concatenate(outs, axis=0)
    o2h, o2m, o2l = _split3(o * o)
    ms = (_mm(o2h, bd) + _mm(o2m, bd) + _mm(o2l, bd)) * (1.0 / C_VDIM)
    gx = g_ref[...]
    o_ref[...] = (o * lax.rsqrt(ms + EPS) * ng) * (gx * jax.nn.sigmoid(gx))


def _hgrn_call(rest, lb_logits, norm_g, layer, tm=256):
    b, s, _ = rest.shape
    depth = lb_logits.shape[0]
    cm = jnp.asarray(_hgrn_consts(), BF16)
    head_of = np.arange(C_WIDTH) // C_KDIM
    bd = jnp.asarray(head_of[:, None] == head_of[None, :], BF16)
    col = lambda j: pl.BlockSpec((None, tm, C_WIDTH), lambda bi, i, j=j: (bi, i, j))
    return pl.pallas_call(
        functools.partial(_hgrn_kernel, layer=layer, tm=tm),
        out_shape=jax.ShapeDtypeStruct((b, s, C_WIDTH), F32),
        grid=(b, s // tm),
        in_specs=[col(1), col(2), col(3), col(4),
                  pl.BlockSpec((depth, C_WIDTH), lambda bi, i: (0, 0)),
                  pl.BlockSpec((1, C_WIDTH), lambda bi, i: (0, 0)),
                  pl.BlockSpec(cm.shape, lambda bi, i: (0, 0)),
                  pl.BlockSpec((C_WIDTH, C_WIDTH), lambda bi, i: (0, 0))],
        out_specs=pl.BlockSpec((None, tm, C_WIDTH), lambda bi, i: (bi, i, 0)),
        scratch_shapes=[pltpu.VMEM((C_WIDTH, C_WIDTH), F32)],
        compiler_params=_cparams(("arbitrary", "arbitrary")),
        name="hgrn2",
    )(rest, rest, rest, rest, lb_logits, jnp.tile(norm_g, C_HEADS).reshape(1, C_WIDTH), cm, bd)


RANK_BITS = 20
CODE_ROWS = 8


def _outproj_router_kernel(ya_ref, yb_ref, yc_ref, h_ref, g1_ref, wo_ref, sh_ref, sc_ref, g_ref,
                           wr_ref, br_ref, tri_ref, h1_ref, u_ref, code_ref, gk_ref, cnt_ref,
                           run_ref):
    @pl.when((pl.program_id(0) == 0) & (pl.program_id(1) == 0))
    def _():
        run_ref[...] = jnp.zeros(run_ref.shape, F32)

    y = _tn(ya_ref[...].astype(BF16), wo_ref[0:A_WIDTH, :])
    y = y + _mm(yb_ref[...].astype(BF16), wo_ref[A_WIDTH:A_WIDTH + B_WIDTH, :])
    y = y + _mm(yc_ref[...].astype(BF16), wo_ref[A_WIDTH + B_WIDTH:, :])
    h1 = h_ref[...] + g1_ref[...] * y
    h1_ref[...] = h1
    u = _ada_norm(h1, g_ref[...], sc_ref[...], sh_ref[...])
    u_ref[...] = u

    u_hi = u.astype(BF16)
    u_lo = (u - u_hi.astype(F32)).astype(BF16)
    wr = wr_ref[...]
    wr_hi = wr.astype(BF16)
    wr_lo = (wr - wr_hi.astype(F32)).astype(BF16)
    logits = _mm(u_hi, wr_hi) + (_mm(u_hi, wr_lo) + _mm(u_lo, wr_hi)) + br_ref[...]
    lane = lax.broadcasted_iota(I32, logits.shape, 1).astype(F32)
    work = jnp.where(lane < N_EXPERTS, logits, -jnp.inf)
    picks, firsts, tops = [], [], []
    for k in range(TOP_K):
        m = jnp.max(work, axis=1, keepdims=True)
        first = jnp.min(jnp.where(work == m, lane, float(LANES)), axis=1, keepdims=True)
        pick = lane == first
        work = jnp.where(pick, -jnp.inf, work)
        picks.append(pick)
        firsts.append(first)
        tops.append(m)
    ex = [jnp.exp(m - tops[0]) for m in tops]
    den = ex[0] + ex[1] + ex[2] + ex[3]

    sel = picks[0] | picks[1] | picks[2] | picks[3]
    sel_f = jnp.where(sel, 1.0, 0.0)
    prefix = _mm(tri_ref[...], sel_f.astype(BF16)) + run_ref[...]
    run_ref[...] += jnp.sum(sel_f, axis=0, keepdims=True)
    cnt_ref[...] = run_ref[...]

    code = jnp.zeros(logits.shape, I32)
    gk = jnp.zeros(logits.shape, F32)
    for k in range(TOP_K):
        rank = jnp.sum(jnp.where(picks[k], prefix, 0.0), axis=1, keepdims=True)
        ck = (firsts[k].astype(I32) << RANK_BITS) | rank.astype(I32)
        code = jnp.where(lane == float(k), ck, code)
        gk = jnp.where(lane == float(k), ex[k] / den, gk)
    code_ref[...] = code.T[0:CODE_ROWS, :]
    gk_ref[...] = gk


def _outproj_router_call(ya, yb, yc, h, g1, w_out, sh, sc, g, w_router, b_router, tm=512):
    b, s, d = h.shape
    tm = min(tm, s)
    tri = jnp.asarray(np.tril(np.ones((tm, tm), np.float32), -1), BF16)
    tok = lambda w: pl.BlockSpec((None, tm, w), lambda bi, i: (bi, i, 0))
    per_b = pl.BlockSpec((None, 1, d), lambda bi, i: (bi, 0, 0))
    full = lambda a: pl.BlockSpec(a.shape, lambda bi, i: (0,) * a.ndim)
    return pl.pallas_call(
        _outproj_router_kernel,
        out_shape=(jax.ShapeDtypeStruct((b, s, d), F32), jax.ShapeDtypeStruct((b, s, d), F32),
                   jax.ShapeDtypeStruct((CODE_ROWS, b * s), I32),
                   jax.ShapeDtypeStruct((b, s, LANES), F32), jax.ShapeDtypeStruct((1, LANES), F32)),
        grid=(b, s // tm),
        in_specs=[pl.BlockSpec((None, A_WIDTH, tm), lambda bi, i: (bi, 0, i)),
                  tok(B_WIDTH), tok(C_WIDTH), tok(d), per_b, full(w_out),
                  per_b, per_b, full(g), full(w_router), full(b_router), full(tri)],
        out_specs=(tok(d), tok(d),
                   pl.BlockSpec((CODE_ROWS, tm), lambda bi, i: (0, bi * (s // tm) + i)),
                   tok(LANES), pl.BlockSpec((1, LANES), lambda bi, i: (0, 0))),
        scratch_shapes=[pltpu.VMEM((1, LANES), F32)],
        compiler_params=_cparams(("arbitrary", "arbitrary")),
        name="outproj_router",
    )(ya, yb, yc, h, g1, w_out, sh, sc, g, w_router, b_router, tri)


MOE_BLOCK = 512
PERM_W = 2 * LANES


ROWS_TILE = 2048


def _rows_kernel(start_ref, code_ref, row_ref):
    code = code_ref[...]
    expert = code >> RANK_BITS
    base = jnp.zeros(code.shape, I32)
    for e in range(N_EXPERTS):
        base = jnp.where(expert == e, start_ref[e], base)
    row_ref[...] = base + (code & ((1 << RANK_BITS) - 1))


def _rows_call(starts, code):
    r, n = code.shape
    tile = min(ROWS_TILE, n)
    return pl.pallas_call(
        _rows_kernel,
        out_shape=jax.ShapeDtypeStruct((r, n), I32),
        grid_spec=pltpu.PrefetchScalarGridSpec(
            num_scalar_prefetch=1, grid=(n // tile,),
            in_specs=[pl.BlockSpec((r, tile), lambda i, *_: (0, i))],
            out_specs=pl.BlockSpec((r, tile), lambda i, *_: (0, i))),
        compiler_params=_cparams(("arbitrary",)),
        name="moe_rows",
    )(starts, code)


def _drain_rows(src_row, dst_row, sem, n):
    def body(t, c):
        pltpu.make_async_copy(src_row, dst_row, sem).wait()
        return c
    lax.fori_loop(0, n, body, 0, unroll=4)


ZERO_ROWS = MOE_BLOCK // 2


def _dispatch_kernel(row_ref, padlo_ref, padhi_ref, x_ref, xs_ref, zbuf, sem, zsem, *, tm):
    tok0 = pl.program_id(0) * tm
    n_tok = row_ref.shape[0] // TOP_K

    @pl.when(pl.program_id(0) == 0)
    def _():
        zbuf[...] = jnp.zeros(zbuf.shape, F32)

        def pad_copies(e, fn):
            lo = padlo_ref[e]
            hi = padhi_ref[e]
            lo8 = jnp.minimum((lo + SUBLANES - 1) // SUBLANES * SUBLANES, hi)
            for j in range(SUBLANES - 1):

                @pl.when(lo + j < lo8)
                def _(j=j):
                    fn(pltpu.make_async_copy(zbuf.at[pl.ds(0, 1)], xs_ref.at[pl.ds(lo + j, 1)], zsem))
            n8 = hi - lo8
            size = ZERO_ROWS
            while size >= SUBLANES:
                off = pl.multiple_of(lo8 + (n8 & ~(2 * size - 1)), SUBLANES)

                @pl.when((n8 & size) != 0)
                def _(size=size, off=off):
                    fn(pltpu.make_async_copy(zbuf.at[pl.ds(0, size)], xs_ref.at[pl.ds(off, size)],
                                             zsem))
                size //= 2

        def tail_copies(fn):
            tail_lo = padhi_ref[N_EXPERTS - 1]

            def piece(j, c):
                off = pl.multiple_of(tail_lo + j * ZERO_ROWS, ZERO_ROWS)

                @pl.when(off < xs_ref.shape[0])
                def _():
                    fn(pltpu.make_async_copy(zbuf, xs_ref.at[pl.ds(off, ZERO_ROWS)], zsem))
                return c

            lax.fori_loop(0, N_EXPERTS * MOE_BLOCK // ZERO_ROWS, piece, 0)

        def start_e(e, c):
            pad_copies(e, lambda cp: cp.start())
            return c

        def wait_e(e, c):
            pad_copies(e, lambda cp: cp.wait())
            return c

        lax.fori_loop(0, N_EXPERTS, start_e, 0)
        tail_copies(lambda cp: cp.start())
        lax.fori_loop(0, N_EXPERTS, wait_e, 0)
        tail_copies(lambda cp: cp.wait())

    def issue(t8, c):
        base = pl.multiple_of(t8 * SUBLANES, SUBLANES)
        for j in range(SUBLANES):
            for k in range(TOP_K):
                dst = row_ref[k * n_tok + tok0 + base + j]
                pltpu.make_async_copy(x_ref.at[pl.ds(base + j, 1)], xs_ref.at[pl.ds(dst, 1)],
                                      sem).start(priority=k % 2)
        return c

    lax.fori_loop(0, tm // SUBLANES, issue, 0)
    _drain_rows(x_ref.at[pl.ds(0, 1)], xs_ref.at[pl.ds(0, 1)], sem, tm * TOP_K)


def _dispatch_call(row_ids, pad_lo, pad_hi, x, rows, tm=512):
    n, d = x.shape
    return pl.pallas_call(
        functools.partial(_dispatch_kernel, tm=tm),
        out_shape=jax.ShapeDtypeStruct((rows, d), F32),
        grid_spec=pltpu.PrefetchScalarGridSpec(
            num_scalar_prefetch=3, grid=(n // tm,),
            in_specs=[pl.BlockSpec((tm, d), lambda i, *_: (i, 0))],
            out_specs=pl.BlockSpec(memory_space=pl.ANY),
            scratch_shapes=[pltpu.VMEM((ZERO_ROWS, d), F32), pltpu.SemaphoreType.DMA,
                            pltpu.SemaphoreType.DMA]),
        compiler_params=_cparams(("arbitrary",)),
        name="moe_dispatch",
    )(row_ids, pad_lo, pad_hi, x)


def _ffn_kernel(blk_e_ref, nb_ref, xs_ref, w1_ref, b1g_ref, b1l_ref, w2_ref, b2_ref, perm_ref,
                ys_ref, w1g_s, w1l_s, w2_s):
    i = pl.program_id(0)
    e = blk_e_ref[i]
    live = i < nb_ref[0]
    fresh = (i == 0) | (e != blk_e_ref[jnp.maximum(i - 1, 0)])

    @pl.when(live & fresh)
    def _():
        perm = perm_ref[...]
        for j in range(w1_ref.shape[1] // PERM_W):
            t = _mm(w1_ref[:, j * PERM_W:(j + 1) * PERM_W].astype(BF16), perm).astype(BF16)
            w1g_s[:, j * LANES:(j + 1) * LANES] = t[:, 0:LANES]
            w1l_s[:, j * LANES:(j + 1) * LANES] = t[:, LANES:PERM_W]
        w2_s[...] = w2_ref[...].astype(BF16)

    @pl.when(live)
    def _():
        x = xs_ref[...].astype(BF16)
        glu = jnp.minimum(_mm(x, w1g_s[...]) + b1g_ref[...], SWIGLU_LIMIT)
        lin = jnp.clip(_mm(x, w1l_s[...]) + b1l_ref[...], -SWIGLU_LIMIT, SWIGLU_LIMIT)
        act = glu * jax.nn.sigmoid(SWIGLU_ALPHA * glu) * (lin + 1.0)
        ys_ref[...] = _mm(act.astype(BF16), w2_s[...]) + b2_ref[...]

    @pl.when(jnp.logical_not(live))
    def _():
        ys_ref[...] = jnp.zeros(ys_ref.shape, F32)


def _ffn_call(blk_e, nb_used, xs, w1, b1g, b1l, w2, b2, layer):
    rows, d = xs.shape
    _, ne, _, ff2 = w1.shape
    ff = ff2 // 2
    perm = np.zeros((PERM_W, PERM_W), np.float32)
    perm[2 * np.arange(LANES), np.arange(LANES)] = 1.0
    perm[2 * np.arange(LANES) + 1, LANES + np.arange(LANES)] = 1.0
    ex = lambda r, c: pl.BlockSpec((None, r, c), lambda i, be, nb: (be[i], 0, 0))
    exl = lambda r, c: pl.BlockSpec((None, None, r, c), lambda i, be, nb: (layer, be[i], 0, 0))
    return pl.pallas_call(
        _ffn_kernel,
        out_shape=jax.ShapeDtypeStruct((rows, d), F32),
        grid_spec=pltpu.PrefetchScalarGridSpec(
            num_scalar_prefetch=2, grid=(rows // MOE_BLOCK,),
            in_specs=[pl.BlockSpec((MOE_BLOCK, d),
                                   lambda i, be, nb: (jnp.minimum(i, jnp.maximum(nb[0] - 1, 0)), 0)),
                      exl(d, ff2), ex(1, ff), ex(1, ff), exl(ff, d), ex(1, d),
                      pl.BlockSpec((PERM_W, PERM_W), lambda i, *_: (0, 0))],
            out_specs=pl.BlockSpec((MOE_BLOCK, d), lambda i, *_: (i, 0)),
            scratch_shapes=[pltpu.VMEM((d, ff), BF16), pltpu.VMEM((d, ff), BF16),
                            pltpu.VMEM((ff, d), BF16)]),
        compiler_params=_cparams(("arbitrary",)),
        name="moe_ffn",
    )(blk_e, nb_used, xs, w1, b1g, b1l, w2, b2, jnp.asarray(perm, BF16))


def _combine_kernel(row_ref, gk_ref, h_ref, g2_ref, fg_ref, ys_ref, o_ref, buf, sem,
                    *, tm, final_norm):
    tok0 = pl.program_id(0) * tm
    n_tok = row_ref.shape[0] // TOP_K

    def issue(t8, c):
        base = pl.multiple_of(t8 * SUBLANES, SUBLANES)
        for j in range(SUBLANES):
            for k in range(TOP_K):
                src = row_ref[k * n_tok + tok0 + base + j]
                pltpu.make_async_copy(ys_ref.at[pl.ds(src, 1)], buf.at[k, pl.ds(base + j, 1)],
                                      sem).start(priority=k % 2)
        return c

    lax.fori_loop(0, tm // SUBLANES, issue, 0)
    _drain_rows(ys_ref.at[pl.ds(0, 1)], buf.at[0, pl.ds(0, 1)], sem, tm * TOP_K)

    gk = gk_ref[...]
    acc = buf[0] * gk[:, 0:1]
    for k in range(1, TOP_K):
        acc = acc + buf[k] * gk[:, k:k + 1]
    out = h_ref[...] + g2_ref[...] * acc
    if final_norm:
        out = out * lax.rsqrt(jnp.mean(out * out, axis=-1, keepdims=True) + EPS) * fg_ref[...]
    o_ref[...] = out


def _combine_call(row_ids, gk, h1, g2, final_g, ys, seq, final_norm, tm=256):
    n, d = h1.shape
    return pl.pallas_call(
        functools.partial(_combine_kernel, tm=tm, final_norm=final_norm),
        out_shape=jax.ShapeDtypeStruct((n, d), F32),
        grid_spec=pltpu.PrefetchScalarGridSpec(
            num_scalar_prefetch=1, grid=(n // tm,),
            in_specs=[pl.BlockSpec((tm, LANES), lambda i, *_: (i, 0)),
                      pl.BlockSpec((tm, d), lambda i, *_: (i, 0)),
                      pl.BlockSpec((None, 1, d), lambda i, *_: ((i * tm) // seq, 0, 0)),
                      pl.BlockSpec((1, d), lambda i, *_: (0, 0)),
                      pl.BlockSpec(memory_space=pl.ANY)],
            out_specs=pl.BlockSpec((tm, d), lambda i, *_: (i, 0)),
            scratch_shapes=[pltpu.VMEM((TOP_K, tm, d), F32), pltpu.SemaphoreType.DMA]),
        compiler_params=_cparams(("arbitrary",)),
        name="moe_combine",
    )(row_ids, gk, h1, g2, final_g, ys)


def _moe_call(u, code, gk, counts, h1, g2, w1, b1, w2, b2, final_g, layer, final_norm):
    b, s, d = h1.shape
    n = b * s
    ne = w1.shape[1]
    rows = n * TOP_K + ne * MOE_BLOCK
    cnt = counts[0, :ne].astype(I32)
    padded = (cnt + MOE_BLOCK - 1) // MOE_BLOCK * MOE_BLOCK
    ends = jnp.cumsum(padded)
    starts = (ends - padded).astype(I32)
    blk_start = jnp.arange(rows // MOE_BLOCK, dtype=I32) * MOE_BLOCK
    blk_e = jnp.minimum(jnp.sum((ends[None, :] <= blk_start[:, None]).astype(I32), axis=1), ne - 1)
    nb_used = (ends[-1:] // MOE_BLOCK).astype(I32)

    row_ids = _rows_call(starts, code)[0:TOP_K].reshape(TOP_K * n)

    xs = _dispatch_call(row_ids, starts + cnt, ends.astype(I32), u.reshape(n, d), rows)
    ff = w1.shape[3] // 2
    ys = _ffn_call(blk_e, nb_used, xs, w1, b1[:, 0::2].reshape(ne, 1, ff),
                   b1[:, 1::2].reshape(ne, 1, ff), w2, b2.reshape(ne, 1, d), layer)
    out = _combine_call(row_ids, gk.reshape(n, LANES), h1.reshape(n, d), g2, final_g, ys, s,
                        final_norm)
    return out.reshape(b, s, d)


def _pack_w_in(w_in):
    d = w_in.shape[0]
    offs = np.cumsum((0,) + IN_SPLITS)
    head = w_in[:, :offs[5]]
    wi = w_in[:, offs[5]:offs[6]]
    pad = jnp.zeros((d, KIWI_W - IDX_DIM - IDX_HEADS), w_in.dtype)
    return jnp.concatenate([head, wi, pad, w_in[:, offs[6]:]], axis=1).astype(BF16)


def _block_diag(pool_w):
    g, c, _ = pool_w.shape
    out = jnp.zeros((g * c, g * c), pool_w.dtype)
    for j in range(g):
        out = out.at[j * c:(j + 1) * c, j * c:(j + 1) * c].set(pool_w[j])
    return out.astype(BF16)


def kernel(x, c, positions, w_ada, b_ada, norm1_g, norm2_g, w_in, w_out, pool_w, pool_scale,
           hgrn_norm_g, lb_logits, w_router, b_router, w1, b1, w2, b2, final_g):
    bsz, s, d = x.shape
    depth = w_ada.shape[0]
    mod = _ada_call(c, w_ada, b_ada)
    cos, sin = _rope_table_call(positions)
    h = x
    for l in range(depth):
        sh1, sc1, g1, sh2, sc2, g2 = [mod[l, :, j * d:(j + 1) * d].reshape(bsz, 1, d)
                                      for j in range(6)]
        qt, qit, wit, k, ki, vt, rest = _inproj_call(
            h, sh1, sc1, norm1_g[l].reshape(1, d), _pack_w_in(w_in[l]), cos, sin)
        ya = _dsa_call(qt, qit, wit, k, ki, vt)
        yb = _pool_call(rest, _block_diag(pool_w[l]), pool_scale[l].reshape(1, B_WIDTH))
        yc = _hgrn_call(rest, lb_logits, hgrn_norm_g[l], l)
        wr = jnp.pad(w_router[l], ((0, 0), (0, LANES - N_EXPERTS)))
        br = jnp.pad(b_router[l], (0, LANES - N_EXPERTS)).reshape(1, LANES)
        h1, u2, code, gk, counts = _outproj_router_call(
            ya, yb, yc, h, g1, w_out[l].astype(BF16), sh2, sc2, norm2_g[l].reshape(1, d), wr, br)
        h = _moe_call(u2, code, gk, counts, h1, g2, w1, b1[l], w2, b2[l],
                      final_g.reshape(1, d), l, final_norm=(l == depth - 1))
    return h
```

```python
import functools

import numpy as np
import jax
import jax.numpy as jnp
from jax import lax
from jax.experimental import pallas as pl
from jax.experimental.pallas import tpu as pltpu

F32 = jnp.float32
BF16 = jnp.bfloat16
I32 = jnp.int32

CHUNK = 64
EPS = 1e-6
NEG_BIG = -1e30
A_HEADS, A_KV_HEADS, HEAD_DIM = 8, 2, 64
IDX_HEADS, IDX_DIM, IDX_TOPK_MAX = 4, 64, 256
Q_BLOCK = 128
ROPE_THETA = 10000.0
A_WIDTH = A_HEADS * HEAD_DIM
POOL_WINDOWS = (2, 4, 8, 16)
POOL_GROUP = 64
B_WIDTH = len(POOL_WINDOWS) * POOL_GROUP
C_HEADS, C_KDIM, C_VDIM = 4, 64, 64
C_WIDTH = C_HEADS * C_VDIM
N_EXPERTS, TOP_K = 32, 4
SWIGLU_LIMIT, SWIGLU_ALPHA = 7.0, 1.702
KV_WIDTH = A_KV_HEADS * HEAD_DIM
IN_SPLITS = (A_WIDTH, KV_WIDTH, KV_WIDTH, IDX_HEADS * IDX_DIM, IDX_DIM, IDX_HEADS,
             B_WIDTH, C_HEADS * C_KDIM, C_HEADS * C_KDIM, C_WIDTH, C_WIDTH)

LANES = 128
SUBLANES = 8
INT_MIN = -(2 ** 31)
VMEM_LIMIT = 56 * 1024 * 1024

KIWI_W = LANES
REST_W = B_WIDTH + 4 * C_WIDTH
PACK_W = A_WIDTH + 2 * KV_WIDTH + IDX_HEADS * IDX_DIM + KIWI_W + REST_W
HGRN_LEVELS = (32, 16, 8, 4, 2, 1)


def _nt(a, b):
    return lax.dot_general(a, b, (((1,), (1,)), ((), ())), preferred_element_type=F32)


def _tn(a, b):
    return lax.dot_general(a, b, (((0,), (0,)), ((), ())), preferred_element_type=F32)


def _mm(a, b):
    return jnp.dot(a, b, preferred_element_type=F32)


def _cparams(sem):
    return pltpu.CompilerParams(dimension_semantics=sem, vmem_limit_bytes=VMEM_LIMIT)


def _ada_kernel(c_ref, w_ref, b_ref, o_ref):
    c = c_ref[...]
    ca = c * jax.nn.sigmoid(c)
    o_ref[...] = jnp.dot(ca, w_ref[...], precision=lax.Precision.HIGHEST,
                         preferred_element_type=F32) + b_ref[...]


def _ada_call(c, w_ada, b_ada):
    depth, d, d6 = w_ada.shape
    b = c.shape[0]
    nblk = d6 // d
    return pl.pallas_call(
        _ada_kernel,
        out_shape=jax.ShapeDtypeStruct((depth, b, d6), F32),
        grid=(depth, nblk),
        in_specs=[pl.BlockSpec((b, d), lambda l, j: (0, 0)),
                  pl.BlockSpec((None, d, d), lambda l, j: (l, 0, j)),
                  pl.BlockSpec((None, 1, d), lambda l, j: (l, 0, j))],
        out_specs=pl.BlockSpec((None, b, d), lambda l, j: (l, 0, j)),
        compiler_params=_cparams(("arbitrary", "arbitrary")),
        name="ada_mod",
    )(c, w_ada, b_ada.reshape(depth, 1, d6))


def _rope_table_kernel(pos_ref, inv_ref, sign_ref, cos_ref, sin_ref):
    ang = pos_ref[...].astype(F32) * inv_ref[...]
    cos_ref[...] = jnp.cos(ang)
    sin_ref[...] = jnp.sin(ang) * sign_ref[...]


def _rope_table_call(positions, ts=512):
    b, s = positions.shape
    half = HEAD_DIM // 2
    inv = jnp.power(jnp.float32(ROPE_THETA), -jnp.arange(0, HEAD_DIM, 2, dtype=F32) / HEAD_DIM)
    inv128 = jnp.tile(inv, LANES // half).reshape(1, LANES)
    sign128 = jnp.tile(jnp.concatenate([-jnp.ones((half,), F32), jnp.ones((half,), F32)]),
                       LANES // HEAD_DIM).reshape(1, LANES)
    spec = pl.BlockSpec((None, ts, LANES), lambda bi, i: (bi, i, 0))
    return pl.pallas_call(
        _rope_table_kernel,
        out_shape=(jax.ShapeDtypeStruct((b, s, LANES), F32),) * 2,
        grid=(b, s // ts),
        in_specs=[pl.BlockSpec((None, ts, 1), lambda bi, i: (bi, i, 0)),
                  pl.BlockSpec((1, LANES), lambda bi, i: (0, 0)),
                  pl.BlockSpec((1, LANES), lambda bi, i: (0, 0))],
        out_specs=(spec, spec),
        compiler_params=_cparams(("arbitrary", "arbitrary")),
        name="rope_table",
    )(positions.reshape(b, s, 1), inv128, sign128)


def _rope_tile(x, cos, sin_signed, first_half):
    partner = jnp.where(first_half, pltpu.roll(x, LANES - HEAD_DIM // 2, 1),
                        pltpu.roll(x, HEAD_DIM // 2, 1))
    return x * cos + partner * sin_signed


def _ada_norm(x, g, sc, sh):
    y = x * lax.rsqrt(jnp.mean(x * x, axis=-1, keepdims=True) + EPS)
    return (y * g) * (1.0 + sc) + sh


VT_W = 256
INPROJ_TILE = 512
WI_ROWS = SUBLANES


def _inproj_kernel(h_ref, sh_ref, sc_ref, g_ref, w_ref, cos_ref, sin_ref,
                   qt_ref, qit_ref, wit_ref, k_ref, ki_ref, vt_ref, rest_ref):
    u = _ada_norm(h_ref[...], g_ref[...], sc_ref[...], sh_ref[...]).astype(BF16)
    cos = cos_ref[...]
    sin = sin_ref[...]
    lane = lax.broadcasted_iota(I32, cos.shape, 1)
    first_half = (lane % HEAD_DIM) < (HEAD_DIM // 2)
    rope = lambda x: _rope_tile(x, cos, sin, first_half)

    off = 0
    q_scale = HEAD_DIM ** -0.5
    for j in range(A_WIDTH // LANES):
        z = _mm(u, w_ref[:, off + j * LANES: off + (j + 1) * LANES])
        qt_ref[j * LANES:(j + 1) * LANES, :] = (rope(z) * q_scale).T.astype(BF16)
    off += A_WIDTH
    k_ref[...] = rope(_mm(u, w_ref[:, off:off + KV_WIDTH])).astype(BF16)
    off += KV_WIDTH
    vz = _mm(u, w_ref[:, off:off + KV_WIDTH])
    for j in range(vt_ref.shape[0]):
        vt_ref[j] = vz[j * VT_W:(j + 1) * VT_W].T.astype(BF16)
    off += KV_WIDTH
    for j in range(IDX_HEADS * IDX_DIM // LANES):
        z = _mm(u, w_ref[:, off + j * LANES: off + (j + 1) * LANES])
        qit_ref[j * LANES:(j + 1) * LANES, :] = rope(z).T.astype(BF16)
    off += IDX_HEADS * IDX_DIM
    z = _mm(u, w_ref[:, off:off + KIWI_W])
    ki_ref[...] = rope(z).astype(BF16)
    wi_scale = (IDX_HEADS * IDX_DIM) ** -0.5
    wit_ref[...] = (z * wi_scale).T[IDX_DIM:IDX_DIM + WI_ROWS, :]
    off += KIWI_W
    rest_ref[...] = _mm(u, w_ref[:, off:off + REST_W])


def _inproj_call(h, sh, sc, g, w_pack, cos, sin):
    b, s, d = h.shape
    tm = min(INPROJ_TILE, s)
    tok = lambda w: pl.BlockSpec((None, tm, w), lambda bi, i: (bi, i, 0))
    tr = lambda r: pl.BlockSpec((None, r, tm), lambda bi, i: (bi, 0, i))
    per_b = pl.BlockSpec((None, 1, d), lambda bi, i: (bi, 0, 0))
    sds = jax.ShapeDtypeStruct
    return pl.pallas_call(
        _inproj_kernel,
        out_shape=(sds((b, A_WIDTH, s), BF16), sds((b, IDX_HEADS * IDX_DIM, s), BF16),
                   sds((b, WI_ROWS, s), F32), sds((b, s, KV_WIDTH), BF16),
                   sds((b, s, KIWI_W), BF16), sds((b, s // VT_W, KV_WIDTH, VT_W), BF16),
                   sds((b, s, REST_W), F32)),
        grid=(b, s // tm),
        in_specs=[tok(d), per_b, per_b,
                  pl.BlockSpec((1, d), lambda bi, i: (0, 0)),
                  pl.BlockSpec((d, PACK_W), lambda bi, i: (0, 0)),
                  tok(LANES), tok(LANES)],
        out_specs=(tr(A_WIDTH), tr(IDX_HEADS * IDX_DIM), tr(WI_ROWS), tok(KV_WIDTH), tok(KIWI_W),
                   pl.BlockSpec((None, tm // VT_W, KV_WIDTH, VT_W), lambda bi, i: (bi, i, 0, 0)),
                   tok(REST_W)),
        compiler_params=_cparams(("arbitrary", "arbitrary")),
        name="inproj",
    )(h, sh, sc, g, w_pack, cos, sin)


CNT_ROWS = 32
CNT_ACCS = 4
SEARCH_HEAD_BITS = 16
ONES_ROWS = 16
TIE_BLOCK = 128
CHUNKS_PER_TRIP = 2


def _dsa_kernel(qt_ref, qit_ref, wit_ref, k_ref, ki_ref, vt_ref, lower_ref, o_ref,
                keys_ref, bias_ref, s_ref, acc_ref, *, kc, topk):
    qb = Q_BLOCK
    i = pl.program_id(1)
    t0 = i * qb
    nkc = (t0 + qb + kc - 1) // kc
    lane = lax.broadcasted_iota(I32, (1, qb), 1)
    limit = t0 + (lane // CHUNK + 1) * CHUNK
    key_off = lax.broadcasted_iota(I32, (kc, qb), 0)
    group = A_HEADS // A_KV_HEADS

    qit = qit_ref[...]
    qi_stack = jnp.concatenate(
        [qit[h * IDX_DIM:(h + 1) * IDX_DIM, :] for h in range(IDX_HEADS)], axis=1)
    wit = wit_ref[...]

    def score_body(c, carry):
        off = pl.multiple_of(c * kc, kc)
        ki = ki_ref[pl.ds(off, kc), 0:IDX_DIM]
        s = jnp.maximum(_mm(ki, qi_stack), 0.0)
        score = s[:, 0:qb] * wit[0:1, :]
        for h in range(1, IDX_HEADS):
            score = score + s[:, h * qb:(h + 1) * qb] * wit[h:h + 1, :]
        score = jnp.where(score == 0.0, 0.0, score)
        bits = lax.bitcast_convert_type(score, I32)
        key = jnp.where(bits < 0, bits ^ jnp.int32(0x7FFFFFFF), bits)
        keys_ref[pl.ds(off, kc), :] = jnp.where(off + key_off < limit, key, jnp.int32(INT_MIN))
        return carry

    n_trips = nkc // CHUNKS_PER_TRIP
    left = nkc - n_trips * CHUNKS_PER_TRIP

    def for_chunks(body, carry):
        def trip(t, cr):
            for j in range(CHUNKS_PER_TRIP):
                cr = body(t * CHUNKS_PER_TRIP + j, cr)
            return cr
        carry = lax.fori_loop(0, n_trips, trip, carry)
        return lax.cond(left > 0, lambda cr: body(nkc - 1, cr), lambda cr: cr, carry)

    for_chunks(score_body, jnp.int32(0))

    def count(pred):
        def body(c, accs):
            off = pl.multiple_of(c * kc, kc)
            blk = keys_ref[pl.ds(off, kc), :]
            accs = list(accs)
            for j in range(kc // CNT_ROWS):
                a = accs[j % CNT_ACCS]
                accs[j % CNT_ACCS] = jnp.where(pred(blk[j * CNT_ROWS:(j + 1) * CNT_ROWS]), a + 1.0, a)
            return tuple(accs)
        accs = lax.fori_loop(0, nkc, body,
                             tuple(jnp.zeros((CNT_ROWS, qb), F32) for _ in range(CNT_ACCS)))
        return jnp.sum((accs[0] + accs[1]) + (accs[2] + accs[3]), axis=0, keepdims=True)

    topk_f = float(topk)
    cnt0 = count(lambda kk: kk >= 0)
    cnt1 = count(lambda kk: kk >= 1)
    nonneg = cnt0 >= topk_f
    thr0 = jnp.where(nonneg, 0, INT_MIN).astype(I32)
    cthr0 = jnp.where(nonneg, cnt0, 2.0 * kc * (nkc + 1).astype(F32))
    done0 = jnp.where((nonneg & (cnt1 < topk_f)) | (cthr0 == topk_f) | (limit <= topk), 1.0, 0.0)

    def try_bit(state, b):
        thr, cthr, done = state
        cand = thr + jnp.left_shift(jnp.int32(1), b)
        cnt = count(lambda kk: kk >= cand)
        take = (cnt >= topk_f) & (done < 0.5)
        thr = jnp.where(take, cand, thr)
        cthr = jnp.where(take, cnt, cthr)
        return thr, cthr, jnp.where(cthr == topk_f, 1.0, done)

    state = lax.fori_loop(0, SEARCH_HEAD_BITS, lambda j, st: try_bit(st, 30 - j),
                          (thr0, cthr0, done0))
    steps = 3
    tail_bits = 31 - SEARCH_HEAD_BITS

    def thr_cond(carry):
        g, st = carry
        return (g * steps < tail_bits) & (jnp.min(st[2]) < 0.5)

    def thr_body(carry):
        g, st = carry
        for jj in range(steps):
            st = try_bit(st, tail_bits - 1 - (g * steps + jj))
        return g + 1, st

    _, (thr, _, _) = lax.while_loop(thr_cond, thr_body, (jnp.int32(0), state))

    need = topk_f - count(lambda kk: kk > thr)
    lower = lower_ref[...]
    tie_off = lax.broadcasted_iota(I32, (TIE_BLOCK, qb), 0)

    def bias_body(c, seen):
        offs = [pl.multiple_of(c * kc + j * TIE_BLOCK, TIE_BLOCK) for j in range(kc // TIE_BLOCK)]
        kks = [keys_ref[pl.ds(off, TIE_BLOCK), :] for off in offs]
        ties = [jnp.where(kk == thr, 1.0, 0.0) for kk in kks]
        ranks = [_mm(lower, t.astype(BF16)) for t in ties]
        for off, kk, tie_f, rank in zip(offs, kks, ties, ranks):
            sel = ((kk > thr) | ((kk == thr) & (rank + seen <= need))) & (off + tie_off < limit)
            bias_ref[pl.ds(off, TIE_BLOCK), :] = jnp.where(sel, 0.0, NEG_BIG)
            seen = seen + jnp.sum(tie_f, axis=0, keepdims=True)
        return seen

    for_chunks(bias_body, jnp.zeros((1, qb), F32))

    qt = qt_ref[...]
    q_n = [jnp.concatenate([qt[(n * group + g) * HEAD_DIM:(n * group + g + 1) * HEAD_DIM, :]
                            for g in range(group)], axis=1) for n in range(A_KV_HEADS)]

    per = kc // VT_W

    def max_chunk(c, parts):
        out = list(parts)
        for j in range(per):
            cv = c * per + j
            off = pl.multiple_of(cv * VT_W, VT_W)
            bias = bias_ref[pl.ds(off, VT_W), :]
            for n in range(A_KV_HEADS):
                s = _mm(k_ref[pl.ds(off, VT_W), n * HEAD_DIM:(n + 1) * HEAD_DIM], q_n[n])
                s = jnp.concatenate([s[:, g * qb:(g + 1) * qb] + bias for g in range(group)], axis=1)
                s_ref[cv, n] = s
                out[n] = jnp.maximum(out[n], jnp.max(
                    s.reshape(VT_W // SUBLANES, SUBLANES, group * qb), axis=0))
        return tuple(out)

    parts = for_chunks(max_chunk, tuple(jnp.full((SUBLANES, group * qb), NEG_BIG, F32)
                                        for _ in range(A_KV_HEADS)))
    m_n = [jnp.max(p, axis=0, keepdims=True) for p in parts]

    acc_ref[...] = jnp.zeros(acc_ref.shape, F32)
    ones = jnp.ones((ONES_ROWS, VT_W), BF16)

    def pv_chunk(c, carry):
        for j in range(per):
            cv = c * per + j
            for n in range(A_KV_HEADS):
                pt = jnp.exp(s_ref[cv, n] - m_n[n]).astype(BF16)
                vt = jnp.concatenate([vt_ref[cv, n * HEAD_DIM:(n + 1) * HEAD_DIM, :], ones], axis=0)
                acc_ref[n] += _mm(vt, pt)
        return carry

    for_chunks(pv_chunk, jnp.int32(0))

    for n in range(A_KV_HEADS):
        for g in range(group):
            hh = n * group + g
            cols = slice(g * qb, (g + 1) * qb)
            o_ref[hh * HEAD_DIM:(hh + 1) * HEAD_DIM, :] = (
                acc_ref[n, 0:HEAD_DIM, cols] / acc_ref[n, HEAD_DIM:HEAD_DIM + 1, cols])


def _dsa_call(qt, qit, wit, k, ki, vt, kc=512):
    b, _, s = qt.shape
    kc = min(kc, s)
    assert kc % VT_W == 0 and s % kc == 0
    topk = min(IDX_TOPK_MAX, s // 4)
    group = A_HEADS // A_KV_HEADS
    lower = jnp.asarray(np.tril(np.ones((TIE_BLOCK, TIE_BLOCK), np.float32)), BF16)
    qblk = lambda r: pl.BlockSpec((None, r, Q_BLOCK), lambda bi, i: (bi, 0, i))
    seq = lambda w: pl.BlockSpec((None, s, w), lambda bi, i: (bi, 0, 0))
    return pl.pallas_call(
        functools.partial(_dsa_kernel, kc=kc, topk=topk),
        out_shape=jax.ShapeDtypeStruct((b, A_WIDTH, s), F32),
        grid=(b, s // Q_BLOCK),
        in_specs=[qblk(A_WIDTH), qblk(IDX_HEADS * IDX_DIM), qblk(WI_ROWS),
                  seq(KV_WIDTH), seq(KIWI_W),
                  pl.BlockSpec((None, s // VT_W, KV_WIDTH, VT_W), lambda bi, i: (bi, 0, 0, 0)),
                  pl.BlockSpec((TIE_BLOCK, TIE_BLOCK), lambda bi, i: (0, 0))],
        out_specs=qblk(A_WIDTH),
        scratch_shapes=[pltpu.VMEM((s, Q_BLOCK), I32),
                        pltpu.VMEM((s, Q_BLOCK), F32),
                        pltpu.VMEM((s // VT_W, A_KV_HEADS, VT_W, group * Q_BLOCK), F32),
                        pltpu.VMEM((A_KV_HEADS, HEAD_DIM + ONES_ROWS, group * Q_BLOCK), F32)],
        compiler_params=_cparams(("arbitrary", "arbitrary")),
        name="dsa",
    )(qt, qit, wit, k, ki, vt, lower)


POOL_HALO = 32


def _pool_kernel(u_ref, w_ref, scale_ref, o_ref, x_buf, a_buf, b_buf, *, tm):
    hl = POOL_HALO
    rows = tm + hl
    first = pl.program_id(1) == 0

    @pl.when(first)
    def _():
        x_buf[0:hl, :] = jnp.zeros((hl, B_WIDTH), F32)

    x = u_ref[...]
    x_buf[hl:rows, :] = x
    a_buf[8:rows, :] = x_buf[8:rows, :] + x_buf[7:rows - 1, :]
    b_buf[16:rows, :] = a_buf[16:rows, :] + a_buf[14:rows - 2, :]
    w2 = a_buf[hl:rows, :]
    w4 = b_buf[hl:rows, :]
    a_buf[24:rows, :] = b_buf[24:rows, :] + b_buf[20:rows - 4, :]
    w8 = a_buf[hl:rows, :]
    b_buf[hl:rows, :] = a_buf[hl:rows, :] + a_buf[hl - 8:rows - 8, :]
    w16 = b_buf[hl:rows, :]
    x_buf[0:hl, :] = x_buf[tm:rows, :]

    lane = lax.broadcasted_iota(I32, (tm, B_WIDTH), 1)
    grp = lane // POOL_GROUP
    wsum = jnp.where(grp == 0, w2, jnp.where(grp == 1, w4, jnp.where(grp == 2, w8, w16)))
    win = jnp.where(grp == 0, 2, jnp.where(grp == 1, 4, jnp.where(grp == 2, 8, 16)))
    t = pl.program_id(1) * tm + lax.broadcasted_iota(I32, (tm, B_WIDTH), 0)
    cnt = jnp.minimum(t + 1, win).astype(F32)
    pooled = wsum / cnt - x
    y = _mm(pooled.astype(BF16), w_ref[...])
    o_ref[...] = y * scale_ref[...]


def _pool_call(rest, w_bd, scale, tm=512):
    b, s, _ = rest.shape
    tm = min(tm, s)
    rows = tm + POOL_HALO
    return pl.pallas_call(
        functools.partial(_pool_kernel, tm=tm),
        out_shape=jax.ShapeDtypeStruct((b, s, B_WIDTH), F32),
        grid=(b, s // tm),
        in_specs=[pl.BlockSpec((None, tm, B_WIDTH), lambda bi, i: (bi, i, 0)),
                  pl.BlockSpec((B_WIDTH, B_WIDTH), lambda bi, i: (0, 0)),
                  pl.BlockSpec((1, B_WIDTH), lambda bi, i: (0, 0))],
        out_specs=pl.BlockSpec((None, tm, B_WIDTH), lambda bi, i: (bi, i, 0)),
        scratch_shapes=[pltpu.VMEM((rows, B_WIDTH), F32)] * 3,
        compiler_params=_cparams(("arbitrary", "arbitrary")),
        name="pool",
    )(rest, w_bd, scale)


def _hgrn_consts():
    tril = np.tril(np.ones((CHUNK, CHUNK), np.float32))
    mats = [tril]
    r = np.arange(CHUNK)
    for h in HGRN_LEVELS:
        mats.append(tril[(r // (2 * h)) * (2 * h) + h - 1])
    return np.concatenate(mats, axis=0)


def _split3(x):
    hi = x.astype(BF16)
    r1 = x - hi.astype(F32)
    mid = r1.astype(BF16)
    lo = (r1 - mid.astype(F32)).astype(BF16)
    return hi, mid, lo


def _hgrn_kernel(q_ref, f_ref, i_ref, g_ref, lb_ref, ng_ref, cm_ref, bd_ref, o_ref, state_ref,
                 *, layer, tm):
    @pl.when(pl.program_id(1) == 0)
    def _():
        state_ref[...] = jnp.zeros(state_ref.shape, F32)

    lbl = lb_ref[...]
    e = jnp.exp(lbl - jnp.max(lbl, axis=0, keepdims=True))
    p = e / jnp.sum(e, axis=0, keepdims=True)
    cum = p[0:1]
    for l in range(1, layer + 1):
        cum = cum + p[l:l + 1]
    lb = jnp.clip(cum - p[0:1], 0.0, 1.0)

    cm = cm_ref[...]
    ng = ng_ref[...]
    bd = bd_ref[...]
    bd_f = bd.astype(F32)
    row = lax.broadcasted_iota(I32, (CHUNK, 1), 0)
    tt = lax.broadcasted_iota(I32, (CHUNK, C_WIDTH), 0)
    ss = lax.broadcasted_iota(I32, (CHUNK, C_WIDTH), 1) % CHUNK
    lvl_mask = [tt == ss] + [(tt // (2 * h)) == (ss // (2 * h)) for h in HGRN_LEVELS]
    w = C_WIDTH
    heads = C_WIDTH // C_KDIM

    def expand(x16):
        return jnp.concatenate([x16] * heads, axis=0) * bd

    chunks = range(tm // CHUNK)
    rows = [slice(ci * CHUNK, (ci + 1) * CHUNK) for ci in chunks]
    z = f_ref[...]
    log_f = jnp.log(lb + (1.0 - lb) * jax.nn.sigmoid(z))
    kin = (1.0 - lb) * jax.nn.sigmoid(-z)
    qx = q_ref[...]
    qv = qx * jax.nn.sigmoid(qx)
    vb16 = i_ref[...].astype(BF16)
    hi, mid, lo = _split3(log_f)
    lf3 = jnp.concatenate([hi, mid, lo], axis=1)
    cs = [_mm(cm, lf3[r]) for r in rows]
    cs = [c[:, 0:w] + c[:, w:2 * w] + c[:, 2 * w:3 * w] for c in cs]
    odd = [((row // h) % 2) == 1 for h in HGRN_LEVELS]
    attn, q_dec, upd, s_dec = [], [], [], []
    for ci in chunks:
        r = rows[ci]
        bcum = cs[ci][0:CHUNK]
        b_last = bcum[CHUNK - 1:CHUNK]
        q_c, k_c = qv[r], kin[r]
        q_dec.append((q_c * jnp.exp(bcum)).astype(BF16))
        s_dec.append(jnp.exp(b_last))
        upd.append(_tn(vb16[r], (k_c * jnp.exp(b_last - bcum)).astype(BF16)) * bd_f)
        qs, ks = [q_c.astype(BF16)], [k_c.astype(BF16)]
        for li in range(len(HGRN_LEVELS)):
            ref = cs[ci][(li + 1) * CHUNK:(li + 2) * CHUNK]
            qs.append((q_c * jnp.exp(jnp.where(odd[li], bcum - ref, NEG_BIG))).astype(BF16))
            ks.append((k_c * jnp.exp(jnp.where(odd[li], NEG_BIG, ref - bcum))).astype(BF16))
        a = jnp.zeros((CHUNK, w), F32)
        for mask, ql, kl in zip(lvl_mask, qs, ks):
            a = a + jnp.where(mask, _nt(ql, expand(kl)), 0.0)
        attn.append(a)
    o_intra = [_mm(attn[ci].astype(BF16), expand(vb16[rows[ci]])) for ci in chunks]
    st = state_ref[...]
    outs = []
    for ci in chunks:
        outs.append(_nt(q_dec[ci], st.astype(BF16)) + o_intra[ci])
        st = st * s_dec[ci] + upd[ci]
    state_ref[...] = st
    o = jnp.concatenate(outs, axis=0)
    o2h, o2m, o2l = _split3(o * o)
    ms = (_mm(o2h, bd) + _mm(o2m, bd) + _mm(o2l, bd)) * (1.0 / C_VDIM)
    gx = g_ref[...]
    o_ref[...] = (o * lax.rsqrt(ms + EPS) * ng) * (gx * jax.nn.sigmoid(gx))


def _hgrn_call(rest, lb_logits, norm_g, layer, tm=256):
    b, s, _ = rest.shape
    depth = lb_logits.shape[0]
    cm = jnp.asarray(_hgrn_consts(), BF16)
    head_of = np.arange(C_WIDTH) // C_KDIM
    bd = jnp.asarray(head_of[:, None] == head_of[None, :], BF16)
    col = lambda j: pl.BlockSpec((None, tm, C_WIDTH), lambda bi, i, j=j: (bi, i, j))
    return pl.pallas_call(
        functools.partial(_hgrn_kernel, layer=layer, tm=tm),
        out_shape=jax.ShapeDtypeStruct((b, s, C_WIDTH), F32),
        grid=(b, s // tm),
        in_specs=[col(1), col(2), col(3), col(4),
                  pl.BlockSpec((depth, C_WIDTH), lambda bi, i: (0, 0)),
                  pl.BlockSpec((1, C_WIDTH), lambda bi, i: (0, 0)),
                  pl.BlockSpec(cm.shape, lambda bi, i: (0, 0)),
                  pl.BlockSpec((C_WIDTH, C_WIDTH), lambda bi, i: (0, 0))],
        out_specs=pl.BlockSpec((None, tm, C_WIDTH), lambda bi, i: (bi, i, 0)),
        scratch_shapes=[pltpu.VMEM((C_WIDTH, C_WIDTH), F32)],
        compiler_params=_cparams(("arbitrary", "arbitrary")),
        name="hgrn2",
    )(rest, rest, rest, rest, lb_logits, jnp.tile(norm_g, C_HEADS).reshape(1, C_WIDTH), cm, bd)


RANK_BITS = 20
CODE_ROWS = 8


def _outproj_router_kernel(ya_ref, yb_ref, yc_ref, h_ref, g1_ref, wo_ref, sh_ref, sc_ref, g_ref,
                           wr_ref, br_ref, tri_ref, h1_ref, u_ref, code_ref, gk_ref, cnt_ref,
                           run_ref):
    @pl.when((pl.program_id(0) == 0) & (pl.program_id(1) == 0))
    def _():
        run_ref[...] = jnp.zeros(run_ref.shape, F32)

    y = _tn(ya_ref[...].astype(BF16), wo_ref[0:A_WIDTH, :])
    y = y + _mm(yb_ref[...].astype(BF16), wo_ref[A_WIDTH:A_WIDTH + B_WIDTH, :])
    y = y + _mm(yc_ref[...].astype(BF16), wo_ref[A_WIDTH + B_WIDTH:, :])
    h1 = h_ref[...] + g1_ref[...] * y
    h1_ref[...] = h1
    u = _ada_norm(h1, g_ref[...], sc_ref[...], sh_ref[...])
    u_ref[...] = u

    u_hi = u.astype(BF16)
    u_lo = (u - u_hi.astype(F32)).astype(BF16)
    wr = wr_ref[...]
    wr_hi = wr.astype(BF16)
    wr_lo = (wr - wr_hi.astype(F32)).astype(BF16)
    logits = _mm(u_hi, wr_hi) + (_mm(u_hi, wr_lo) + _mm(u_lo, wr_hi)) + br_ref[...]
    lane = lax.broadcasted_iota(I32, logits.shape, 1).astype(F32)
    work = jnp.where(lane < N_EXPERTS, logits, -jnp.inf)
    picks, firsts, tops = [], [], []
    for k in range(TOP_K):
        m = jnp.max(work, axis=1, keepdims=True)
        first = jnp.min(jnp.where(work == m, lane, float(LANES)), axis=1, keepdims=True)
        pick = lane == first
        work = jnp.where(pick, -jnp.inf, work)
        picks.append(pick)
        firsts.append(first)
        tops.append(m)
    ex = [jnp.exp(m - tops[0]) for m in tops]
    den = ex[0] + ex[1] + ex[2] + ex[3]

    sel = picks[0] | picks[1] | picks[2] | picks[3]
    sel_f = jnp.where(sel, 1.0, 0.0)
    prefix = _mm(tri_ref[...], sel_f.astype(BF16)) + run_ref[...]
    run_ref[...] += jnp.sum(sel_f, axis=0, keepdims=True)
    cnt_ref[...] = run_ref[...]

    code = jnp.zeros(logits.shape, I32)
    gk = jnp.zeros(logits.shape, F32)
    for k in range(TOP_K):
        rank = jnp.sum(jnp.where(picks[k], prefix, 0.0), axis=1, keepdims=True)
        ck = (firsts[k].astype(I32) << RANK_BITS) | rank.astype(I32)
        code = jnp.where(lane == float(k), ck, code)
        gk = jnp.where(lane == float(k), ex[k] / den, gk)
    code_ref[...] = code.T[0:CODE_ROWS, :]
    gk_ref[...] = gk


def _outproj_router_call(ya, yb, yc, h, g1, w_out, sh, sc, g, w_router, b_router, tm=512):
    b, s, d = h.shape
    tm = min(tm, s)
    tri = jnp.asarray(np.tril(np.ones((tm, tm), np.float32), -1), BF16)
    tok = lambda w: pl.BlockSpec((None, tm, w), lambda bi, i: (bi, i, 0))
    per_b = pl.BlockSpec((None, 1, d), lambda bi, i: (bi, 0, 0))
    full = lambda a: pl.BlockSpec(a.shape, lambda bi, i: (0,) * a.ndim)
    return pl.pallas_call(
        _outproj_router_kernel,
        out_shape=(jax.ShapeDtypeStruct((b, s, d), F32), jax.ShapeDtypeStruct((b, s, d), F32),
                   jax.ShapeDtypeStruct((CODE_ROWS, b * s), I32),
                   jax.ShapeDtypeStruct((b, s, LANES), F32), jax.ShapeDtypeStruct((1, LANES), F32)),
        grid=(b, s // tm),
        in_specs=[pl.BlockSpec((None, A_WIDTH, tm), lambda bi, i: (bi, 0, i)),
                  tok(B_WIDTH), tok(C_WIDTH), tok(d), per_b, full(w_out),
                  per_b, per_b, full(g), full(w_router), full(b_router), full(tri)],
        out_specs=(tok(d), tok(d),
                   pl.BlockSpec((CODE_ROWS, tm), lambda bi, i: (0, bi * (s // tm) + i)),
                   tok(LANES), pl.BlockSpec((1, LANES), lambda bi, i: (0, 0))),
        scratch_shapes=[pltpu.VMEM((1, LANES), F32)],
        compiler_params=_cparams(("arbitrary", "arbitrary")),
        name="outproj_router",
    )(ya, yb, yc, h, g1, w_out, sh, sc, g, w_router, b_router, tri)


MOE_BLOCK = 512
PERM_W = 2 * LANES


ROWS_TILE = 2048


def _rows_kernel(start_ref, code_ref, row_ref):
    code = code_ref[...]
    expert = code >> RANK_BITS
    base = jnp.zeros(code.shape, I32)
    for e in range(N_EXPERTS):
        base = jnp.where(expert == e, start_ref[e], base)
    row_ref[...] = base + (code & ((1 << RANK_BITS) - 1))


def _rows_call(starts, code):
    r, n = code.shape
    tile = min(ROWS_TILE, n)
    return pl.pallas_call(
        _rows_kernel,
        out_shape=jax.ShapeDtypeStruct((r, n), I32),
        grid_spec=pltpu.PrefetchScalarGridSpec(
            num_scalar_prefetch=1, grid=(n // tile,),
            in_specs=[pl.BlockSpec((r, tile), lambda i, *_: (0, i))],
            out_specs=pl.BlockSpec((r, tile), lambda i, *_: (0, i))),
        compiler_params=_cparams(("arbitrary",)),
        name="moe_rows",
    )(starts, code)


def _drain_rows(src_row, dst_row, sem, n):
    def body(t, c):
        pltpu.make_async_copy(src_row, dst_row, sem).wait()
        return c
    lax.fori_loop(0, n, body, 0, unroll=4)


ZERO_ROWS = MOE_BLOCK // 2


def _dispatch_kernel(row_ref, padlo_ref, padhi_ref, x_ref, xs_ref, zbuf, sem, zsem, *, tm):
    tok0 = pl.program_id(0) * tm
    n_tok = row_ref.shape[0] // TOP_K

    @pl.when(pl.program_id(0) == 0)
    def _():
        zbuf[...] = jnp.zeros(zbuf.shape, F32)

        def pad_copies(e, fn):
            lo = padlo_ref[e]
            hi = padhi_ref[e]
            lo8 = jnp.minimum((lo + SUBLANES - 1) // SUBLANES * SUBLANES, hi)
            for j in range(SUBLANES - 1):

                @pl.when(lo + j < lo8)
                def _(j=j):
                    fn(pltpu.make_async_copy(zbuf.at[pl.ds(0, 1)], xs_ref.at[pl.ds(lo + j, 1)], zsem))
            n8 = hi - lo8
            size = ZERO_ROWS
            while size >= SUBLANES:
                off = pl.multiple_of(lo8 + (n8 & ~(2 * size - 1)), SUBLANES)

                @pl.when((n8 & size) != 0)
                def _(size=size, off=off):
                    fn(pltpu.make_async_copy(zbuf.at[pl.ds(0, size)], xs_ref.at[pl.ds(off, size)],
                                             zsem))
                size //= 2

        def tail_copies(fn):
            tail_lo = padhi_ref[N_EXPERTS - 1]

            def piece(j, c):
                off = pl.multiple_of(tail_lo + j * ZERO_ROWS, ZERO_ROWS)

                @pl.when(off < xs_ref.shape[0])
                def _():
                    fn(pltpu.make_async_copy(zbuf, xs_ref.at[pl.ds(off, ZERO_ROWS)], zsem))
                return c

            lax.fori_loop(0, N_EXPERTS * MOE_BLOCK // ZERO_ROWS, piece, 0)

        def start_e(e, c):
            pad_copies(e, lambda cp: cp.start())
            return c

        def wait_e(e, c):
            pad_copies(e, lambda cp: cp.wait())
            return c

        lax.fori_loop(0, N_EXPERTS, start_e, 0)
        tail_copies(lambda cp: cp.start())
        lax.fori_loop(0, N_EXPERTS, wait_e, 0)
        tail_copies(lambda cp: cp.wait())

    def issue(t8, c):
        base = pl.multiple_of(t8 * SUBLANES, SUBLANES)
        for j in range(SUBLANES):
            for k in range(TOP_K):
                dst = row_ref[k * n_tok + tok0 + base + j]
                pltpu.make_async_copy(x_ref.at[pl.ds(base + j, 1)], xs_ref.at[pl.ds(dst, 1)],
                                      sem).start(priority=k % 2)
        return c

    lax.fori_loop(0, tm // SUBLANES, issue, 0)
    _drain_rows(x_ref.at[pl.ds(0, 1)], xs_ref.at[pl.ds(0, 1)], sem, tm * TOP_K)


def _dispatch_call(row_ids, pad_lo, pad_hi, x, rows, tm=512):
    n, d = x.shape
    return pl.pallas_call(
        functools.partial(_dispatch_kernel, tm=tm),
        out_shape=jax.ShapeDtypeStruct((rows, d), F32),
        grid_spec=pltpu.PrefetchScalarGridSpec(
            num_scalar_prefetch=3, grid=(n // tm,),
            in_specs=[pl.BlockSpec((tm, d), lambda i, *_: (i, 0))],
            out_specs=pl.BlockSpec(memory_space=pl.ANY),
            scratch_shapes=[pltpu.VMEM((ZERO_ROWS, d), F32), pltpu.SemaphoreType.DMA,
                            pltpu.SemaphoreType.DMA]),
        compiler_params=_cparams(("arbitrary",)),
        name="moe_dispatch",
    )(row_ids, pad_lo, pad_hi, x)


def _ffn_kernel(blk_e_ref, nb_ref, xs_ref, w1_ref, b1g_ref, b1l_ref, w2_ref, b2_ref, perm_ref,
                ys_ref, w1g_s, w1l_s, w2_s):
    i = pl.program_id(0)
    e = blk_e_ref[i]
    live = i < nb_ref[0]
    fresh = (i == 0) | (e != blk_e_ref[jnp.maximum(i - 1, 0)])

    @pl.when(live & fresh)
    def _():
        perm = perm_ref[...]
        for j in range(w1_ref.shape[1] // PERM_W):
            t = _mm(w1_ref[:, j * PERM_W:(j + 1) * PERM_W].astype(BF16), perm).astype(BF16)
            w1g_s[:, j * LANES:(j + 1) * LANES] = t[:, 0:LANES]
            w1l_s[:, j * LANES:(j + 1) * LANES] = t[:, LANES:PERM_W]
        w2_s[...] = w2_ref[...].astype(BF16)

    @pl.when(live)
    def _():
        x = xs_ref[...].astype(BF16)
        glu = jnp.minimum(_mm(x, w1g_s[...]) + b1g_ref[...], SWIGLU_LIMIT)
        lin = jnp.clip(_mm(x, w1l_s[...]) + b1l_ref[...], -SWIGLU_LIMIT, SWIGLU_LIMIT)
        act = glu * jax.nn.sigmoid(SWIGLU_ALPHA * glu) * (lin + 1.0)
        ys_ref[...] = _mm(act.astype(BF16), w2_s[...]) + b2_ref[...]

    @pl.when(jnp.logical_not(live))
    def _():
        ys_ref[...] = jnp.zeros(ys_ref.shape, F32)


def _ffn_call(blk_e, nb_used, xs, w1, b1g, b1l, w2, b2, layer):
    rows, d = xs.shape
    _, ne, _, ff2 = w1.shape
    ff = ff2 // 2
    perm = np.zeros((PERM_W, PERM_W), np.float32)
    perm[2 * np.arange(LANES), np.arange(LANES)] = 1.0
    perm[2 * np.arange(LANES) + 1, LANES + np.arange(LANES)] = 1.0
    ex = lambda r, c: pl.BlockSpec((None, r, c), lambda i, be, nb: (be[i], 0, 0))
    exl = lambda r, c: pl.BlockSpec((None, None, r, c), lambda i, be, nb: (layer, be[i], 0, 0))
    return pl.pallas_call(
        _ffn_kernel,
        out_shape=jax.ShapeDtypeStruct((rows, d), F32),
        grid_spec=pltpu.PrefetchScalarGridSpec(
            num_scalar_prefetch=2, grid=(rows // MOE_BLOCK,),
            in_specs=[pl.BlockSpec((MOE_BLOCK, d),
                                   lambda i, be, nb: (jnp.minimum(i, jnp.maximum(nb[0] - 1, 0)), 0)),
                      exl(d, ff2), ex(1, ff), ex(1, ff), exl(ff, d), ex(1, d),
                      pl.BlockSpec((PERM_W, PERM_W), lambda i, *_: (0, 0))],
            out_specs=pl.BlockSpec((MOE_BLOCK, d), lambda i, *_: (i, 0)),
            scratch_shapes=[pltpu.VMEM((d, ff), BF16), pltpu.VMEM((d, ff), BF16),
                            pltpu.VMEM((ff, d), BF16)]),
        compiler_params=_cparams(("arbitrary",)),
        name="moe_ffn",
    )(blk_e, nb_used, xs, w1, b1g, b1l, w2, b2, jnp.asarray(perm, BF16))


def _combine_kernel(row_ref, gk_ref, h_ref, g2_ref, fg_ref, ys_ref, o_ref, buf, sem,
                    *, tm, final_norm):
    tok0 = pl.program_id(0) * tm
    n_tok = row_ref.shape[0] // TOP_K

    def issue(t8, c):
        base = pl.multiple_of(t8 * SUBLANES, SUBLANES)
        for j in range(SUBLANES):
            for k in range(TOP_K):
                src = row_ref[k * n_tok + tok0 + base + j]
                pltpu.make_async_copy(ys_ref.at[pl.ds(src, 1)], buf.at[k, pl.ds(base + j, 1)],
                                      sem).start(priority=k % 2)
        return c

    lax.fori_loop(0, tm // SUBLANES, issue, 0)
    _drain_rows(ys_ref.at[pl.ds(0, 1)], buf.at[0, pl.ds(0, 1)], sem, tm * TOP_K)

    gk = gk_ref[...]
    acc = buf[0] * gk[:, 0:1]
    for k in range(1, TOP_K):
        acc = acc + buf[k] * gk[:, k:k + 1]
    out = h_ref[...] + g2_ref[...] * acc
    if final_norm:
        out = out * lax.rsqrt(jnp.mean(out * out, axis=-1, keepdims=True) + EPS) * fg_ref[...]
    o_ref[...] = out


def _combine_call(row_ids, gk, h1, g2, final_g, ys, seq, final_norm, tm=256):
    n, d = h1.shape
    return pl.pallas_call(
        functools.partial(_combine_kernel, tm=tm, final_norm=final_norm),
        out_shape=jax.ShapeDtypeStruct((n, d), F32),
        grid_spec=pltpu.PrefetchScalarGridSpec(
            num_scalar_prefetch=1, grid=(n // tm,),
            in_specs=[pl.BlockSpec((tm, LANES), lambda i, *_: (i, 0)),
                      pl.BlockSpec((tm, d), lambda i, *_: (i, 0)),
                      pl.BlockSpec((None, 1, d), lambda i, *_: ((i * tm) // seq, 0, 0)),
                      pl.BlockSpec((1, d), lambda i, *_: (0, 0)),
                      pl.BlockSpec(memory_space=pl.ANY)],
            out_specs=pl.BlockSpec((tm, d), lambda i, *_: (i, 0)),
            scratch_shapes=[pltpu.VMEM((TOP_K, tm, d), F32), pltpu.SemaphoreType.DMA]),
        compiler_params=_cparams(("arbitrary",)),
        name="moe_combine",
    )(row_ids, gk, h1, g2, final_g, ys)


def _moe_call(u, code, gk, counts, h1, g2, w1, b1, w2, b2, final_g, layer, final_norm):
    b, s, d = h1.shape
    n = b * s
    ne = w1.shape[1]
    rows = n * TOP_K + ne * MOE_BLOCK
    cnt = counts[0, :ne].astype(I32)
    padded = (cnt + MOE_BLOCK - 1) // MOE_BLOCK * MOE_BLOCK
    ends = jnp.cumsum(padded)
    starts = (ends - padded).astype(I32)
    blk_start = jnp.arange(rows // MOE_BLOCK, dtype=I32) * MOE_BLOCK
    blk_e = jnp.minimum(jnp.sum((ends[None, :] <= blk_start[:, None]).astype(I32), axis=1), ne - 1)
    nb_used = (ends[-1:] // MOE_BLOCK).astype(I32)

    row_ids = _rows_call(starts, code)[0:TOP_K].reshape(TOP_K * n)

    xs = _dispatch_call(row_ids, starts + cnt, ends.astype(I32), u.reshape(n, d), rows)
    ff = w1.shape[3] // 2
    ys = _ffn_call(blk_e, nb_used, xs, w1, b1[:, 0::2].reshape(ne, 1, ff),
                   b1[:, 1::2].reshape(ne, 1, ff), w2, b2.reshape(ne, 1, d), layer)
    out = _combine_call(row_ids, gk.reshape(n, LANES), h1.reshape(n, d), g2, final_g, ys, s,
                        final_norm)
    return out.reshape(b, s, d)


def _pack_w_in(w_in):
    d = w_in.shape[0]
    offs = np.cumsum((0,) + IN_SPLITS)
    head = w_in[:, :offs[5]]
    wi = w_in[:, offs[5]:offs[6]]
    pad = jnp.zeros((d, KIWI_W - IDX_DIM - IDX_HEADS), w_in.dtype)
    return jnp.concatenate([head, wi, pad, w_in[:, offs[6]:]], axis=1).astype(BF16)


def _block_diag(pool_w):
    g, c, _ = pool_w.shape
    out = jnp.zeros((g * c, g * c), pool_w.dtype)
    for j in range(g):
        out = out.at[j * c:(j + 1) * c, j * c:(j + 1) * c].set(pool_w[j])
    return out.astype(BF16)


def kernel(x, c, positions, w_ada, b_ada, norm1_g, norm2_g, w_in, w_out, pool_w, pool_scale,
           hgrn_norm_g, lb_logits, w_router, b_router, w1, b1, w2, b2, final_g):
    bsz, s, d = x.shape
    depth = w_ada.shape[0]
    mod = _ada_call(c, w_ada, b_ada)
    cos, sin = _rope_table_call(positions)
    h = x
    for l in range(depth):
        sh1, sc1, g1, sh2, sc2, g2 = [mod[l, :, j * d:(j + 1) * d].reshape(bsz, 1, d)
                                      for j in range(6)]
        qt, qit, wit, k, ki, vt, rest = _inproj_call(
            h, sh1, sc1, norm1_g[l].reshape(1, d), _pack_w_in(w_in[l]), cos, sin)
        ya = _dsa_call(qt, qit, wit, k, ki, vt)
        yb = _pool_call(rest, _block_diag(pool_w[l]), pool_scale[l].reshape(1, B_WIDTH))
        yc = _hgrn_call(rest, lb_logits, hgrn_norm_g[l], l)
        wr = jnp.pad(w_router[l], ((0, 0), (0, LANES - N_EXPERTS)))
        br = jnp.pad(b_router[l], (0, LANES - N_EXPERTS)).reshape(1, LANES)
        h1, u2, code, gk, counts = _outproj_router_call(
            ya, yb, yc, h, g1, w_out[l].astype(BF16), sh2, sc2, norm2_g[l].reshape(1, d), wr, br)
        h = _moe_call(u2, code, gk, counts, h1, g2, w1, b1[l], w2, b2[l],
                      final_g.reshape(1, d), l, final_norm=(l == depth - 1))
    return h
```

```python
import functools

import numpy as np
import jax
import jax.numpy as jnp
from jax import lax
from jax.experimental import pallas as pl
from jax.experimental.pallas import tpu as pltpu

F32 = jnp.float32
BF16 = jnp.bfloat16
I32 = jnp.int32

CHUNK = 64
EPS = 1e-6
NEG_BIG = -1e30
A_HEADS, A_KV_HEADS, HEAD_DIM = 8, 2, 64
IDX_HEADS, IDX_DIM, IDX_TOPK_MAX = 4, 64, 256
Q_BLOCK = 128
ROPE_THETA = 10000.0
A_WIDTH = A_HEADS * HEAD_DIM
POOL_WINDOWS = (2, 4, 8, 16)
POOL_GROUP = 64
B_WIDTH = len(POOL_WINDOWS) * POOL_GROUP
C_HEADS, C_KDIM, C_VDIM = 4, 64, 64
C_WIDTH = C_HEADS * C_VDIM
N_EXPERTS, TOP_K = 32, 4
SWIGLU_LIMIT, SWIGLU_ALPHA = 7.0, 1.702
KV_WIDTH = A_KV_HEADS * HEAD_DIM
IN_SPLITS = (A_WIDTH, KV_WIDTH, KV_WIDTH, IDX_HEADS * IDX_DIM, IDX_DIM, IDX_HEADS,
             B_WIDTH, C_HEADS * C_KDIM, C_HEADS * C_KDIM, C_WIDTH, C_WIDTH)

LANES = 128
SUBLANES = 8
INT_MIN = -(2 ** 31)
VMEM_LIMIT = 56 * 1024 * 1024

KIWI_W = LANES
REST_W = B_WIDTH + 4 * C_WIDTH
PACK_W = A_WIDTH + 2 * KV_WIDTH + IDX_HEADS * IDX_DIM + KIWI_W + REST_W
HGRN_LEVELS = (32, 16, 8, 4, 2, 1)


def _nt(a, b):
    return lax.dot_general(a, b, (((1,), (1,)), ((), ())), preferred_element_type=F32)


def _tn(a, b):
    return lax.dot_general(a, b, (((0,), (0,)), ((), ())), preferred_element_type=F32)


def _mm(a, b):
    return jnp.dot(a, b, preferred_element_type=F32)


def _cparams(sem):
    return pltpu.CompilerParams(dimension_semantics=sem, vmem_limit_bytes=VMEM_LIMIT)


def _ada_kernel(c_ref, w_ref, b_ref, o_ref):
    c = c_ref[...]
    ca = c * jax.nn.sigmoid(c)
    o_ref[...] = jnp.dot(ca, w_ref[...], precision=lax.Precision.HIGHEST,
                         preferred_element_type=F32) + b_ref[...]


def _ada_call(c, w_ada, b_ada):
    depth, d, d6 = w_ada.shape
    b = c.shape[0]
    nblk = d6 // d
    return pl.pallas_call(
        _ada_kernel,
        out_shape=jax.ShapeDtypeStruct((depth, b, d6), F32),
        grid=(depth, nblk),
        in_specs=[pl.BlockSpec((b, d), lambda l, j: (0, 0)),
                  pl.BlockSpec((None, d, d), lambda l, j: (l, 0, j)),
                  pl.BlockSpec((None, 1, d), lambda l, j: (l, 0, j))],
        out_specs=pl.BlockSpec((None, b, d), lambda l, j: (l, 0, j)),
        compiler_params=_cparams(("arbitrary", "arbitrary")),
        name="ada_mod",
    )(c, w_ada, b_ada.reshape(depth, 1, d6))


def _rope_table_kernel(pos_ref, inv_ref, sign_ref, cos_ref, sin_ref):
    ang = pos_ref[...].astype(F32) * inv_ref[...]
    cos_ref[...] = jnp.cos(ang)
    sin_ref[...] = jnp.sin(ang) * sign_ref[...]


def _rope_table_call(positions, ts=512):
    b, s = positions.shape
    half = HEAD_DIM // 2
    inv = jnp.power(jnp.float32(ROPE_THETA), -jnp.arange(0, HEAD_DIM, 2, dtype=F32) / HEAD_DIM)
    inv128 = jnp.tile(inv, LANES // half).reshape(1, LANES)
    sign128 = jnp.tile(jnp.concatenate([-jnp.ones((half,), F32), jnp.ones((half,), F32)]),
                       LANES // HEAD_DIM).reshape(1, LANES)
    spec = pl.BlockSpec((None, ts, LANES), lambda bi, i: (bi, i, 0))
    return pl.pallas_call(
        _rope_table_kernel,
        out_shape=(jax.ShapeDtypeStruct((b, s, LANES), F32),) * 2,
        grid=(b, s // ts),
        in_specs=[pl.BlockSpec((None, ts, 1), lambda bi, i: (bi, i, 0)),
                  pl.BlockSpec((1, LANES), lambda bi, i: (0, 0)),
                  pl.BlockSpec((1, LANES), lambda bi, i: (0, 0))],
        out_specs=(spec, spec),
        compiler_params=_cparams(("arbitrary", "arbitrary")),
        name="rope_table",
    )(positions.reshape(b, s, 1), inv128, sign128)


def _rope_tile(x, cos, sin_signed, first_half):
    partner = jnp.where(first_half, pltpu.roll(x, LANES - HEAD_DIM // 2, 1),
                        pltpu.roll(x, HEAD_DIM // 2, 1))
    return x * cos + partner * sin_signed


def _ada_norm(x, g, sc, sh):
    y = x * lax.rsqrt(jnp.mean(x * x, axis=-1, keepdims=True) + EPS)
    return (y * g) * (1.0 + sc) + sh


VT_W = 256
INPROJ_TILE = 512
WI_ROWS = SUBLANES


def _inproj_kernel(h_ref, sh_ref, sc_ref, g_ref, w_ref, cos_ref, sin_ref,
                   qt_ref, qit_ref, wit_ref, k_ref, ki_ref, vt_ref, rest_ref):
    u = _ada_norm(h_ref[...], g_ref[...], sc_ref[...], sh_ref[...]).astype(BF16)
    cos = cos_ref[...]
    sin = sin_ref[...]
    lane = lax.broadcasted_iota(I32, cos.shape, 1)
    first_half = (lane % HEAD_DIM) < (HEAD_DIM // 2)
    rope = lambda x: _rope_tile(x, cos, sin, first_half)

    off = 0
    q_scale = HEAD_DIM ** -0.5
    for j in range(A_WIDTH // LANES):
        z = _mm(u, w_ref[:, off + j * LANES: off + (j + 1) * LANES])
        qt_ref[j * LANES:(j + 1) * LANES, :] = (rope(z) * q_scale).T.astype(BF16)
    off += A_WIDTH
    k_ref[...] = rope(_mm(u, w_ref[:, off:off + KV_WIDTH])).astype(BF16)
    off += KV_WIDTH
    vz = _mm(u, w_ref[:, off:off + KV_WIDTH])
    for j in range(vt_ref.shape[0]):
        vt_ref[j] = vz[j * VT_W:(j + 1) * VT_W].T.astype(BF16)
    off += KV_WIDTH
    for j in range(IDX_HEADS * IDX_DIM // LANES):
        z = _mm(u, w_ref[:, off + j * LANES: off + (j + 1) * LANES])
        qit_ref[j * LANES:(j + 1) * LANES, :] = rope(z).T.astype(BF16)
    off += IDX_HEADS * IDX_DIM
    z = _mm(u, w_ref[:, off:off + KIWI_W])
    ki_ref[...] = rope(z).astype(BF16)
    wi_scale = (IDX_HEADS * IDX_DIM) ** -0.5
    wit_ref[...] = (z * wi_scale).T[IDX_DIM:IDX_DIM + WI_ROWS, :]
    off += KIWI_W
    rest_ref[...] = _mm(u, w_ref[:, off:off + REST_W])


def _inproj_call(h, sh, sc, g, w_pack, cos, sin):
    b, s, d = h.shape
    tm = min(INPROJ_TILE, s)
    tok = lambda w: pl.BlockSpec((None, tm, w), lambda bi, i: (bi, i, 0))
    tr = lambda r: pl.BlockSpec((None, r, tm), lambda bi, i: (bi, 0, i))
    per_b = pl.BlockSpec((None, 1, d), lambda bi, i: (bi, 0, 0))
    sds = jax.ShapeDtypeStruct
    return pl.pallas_call(
        _inproj_kernel,
        out_shape=(sds((b, A_WIDTH, s), BF16), sds((b, IDX_HEADS * IDX_DIM, s), BF16),
                   sds((b, WI_ROWS, s), F32), sds((b, s, KV_WIDTH), BF16),
                   sds((b, s, KIWI_W), BF16), sds((b, s // VT_W, KV_WIDTH, VT_W), BF16),
                   sds((b, s, REST_W), F32)),
        grid=(b, s // tm),
        in_specs=[tok(d), per_b, per_b,
                  pl.BlockSpec((1, d), lambda bi, i: (0, 0)),
                  pl.BlockSpec((d, PACK_W), lambda bi, i: (0, 0)),
                  tok(LANES), tok(LANES)],
        out_specs=(tr(A_WIDTH), tr(IDX_HEADS * IDX_DIM), tr(WI_ROWS), tok(KV_WIDTH), tok(KIWI_W),
                   pl.BlockSpec((None, tm // VT_W, KV_WIDTH, VT_W), lambda bi, i: (bi, i, 0, 0)),
                   tok(REST_W)),
        compiler_params=_cparams(("arbitrary", "arbitrary")),
        name="inproj",
    )(h, sh, sc, g, w_pack, cos, sin)


CNT_ROWS = 32
CNT_ACCS = 4
SEARCH_HEAD_BITS = 19
ONES_ROWS = 16
TIE_BLOCK = 128
CHUNKS_PER_TRIP = 2


def _dsa_kernel(qt_ref, qit_ref, wit_ref, k_ref, ki_ref, vt_ref, lower_ref, o_ref,
                keys_ref, bias_ref, s_ref, acc_ref, *, kc, topk):
    qb = Q_BLOCK
    i = pl.program_id(1)
    t0 = i * qb
    nkc = (t0 + qb + kc - 1) // kc
    lane = lax.broadcasted_iota(I32, (1, qb), 1)
    limit = t0 + (lane // CHUNK + 1) * CHUNK
    key_off = lax.broadcasted_iota(I32, (kc, qb), 0)
    group = A_HEADS // A_KV_HEADS

    qit = qit_ref[...]
    qi_stack = jnp.concatenate(
        [qit[h * IDX_DIM:(h + 1) * IDX_DIM, :] for h in range(IDX_HEADS)], axis=1)
    wit = wit_ref[...]

    def score_body(c, carry):
        off = pl.multiple_of(c * kc, kc)
        ki = ki_ref[pl.ds(off, kc), 0:IDX_DIM]
        s = jnp.maximum(_mm(ki, qi_stack), 0.0)
        score = s[:, 0:qb] * wit[0:1, :]
        for h in range(1, IDX_HEADS):
            score = score + s[:, h * qb:(h + 1) * qb] * wit[h:h + 1, :]
        score = jnp.where(score == 0.0, 0.0, score)
        bits = lax.bitcast_convert_type(score, I32)
        key = jnp.where(bits < 0, bits ^ jnp.int32(0x7FFFFFFF), bits)
        keys_ref[pl.ds(off, kc), :] = jnp.where(off + key_off < limit, key, jnp.int32(INT_MIN))
        return carry

    n_trips = nkc // CHUNKS_PER_TRIP
    left = nkc - n_trips * CHUNKS_PER_TRIP

    def for_chunks(body, carry):
        def trip(t, cr):
            for j in range(CHUNKS_PER_TRIP):
                cr = body(t * CHUNKS_PER_TRIP + j, cr)
            return cr
        carry = lax.fori_loop(0, n_trips, trip, carry)
        return lax.cond(left > 0, lambda cr: body(nkc - 1, cr), lambda cr: cr, carry)

    for_chunks(score_body, jnp.int32(0))

    def count(pred):
        def body(c, accs):
            off = pl.multiple_of(c * kc, kc)
            blk = keys_ref[pl.ds(off, kc), :]
            accs = list(accs)
            for j in range(kc // CNT_ROWS):
                a = accs[j % CNT_ACCS]
                accs[j % CNT_ACCS] = jnp.where(pred(blk[j * CNT_ROWS:(j + 1) * CNT_ROWS]), a + 1.0, a)
            return tuple(accs)
        accs = lax.fori_loop(0, nkc, body,
                             tuple(jnp.zeros((CNT_ROWS, qb), F32) for _ in range(CNT_ACCS)))
        return jnp.sum((accs[0] + accs[1]) + (accs[2] + accs[3]), axis=0, keepdims=True)

    topk_f = float(topk)
    cnt0 = count(lambda kk: kk >= 0)
    cnt1 = count(lambda kk: kk >= 1)
    nonneg = cnt0 >= topk_f
    thr0 = jnp.where(nonneg, 0, INT_MIN).astype(I32)
    cthr0 = jnp.where(nonneg, cnt0, 2.0 * kc * (nkc + 1).astype(F32))
    done0 = jnp.where((nonneg & (cnt1 < topk_f)) | (cthr0 == topk_f) | (limit <= topk), 1.0, 0.0)

    def try_bit(state, b):
        thr, cthr, done = state
        cand = thr + jnp.left_shift(jnp.int32(1), b)
        cnt = count(lambda kk: kk >= cand)
        take = (cnt >= topk_f) & (done < 0.5)
        thr = jnp.where(take, cand, thr)
        cthr = jnp.where(take, cnt, cthr)
        return thr, cthr, jnp.where(cthr == topk_f, 1.0, done)

    state = lax.fori_loop(0, SEARCH_HEAD_BITS, lambda j, st: try_bit(st, 30 - j),
                          (thr0, cthr0, done0))
    steps = 3
    tail_bits = 31 - SEARCH_HEAD_BITS

    def thr_cond(carry):
        g, st = carry
        return (g * steps < tail_bits) & (jnp.min(st[2]) < 0.5)

    def thr_body(carry):
        g, st = carry
        for jj in range(steps):
            st = try_bit(st, tail_bits - 1 - (g * steps + jj))
        return g + 1, st

    _, (thr, _, _) = lax.while_loop(thr_cond, thr_body, (jnp.int32(0), state))

    need = topk_f - count(lambda kk: kk > thr)
    lower = lower_ref[...]
    tie_off = lax.broadcasted_iota(I32, (TIE_BLOCK, qb), 0)

    def bias_body(c, seen):
        offs = [pl.multiple_of(c * kc + j * TIE_BLOCK, TIE_BLOCK) for j in range(kc // TIE_BLOCK)]
        kks = [keys_ref[pl.ds(off, TIE_BLOCK), :] for off in offs]
        ties = [jnp.where(kk == thr, 1.0, 0.0) for kk in kks]
        ranks = [_mm(lower, t.astype(BF16)) for t in ties]
        for off, kk, tie_f, rank in zip(offs, kks, ties, ranks):
            sel = ((kk > thr) | ((kk == thr) & (rank + seen <= need))) & (off + tie_off < limit)
            bias_ref[pl.ds(off, TIE_BLOCK), :] = jnp.where(sel, 0.0, NEG_BIG)
            seen = seen + jnp.sum(tie_f, axis=0, keepdims=True)
        return seen

    for_chunks(bias_body, jnp.zeros((1, qb), F32))

    qt = qt_ref[...]
    q_n = [jnp.concatenate([qt[(n * group + g) * HEAD_DIM:(n * group + g + 1) * HEAD_DIM, :]
                            for g in range(group)], axis=1) for n in range(A_KV_HEADS)]

    per = kc // VT_W

    def max_chunk(c, parts):
        out = list(parts)
        for j in range(per):
            cv = c * per + j
            off = pl.multiple_of(cv * VT_W, VT_W)
            bias = bias_ref[pl.ds(off, VT_W), :]
            for n in range(A_KV_HEADS):
                s = _mm(k_ref[pl.ds(off, VT_W), n * HEAD_DIM:(n + 1) * HEAD_DIM], q_n[n])
                s = jnp.concatenate([s[:, g * qb:(g + 1) * qb] + bias for g in range(group)], axis=1)
                s_ref[cv, n] = s
                out[n] = jnp.maximum(out[n], jnp.max(
                    s.reshape(VT_W // SUBLANES, SUBLANES, group * qb), axis=0))
        return tuple(out)

    parts = for_chunks(max_chunk, tuple(jnp.full((SUBLANES, group * qb), NEG_BIG, F32)
                                        for _ in range(A_KV_HEADS)))
    m_n = [jnp.max(p, axis=0, keepdims=True) for p in parts]

    acc_ref[...] = jnp.zeros(acc_ref.shape, F32)
    ones = jnp.ones((ONES_ROWS, VT_W), BF16)

    def pv_chunk(c, carry):
        for j in range(per):
            cv = c * per + j
            for n in range(A_KV_HEADS):
                pt = jnp.exp(s_ref[cv, n] - m_n[n]).astype(BF16)
                vt = jnp.concatenate([vt_ref[cv, n * HEAD_DIM:(n + 1) * HEAD_DIM, :], ones], axis=0)
                acc_ref[n] += _mm(vt, pt)
        return carry

    for_chunks(pv_chunk, jnp.int32(0))

    for n in range(A_KV_HEADS):
        for g in range(group):
            hh = n * group + g
            cols = slice(g * qb, (g + 1) * qb)
            o_ref[hh * HEAD_DIM:(hh + 1) * HEAD_DIM, :] = (
                acc_ref[n, 0:HEAD_DIM, cols] / acc_ref[n, HEAD_DIM:HEAD_DIM + 1, cols])


def _dsa_call(qt, qit, wit, k, ki, vt, kc=512):
    b, _, s = qt.shape
    kc = min(kc, s)
    assert kc % VT_W == 0 and s % kc == 0
    topk = min(IDX_TOPK_MAX, s // 4)
    group = A_HEADS // A_KV_HEADS
    lower = jnp.asarray(np.tril(np.ones((TIE_BLOCK, TIE_BLOCK), np.float32)), BF16)
    qblk = lambda r: pl.BlockSpec((None, r, Q_BLOCK), lambda bi, i: (bi, 0, i))
    seq = lambda w: pl.BlockSpec((None, s, w), lambda bi, i: (bi, 0, 0))
    return pl.pallas_call(
        functools.partial(_dsa_kernel, kc=kc, topk=topk),
        out_shape=jax.ShapeDtypeStruct((b, A_WIDTH, s), F32),
        grid=(b, s // Q_BLOCK),
        in_specs=[qblk(A_WIDTH), qblk(IDX_HEADS * IDX_DIM), qblk(WI_ROWS),
                  seq(KV_WIDTH), seq(KIWI_W),
                  pl.BlockSpec((None, s // VT_W, KV_WIDTH, VT_W), lambda bi, i: (bi, 0, 0, 0)),
                  pl.BlockSpec((TIE_BLOCK, TIE_BLOCK), lambda bi, i: (0, 0))],
        out_specs=qblk(A_WIDTH),
        scratch_shapes=[pltpu.VMEM((s, Q_BLOCK), I32),
                        pltpu.VMEM((s, Q_BLOCK), F32),
                        pltpu.VMEM((s // VT_W, A_KV_HEADS, VT_W, group * Q_BLOCK), F32),
                        pltpu.VMEM((A_KV_HEADS, HEAD_DIM + ONES_ROWS, group * Q_BLOCK), F32)],
        compiler_params=_cparams(("arbitrary", "arbitrary")),
        name="dsa",
    )(qt, qit, wit, k, ki, vt, lower)


POOL_HALO = 32


def _pool_kernel(u_ref, w_ref, scale_ref, o_ref, x_buf, a_buf, b_buf, *, tm):
    hl = POOL_HALO
    rows = tm + hl
    first = pl.program_id(1) == 0

    @pl.when(first)
    def _():
        x_buf[0:hl, :] = jnp.zeros((hl, B_WIDTH), F32)

    x = u_ref[...]
    x_buf[hl:rows, :] = x
    a_buf[8:rows, :] = x_buf[8:rows, :] + x_buf[7:rows - 1, :]
    b_buf[16:rows, :] = a_buf[16:rows, :] + a_buf[14:rows - 2, :]
    w2 = a_buf[hl:rows, :]
    w4 = b_buf[hl:rows, :]
    a_buf[24:rows, :] = b_buf[24:rows, :] + b_buf[20:rows - 4, :]
    w8 = a_buf[hl:rows, :]
    b_buf[hl:rows, :] = a_buf[hl:rows, :] + a_buf[hl - 8:rows - 8, :]
    w16 = b_buf[hl:rows, :]
    x_buf[0:hl, :] = x_buf[tm:rows, :]

    lane = lax.broadcasted_iota(I32, (tm, B_WIDTH), 1)
    grp = lane // POOL_GROUP
    wsum = jnp.where(grp == 0, w2, jnp.where(grp == 1, w4, jnp.where(grp == 2, w8, w16)))
    win = jnp.where(grp == 0, 2, jnp.where(grp == 1, 4, jnp.where(grp == 2, 8, 16)))
    t = pl.program_id(1) * tm + lax.broadcasted_iota(I32, (tm, B_WIDTH), 0)
    cnt = jnp.minimum(t + 1, win).astype(F32)
    pooled = wsum / cnt - x
    y = _mm(pooled.astype(BF16), w_ref[...])
    o_ref[...] = y * scale_ref[...]


def _pool_call(rest, w_bd, scale, tm=512):
    b, s, _ = rest.shape
    tm = min(tm, s)
    rows = tm + POOL_HALO
    return pl.pallas_call(
        functools.partial(_pool_kernel, tm=tm),
        out_shape=jax.ShapeDtypeStruct((b, s, B_WIDTH), F32),
        grid=(b, s // tm),
        in_specs=[pl.BlockSpec((None, tm, B_WIDTH), lambda bi, i: (bi, i, 0)),
                  pl.BlockSpec((B_WIDTH, B_WIDTH), lambda bi, i: (0, 0)),
                  pl.BlockSpec((1, B_WIDTH), lambda bi, i: (0, 0))],
        out_specs=pl.BlockSpec((None, tm, B_WIDTH), lambda bi, i: (bi, i, 0)),
        scratch_shapes=[pltpu.VMEM((rows, B_WIDTH), F32)] * 3,
        compiler_params=_cparams(("arbitrary", "arbitrary")),
        name="pool",
    )(rest, w_bd, scale)


def _hgrn_consts():
    tril = np.tril(np.ones((CHUNK, CHUNK), np.float32))
    mats = [tril]
    r = np.arange(CHUNK)
    for h in HGRN_LEVELS:
        mats.append(tril[(r // (2 * h)) * (2 * h) + h - 1])
    return np.concatenate(mats, axis=0)


def _split3(x):
    hi = x.astype(BF16)
    r1 = x - hi.astype(F32)
    mid = r1.astype(BF16)
    lo = (r1 - mid.astype(F32)).astype(BF16)
    return hi, mid, lo


def _hgrn_kernel(q_ref, f_ref, i_ref, g_ref, lb_ref, ng_ref, cm_ref, bd_ref, o_ref, state_ref,
                 *, layer, tm):
    @pl.when(pl.program_id(1) == 0)
    def _():
        state_ref[...] = jnp.zeros(state_ref.shape, F32)

    lbl = lb_ref[...]
    e = jnp.exp(lbl - jnp.max(lbl, axis=0, keepdims=True))
    p = e / jnp.sum(e, axis=0, keepdims=True)
    cum = p[0:1]
    for l in range(1, layer + 1):
        cum = cum + p[l:l + 1]
    lb = jnp.clip(cum - p[0:1], 0.0, 1.0)

    cm = cm_ref[...]
    ng = ng_ref[...]
    bd = bd_ref[...]
    bd_f = bd.astype(F32)
    row = lax.broadcasted_iota(I32, (CHUNK, 1), 0)
    tt = lax.broadcasted_iota(I32, (CHUNK, C_WIDTH), 0)
    ss = lax.broadcasted_iota(I32, (CHUNK, C_WIDTH), 1) % CHUNK
    lvl_mask = [tt == ss] + [(tt // (2 * h)) == (ss // (2 * h)) for h in HGRN_LEVELS]
    w = C_WIDTH
    heads = C_WIDTH // C_KDIM

    def expand(x16):
        return jnp.concatenate([x16] * heads, axis=0) * bd

    chunks = range(tm // CHUNK)
    rows = [slice(ci * CHUNK, (ci + 1) * CHUNK) for ci in chunks]
    z = f_ref[...]
    log_f = jnp.log(lb + (1.0 - lb) * jax.nn.sigmoid(z))
    kin = (1.0 - lb) * jax.nn.sigmoid(-z)
    qx = q_ref[...]
    qv = qx * jax.nn.sigmoid(qx)
    vb16 = i_ref[...].astype(BF16)
    hi, mid, lo = _split3(log_f)
    lf3 = jnp.concatenate([hi, mid, lo], axis=1)
    cs = [_mm(cm, lf3[r]) for r in rows]
    cs = [c[:, 0:w] + c[:, w:2 * w] + c[:, 2 * w:3 * w] for c in cs]
    odd = [((row // h) % 2) == 1 for h in HGRN_LEVELS]
    attn, q_dec, upd, s_dec = [], [], [], []
    for ci in chunks:
        r = rows[ci]
        bcum = cs[ci][0:CHUNK]
        b_last = bcum[CHUNK - 1:CHUNK]
        q_c, k_c = qv[r], kin[r]
        q_dec.append((q_c * jnp.exp(bcum)).astype(BF16))
        s_dec.append(jnp.exp(b_last))
        upd.append(_tn(vb16[r], (k_c * jnp.exp(b_last - bcum)).astype(BF16)) * bd_f)
        qs, ks = [q_c.astype(BF16)], [k_c.astype(BF16)]
        for li in range(len(HGRN_LEVELS)):
            ref = cs[ci][(li + 1) * CHUNK:(li + 2) * CHUNK]
            qs.append((q_c * jnp.exp(jnp.where(odd[li], bcum - ref, NEG_BIG))).astype(BF16))
            ks.append((k_c * jnp.exp(jnp.where(odd[li], NEG_BIG, ref - bcum))).astype(BF16))
        a = jnp.zeros((CHUNK, w), F32)
        for mask, ql, kl in zip(lvl_mask, qs, ks):
            a = a + jnp.where(mask, _nt(ql, expand(kl)), 0.0)
        attn.append(a)
    o_intra = [_mm(attn[ci].astype(BF16), expand(vb16[rows[ci]])) for ci in chunks]
    st = state_ref[...]
    outs = []
    for ci in chunks:
        outs.append(_nt(q_dec[ci], st.astype(BF16)) + o_intra[ci])
        st = st * s_dec[ci] + upd[ci]
    state_ref[...] = st
    o = jnp.concatenate(outs, axis=0)
    o2h, o2m, o2l = _split3(o * o)
    ms = (_mm(o2h, bd) + _mm(o2m, bd) + _mm(o2l, bd)) * (1.0 / C_VDIM)
    gx = g_ref[...]
    o_ref[...] = (o * lax.rsqrt(ms + EPS) * ng) * (gx * jax.nn.sigmoid(gx))


def _hgrn_call(rest, lb_logits, norm_g, layer, tm=512):
    b, s, _ = rest.shape
    depth = lb_logits.shape[0]
    cm = jnp.asarray(_hgrn_consts(), BF16)
    head_of = np.arange(C_WIDTH) // C_KDIM
    bd = jnp.asarray(head_of[:, None] == head_of[None, :], BF16)
    col = lambda j: pl.BlockSpec((None, tm, C_WIDTH), lambda bi, i, j=j: (bi, i, j))
    return pl.pallas_call(
        functools.partial(_hgrn_kernel, layer=layer, tm=tm),
        out_shape=jax.ShapeDtypeStruct((b, s, C_WIDTH), F32),
        grid=(b, s // tm),
        in_specs=[col(1), col(2), col(3), col(4),
                  pl.BlockSpec((depth, C_WIDTH), lambda bi, i: (0, 0)),
                  pl.BlockSpec((1, C_WIDTH), lambda bi, i: (0, 0)),
                  pl.BlockSpec(cm.shape, lambda bi, i: (0, 0)),
                  pl.BlockSpec((C_WIDTH, C_WIDTH), lambda bi, i: (0, 0))],
        out_specs=pl.BlockSpec((None, tm, C_WIDTH), lambda bi, i: (bi, i, 0)),
        scratch_shapes=[pltpu.VMEM((C_WIDTH, C_WIDTH), F32)],
        compiler_params=_cparams(("arbitrary", "arbitrary")),
        name="hgrn2",
    )(rest, rest, rest, rest, lb_logits, jnp.tile(norm_g, C_HEADS).reshape(1, C_WIDTH), cm, bd)


RANK_BITS = 20
CODE_ROWS = 8


def _outproj_router_kernel(ya_ref, yb_ref, yc_ref, h_ref, g1_ref, wo_ref, sh_ref, sc_ref, g_ref,
                           wr_ref, br_ref, tri_ref, h1_ref, u_ref, code_ref, gk_ref, cnt_ref,
                           run_ref):
    @pl.when((pl.program_id(0) == 0) & (pl.program_id(1) == 0))
    def _():
        run_ref[...] = jnp.zeros(run_ref.shape, F32)

    y = _tn(ya_ref[...].astype(BF16), wo_ref[0:A_WIDTH, :])
    y = y + _mm(yb_ref[...].astype(BF16), wo_ref[A_WIDTH:A_WIDTH + B_WIDTH, :])
    y = y + _mm(yc_ref[...].astype(BF16), wo_ref[A_WIDTH + B_WIDTH:, :])
    h1 = h_ref[...] + g1_ref[...] * y
    h1_ref[...] = h1
    u = _ada_norm(h1, g_ref[...], sc_ref[...], sh_ref[...])
    u_ref[...] = u

    u_hi = u.astype(BF16)
    u_lo = (u - u_hi.astype(F32)).astype(BF16)
    wr = wr_ref[...]
    wr_hi = wr.astype(BF16)
    wr_lo = (wr - wr_hi.astype(F32)).astype(BF16)
    logits = _mm(u_hi, wr_hi) + (_mm(u_hi, wr_lo) + _mm(u_lo, wr_hi)) + br_ref[...]
    lane = lax.broadcasted_iota(I32, logits.shape, 1).astype(F32)
    work = jnp.where(lane < N_EXPERTS, logits, -jnp.inf)
    picks, firsts, tops = [], [], []
    for k in range(TOP_K):
        m = jnp.max(work, axis=1, keepdims=True)
        first = jnp.min(jnp.where(work == m, lane, float(LANES)), axis=1, keepdims=True)
        pick = lane == first
        work = jnp.where(pick, -jnp.inf, work)
        picks.append(pick)
        firsts.append(first)
        tops.append(m)
    ex = [jnp.exp(m - tops[0]) for m in tops]
    den = ex[0] + ex[1] + ex[2] + ex[3]

    sel = picks[0] | picks[1] | picks[2] | picks[3]
    sel_f = jnp.where(sel, 1.0, 0.0)
    prefix = _mm(tri_ref[...], sel_f.astype(BF16)) + run_ref[...]
    run_ref[...] += jnp.sum(sel_f, axis=0, keepdims=True)
    cnt_ref[...] = run_ref[...]

    code = jnp.zeros(logits.shape, I32)
    gk = jnp.zeros(logits.shape, F32)
    for k in range(TOP_K):
        rank = jnp.sum(jnp.where(picks[k], prefix, 0.0), axis=1, keepdims=True)
        ck = (firsts[k].astype(I32) << RANK_BITS) | rank.astype(I32)
        code = jnp.where(lane == float(k), ck, code)
        gk = jnp.where(lane == float(k), ex[k] / den, gk)
    code_ref[...] = code.T[0:CODE_ROWS, :]
    gk_ref[...] = gk


def _outproj_router_call(ya, yb, yc, h, g1, w_out, sh, sc, g, w_router, b_router, tm=512):
    b, s, d = h.shape
    tm = min(tm, s)
    tri = jnp.asarray(np.tril(np.ones((tm, tm), np.float32), -1), BF16)
    tok = lambda w: pl.BlockSpec((None, tm, w), lambda bi, i: (bi, i, 0))
    per_b = pl.BlockSpec((None, 1, d), lambda bi, i: (bi, 0, 0))
    full = lambda a: pl.BlockSpec(a.shape, lambda bi, i: (0,) * a.ndim)
    return pl.pallas_call(
        _outproj_router_kernel,
        out_shape=(jax.ShapeDtypeStruct((b, s, d), F32), jax.ShapeDtypeStruct((b, s, d), F32),
                   jax.ShapeDtypeStruct((CODE_ROWS, b * s), I32),
                   jax.ShapeDtypeStruct((b, s, LANES), F32), jax.ShapeDtypeStruct((1, LANES), F32)),
        grid=(b, s // tm),
        in_specs=[pl.BlockSpec((None, A_WIDTH, tm), lambda bi, i: (bi, 0, i)),
                  tok(B_WIDTH), tok(C_WIDTH), tok(d), per_b, full(w_out),
                  per_b, per_b, full(g), full(w_router), full(b_router), full(tri)],
        out_specs=(tok(d), tok(d),
                   pl.BlockSpec((CODE_ROWS, tm), lambda bi, i: (0, bi * (s // tm) + i)),
                   tok(LANES), pl.BlockSpec((1, LANES), lambda bi, i: (0, 0))),
        scratch_shapes=[pltpu.VMEM((1, LANES), F32)],
        compiler_params=_cparams(("arbitrary", "arbitrary")),
        name="outproj_router",
    )(ya, yb, yc, h, g1, w_out, sh, sc, g, w_router, b_router, tri)


MOE_BLOCK = 512
PERM_W = 2 * LANES


ROWS_TILE = 2048


def _rows_kernel(start_ref, code_ref, row_ref):
    code = code_ref[...]
    expert = code >> RANK_BITS
    base = jnp.zeros(code.shape, I32)
    for e in range(N_EXPERTS):
        base = jnp.where(expert == e, start_ref[e], base)
    row_ref[...] = base + (code & ((1 << RANK_BITS) - 1))


def _rows_call(starts, code):
    r, n = code.shape
    tile = min(ROWS_TILE, n)
    return pl.pallas_call(
        _rows_kernel,
        out_shape=jax.ShapeDtypeStruct((r, n), I32),
        grid_spec=pltpu.PrefetchScalarGridSpec(
            num_scalar_prefetch=1, grid=(n // tile,),
            in_specs=[pl.BlockSpec((r, tile), lambda i, *_: (0, i))],
            out_specs=pl.BlockSpec((r, tile), lambda i, *_: (0, i))),
        compiler_params=_cparams(("arbitrary",)),
        name="moe_rows",
    )(starts, code)


def _drain_rows(src_row, dst_row, sem, n):
    def body(t, c):
        pltpu.make_async_copy(src_row, dst_row, sem).wait()
        return c
    lax.fori_loop(0, n, body, 0, unroll=4)


ZERO_ROWS = MOE_BLOCK // 2


def _dispatch_kernel(row_ref, padlo_ref, padhi_ref, x_ref, xs_ref, zbuf, sem, zsem, *, tm):
    tok0 = pl.program_id(0) * tm
    n_tok = row_ref.shape[0] // TOP_K

    @pl.when(pl.program_id(0) == 0)
    def _():
        zbuf[...] = jnp.zeros(zbuf.shape, F32)

        def pad_copies(e, fn):
            lo = padlo_ref[e]
            hi = padhi_ref[e]
            lo8 = jnp.minimum((lo + SUBLANES - 1) // SUBLANES * SUBLANES, hi)
            for j in range(SUBLANES - 1):

                @pl.when(lo + j < lo8)
                def _(j=j):
                    fn(pltpu.make_async_copy(zbuf.at[pl.ds(0, 1)], xs_ref.at[pl.ds(lo + j, 1)], zsem))
            n8 = hi - lo8
            size = ZERO_ROWS
            while size >= SUBLANES:
                off = pl.multiple_of(lo8 + (n8 & ~(2 * size - 1)), SUBLANES)

                @pl.when((n8 & size) != 0)
                def _(size=size, off=off):
                    fn(pltpu.make_async_copy(zbuf.at[pl.ds(0, size)], xs_ref.at[pl.ds(off, size)],
                                             zsem))
                size //= 2

        def tail_copies(fn):
            tail_lo = padhi_ref[N_EXPERTS - 1]

            def piece(j, c):
                off = pl.multiple_of(tail_lo + j * ZERO_ROWS, ZERO_ROWS)

                @pl.when(off < xs_ref.shape[0])
                def _():
                    fn(pltpu.make_async_copy(zbuf, xs_ref.at[pl.ds(off, ZERO_ROWS)], zsem))
                return c

            lax.fori_loop(0, N_EXPERTS * MOE_BLOCK // ZERO_ROWS, piece, 0)

        def start_e(e, c):
            pad_copies(e, lambda cp: cp.start())
            return c

        def wait_e(e, c):
            pad_copies(e, lambda cp: cp.wait())
            return c

        lax.fori_loop(0, N_EXPERTS, start_e, 0)
        tail_copies(lambda cp: cp.start())
        lax.fori_loop(0, N_EXPERTS, wait_e, 0)
        tail_copies(lambda cp: cp.wait())

    def issue(t8, c):
        base = pl.multiple_of(t8 * SUBLANES, SUBLANES)
        for j in range(SUBLANES):
            for k in range(TOP_K):
                dst = row_ref[k * n_tok + tok0 + base + j]
                pltpu.make_async_copy(x_ref.at[pl.ds(base + j, 1)], xs_ref.at[pl.ds(dst, 1)],
                                      sem).start(priority=k % 2)
        return c

    lax.fori_loop(0, tm // SUBLANES, issue, 0)
    _drain_rows(x_ref.at[pl.ds(0, 1)], xs_ref.at[pl.ds(0, 1)], sem, tm * TOP_K)


def _dispatch_call(row_ids, pad_lo, pad_hi, x, rows, tm=512):
    n, d = x.shape
    return pl.pallas_call(
        functools.partial(_dispatch_kernel, tm=tm),
        out_shape=jax.ShapeDtypeStruct((rows, d), F32),
        grid_spec=pltpu.PrefetchScalarGridSpec(
            num_scalar_prefetch=3, grid=(n // tm,),
            in_specs=[pl.BlockSpec((tm, d), lambda i, *_: (i, 0))],
            out_specs=pl.BlockSpec(memory_space=pl.ANY),
            scratch_shapes=[pltpu.VMEM((ZERO_ROWS, d), F32), pltpu.SemaphoreType.DMA,
                            pltpu.SemaphoreType.DMA]),
        compiler_params=_cparams(("arbitrary",)),
        name="moe_dispatch",
    )(row_ids, pad_lo, pad_hi, x)


def _ffn_kernel(blk_e_ref, nb_ref, xs_ref, w1_ref, b1g_ref, b1l_ref, w2_ref, b2_ref, perm_ref,
                ys_ref, w1g_s, w1l_s, w2_s):
    i = pl.program_id(0)
    e = blk_e_ref[i]
    live = i < nb_ref[0]
    fresh = (i == 0) | (e != blk_e_ref[jnp.maximum(i - 1, 0)])

    @pl.when(live & fresh)
    def _():
        perm = perm_ref[...]
        for j in range(w1_ref.shape[1] // PERM_W):
            t = _mm(w1_ref[:, j * PERM_W:(j + 1) * PERM_W].astype(BF16), perm).astype(BF16)
            w1g_s[:, j * LANES:(j + 1) * LANES] = t[:, 0:LANES]
            w1l_s[:, j * LANES:(j + 1) * LANES] = t[:, LANES:PERM_W]
        w2_s[...] = w2_ref[...].astype(BF16)

    @pl.when(live)
    def _():
        x = xs_ref[...].astype(BF16)
        glu = jnp.minimum(_mm(x, w1g_s[...]) + b1g_ref[...], SWIGLU_LIMIT)
        lin = jnp.clip(_mm(x, w1l_s[...]) + b1l_ref[...], -SWIGLU_LIMIT, SWIGLU_LIMIT)
        act = glu * jax.nn.sigmoid(SWIGLU_ALPHA * glu) * (lin + 1.0)
        ys_ref[...] = _mm(act.astype(BF16), w2_s[...]) + b2_ref[...]

    @pl.when(jnp.logical_not(live))
    def _():
        ys_ref[...] = jnp.zeros(ys_ref.shape, F32)


def _ffn_call(blk_e, nb_used, xs, w1, b1g, b1l, w2, b2, layer):
    rows, d = xs.shape
    _, ne, _, ff2 = w1.shape
    ff = ff2 // 2
    perm = np.zeros((PERM_W, PERM_W), np.float32)
    perm[2 * np.arange(LANES), np.arange(LANES)] = 1.0
    perm[2 * np.arange(LANES) + 1, LANES + np.arange(LANES)] = 1.0
    ex = lambda r, c: pl.BlockSpec((None, r, c), lambda i, be, nb: (be[i], 0, 0))
    exl = lambda r, c: pl.BlockSpec((None, None, r, c), lambda i, be, nb: (layer, be[i], 0, 0))
    return pl.pallas_call(
        _ffn_kernel,
        out_shape=jax.ShapeDtypeStruct((rows, d), F32),
        grid_spec=pltpu.PrefetchScalarGridSpec(
            num_scalar_prefetch=2, grid=(rows // MOE_BLOCK,),
            in_specs=[pl.BlockSpec((MOE_BLOCK, d),
                                   lambda i, be, nb: (jnp.minimum(i, jnp.maximum(nb[0] - 1, 0)), 0)),
                      exl(d, ff2), ex(1, ff), ex(1, ff), exl(ff, d), ex(1, d),
                      pl.BlockSpec((PERM_W, PERM_W), lambda i, *_: (0, 0))],
            out_specs=pl.BlockSpec((MOE_BLOCK, d), lambda i, *_: (i, 0)),
            scratch_shapes=[pltpu.VMEM((d, ff), BF16), pltpu.VMEM((d, ff), BF16),
                            pltpu.VMEM((ff, d), BF16)]),
        compiler_params=_cparams(("arbitrary",)),
        name="moe_ffn",
    )(blk_e, nb_used, xs, w1, b1g, b1l, w2, b2, jnp.asarray(perm, BF16))


def _combine_kernel(row_ref, gk_ref, h_ref, g2_ref, fg_ref, ys_ref, o_ref, buf, sem,
                    *, tm, final_norm):
    tok0 = pl.program_id(0) * tm
    n_tok = row_ref.shape[0] // TOP_K

    def issue(t8, c):
        base = pl.multiple_of(t8 * SUBLANES, SUBLANES)
        for j in range(SUBLANES):
            for k in range(TOP_K):
                src = row_ref[k * n_tok + tok0 + base + j]
                pltpu.make_async_copy(ys_ref.at[pl.ds(src, 1)], buf.at[k, pl.ds(base + j, 1)],
                                      sem).start(priority=k % 2)
        return c

    lax.fori_loop(0, tm // SUBLANES, issue, 0)
    _drain_rows(ys_ref.at[pl.ds(0, 1)], buf.at[0, pl.ds(0, 1)], sem, tm * TOP_K)

    gk = gk_ref[...]
    acc = buf[0] * gk[:, 0:1]
    for k in range(1, TOP_K):
        acc = acc + buf[k] * gk[:, k:k + 1]
    out = h_ref[...] + g2_ref[...] * acc
    if final_norm:
        out = out * lax.rsqrt(jnp.mean(out * out, axis=-1, keepdims=True) + EPS) * fg_ref[...]
    o_ref[...] = out


def _combine_call(row_ids, gk, h1, g2, final_g, ys, seq, final_norm, tm=256):
    n, d = h1.shape
    return pl.pallas_call(
        functools.partial(_combine_kernel, tm=tm, final_norm=final_norm),
        out_shape=jax.ShapeDtypeStruct((n, d), F32),
        grid_spec=pltpu.PrefetchScalarGridSpec(
            num_scalar_prefetch=1, grid=(n // tm,),
            in_specs=[pl.BlockSpec((tm, LANES), lambda i, *_: (i, 0)),
                      pl.BlockSpec((tm, d), lambda i, *_: (i, 0)),
                      pl.BlockSpec((None, 1, d), lambda i, *_: ((i * tm) // seq, 0, 0)),
                      pl.BlockSpec((1, d), lambda i, *_: (0, 0)),
                      pl.BlockSpec(memory_space=pl.ANY)],
            out_specs=pl.BlockSpec((tm, d), lambda i, *_: (i, 0)),
            scratch_shapes=[pltpu.VMEM((TOP_K, tm, d), F32), pltpu.SemaphoreType.DMA]),
        compiler_params=_cparams(("arbitrary",)),
        name="moe_combine",
    )(row_ids, gk, h1, g2, final_g, ys)


def _moe_call(u, code, gk, counts, h1, g2, w1, b1, w2, b2, final_g, layer, final_norm):
    b, s, d = h1.shape
    n = b * s
    ne = w1.shape[1]
    rows = n * TOP_K + ne * MOE_BLOCK
    cnt = counts[0, :ne].astype(I32)
    padded = (cnt + MOE_BLOCK - 1) // MOE_BLOCK * MOE_BLOCK
    ends = jnp.cumsum(padded)
    starts = (ends - padded).astype(I32)
    blk_start = jnp.arange(rows // MOE_BLOCK, dtype=I32) * MOE_BLOCK
    blk_e = jnp.minimum(jnp.sum((ends[None, :] <= blk_start[:, None]).astype(I32), axis=1), ne - 1)
    nb_used = (ends[-1:] // MOE_BLOCK).astype(I32)

    row_ids = _rows_call(starts, code)[0:TOP_K].reshape(TOP_K * n)

    xs = _dispatch_call(row_ids, starts + cnt, ends.astype(I32), u.reshape(n, d), rows)
    ff = w1.shape[3] // 2
    ys = _ffn_call(blk_e, nb_used, xs, w1, b1[:, 0::2].reshape(ne, 1, ff),
                   b1[:, 1::2].reshape(ne, 1, ff), w2, b2.reshape(ne, 1, d), layer)
    out = _combine_call(row_ids, gk.reshape(n, LANES), h1.reshape(n, d), g2, final_g, ys, s,
                        final_norm)
    return out.reshape(b, s, d)


def _pack_w_in(w_in):
    d = w_in.shape[0]
    offs = np.cumsum((0,) + IN_SPLITS)
    head = w_in[:, :offs[5]]
    wi = w_in[:, offs[5]:offs[6]]
    pad = jnp.zeros((d, KIWI_W - IDX_DIM - IDX_HEADS), w_in.dtype)
    return jnp.concatenate([head, wi, pad, w_in[:, offs[6]:]], axis=1).astype(BF16)


def _block_diag(pool_w):
    g, c, _ = pool_w.shape
    out = jnp.zeros((g * c, g * c), pool_w.dtype)
    for j in range(g):
        out = out.at[j * c:(j + 1) * c, j * c:(j + 1) * c].set(pool_w[j])
    return out.astype(BF16)


def kernel(x, c, positions, w_ada, b_ada, norm1_g, norm2_g, w_in, w_out, pool_w, pool_scale,
           hgrn_norm_g, lb_logits, w_router, b_router, w1, b1, w2, b2, final_g):
    bsz, s, d = x.shape
    depth = w_ada.shape[0]
    mod = _ada_call(c, w_ada, b_ada)
    cos, sin = _rope_table_call(positions)
    h = x
    for l in range(depth):
        sh1, sc1, g1, sh2, sc2, g2 = [mod[l, :, j * d:(j + 1) * d].reshape(bsz, 1, d)
                                      for j in range(6)]
        qt, qit, wit, k, ki, vt, rest = _inproj_call(
            h, sh1, sc1, norm1_g[l].reshape(1, d), _pack_w_in(w_in[l]), cos, sin)
        ya = _dsa_call(qt, qit, wit, k, ki, vt)
        yb = _pool_call(rest, _block_diag(pool_w[l]), pool_scale[l].reshape(1, B_WIDTH))
        yc = _hgrn_call(rest, lb_logits, hgrn_norm_g[l], l)
        wr = jnp.pad(w_router[l], ((0, 0), (0, LANES - N_EXPERTS)))
        br = jnp.pad(b_router[l], (0, LANES - N_EXPERTS)).reshape(1, LANES)
        h1, u2, code, gk, counts = _outproj_router_call(
            ya, yb, yc, h, g1, w_out[l].astype(BF16), sh2, sc2, norm2_g[l].reshape(1, d), wr, br)
        h = _moe_call(u2, code, gk, counts, h1, g2, w1, b1[l], w2, b2[l],
                      final_g.reshape(1, d), l, final_norm=(l == depth - 1))
    return h
```

```python
import functools

import numpy as np
import jax
import jax.numpy as jnp
from jax import lax
from jax.experimental import pallas as pl
from jax.experimental.pallas import tpu as pltpu

F32 = jnp.float32
BF16 = jnp.bfloat16
I32 = jnp.int32

CHUNK = 64
EPS = 1e-6
NEG_BIG = -1e30
A_HEADS, A_KV_HEADS, HEAD_DIM = 8, 2, 64
IDX_HEADS, IDX_DIM, IDX_TOPK_MAX = 4, 64, 256
Q_BLOCK = 128
ROPE_THETA = 10000.0
A_WIDTH = A_HEADS * HEAD_DIM
POOL_WINDOWS = (2, 4, 8, 16)
POOL_GROUP = 64
B_WIDTH = len(POOL_WINDOWS) * POOL_GROUP
C_HEADS, C_KDIM, C_VDIM = 4, 64, 64
C_WIDTH = C_HEADS * C_VDIM
N_EXPERTS, TOP_K = 32, 4
SWIGLU_LIMIT, SWIGLU_ALPHA = 7.0, 1.702
KV_WIDTH = A_KV_HEADS * HEAD_DIM
IN_SPLITS = (A_WIDTH, KV_WIDTH, KV_WIDTH, IDX_HEADS * IDX_DIM, IDX_DIM, IDX_HEADS,
             B_WIDTH, C_HEADS * C_KDIM, C_HEADS * C_KDIM, C_WIDTH, C_WIDTH)

LANES = 128
SUBLANES = 8
INT_MIN = -(2 ** 31)
VMEM_LIMIT = 56 * 1024 * 1024

KIWI_W = LANES
REST_W = B_WIDTH + 4 * C_WIDTH
PACK_W = A_WIDTH + 2 * KV_WIDTH + IDX_HEADS * IDX_DIM + KIWI_W + REST_W
HGRN_LEVELS = (32, 16, 8, 4, 2, 1)


def _nt(a, b):
    return lax.dot_general(a, b, (((1,), (1,)), ((), ())), preferred_element_type=F32)


def _tn(a, b):
    return lax.dot_general(a, b, (((0,), (0,)), ((), ())), preferred_element_type=F32)


def _mm(a, b):
    return jnp.dot(a, b, preferred_element_type=F32)


def _cparams(sem):
    return pltpu.CompilerParams(dimension_semantics=sem, vmem_limit_bytes=VMEM_LIMIT)


def _ada_kernel(c_ref, w_ref, b_ref, o_ref):
    c = c_ref[...]
    ca = c * jax.nn.sigmoid(c)
    o_ref[...] = jnp.dot(ca, w_ref[...], precision=lax.Precision.HIGHEST,
                         preferred_element_type=F32) + b_ref[...]


def _ada_call(c, w_ada, b_ada):
    depth, d, d6 = w_ada.shape
    b = c.shape[0]
    nblk = d6 // d
    return pl.pallas_call(
        _ada_kernel,
        out_shape=jax.ShapeDtypeStruct((depth, b, d6), F32),
        grid=(depth, nblk),
        in_specs=[pl.BlockSpec((b, d), lambda l, j: (0, 0)),
                  pl.BlockSpec((None, d, d), lambda l, j: (l, 0, j)),
                  pl.BlockSpec((None, 1, d), lambda l, j: (l, 0, j))],
        out_specs=pl.BlockSpec((None, b, d), lambda l, j: (l, 0, j)),
        compiler_params=_cparams(("arbitrary", "arbitrary")),
        name="ada_mod",
    )(c, w_ada, b_ada.reshape(depth, 1, d6))


def _rope_table_kernel(pos_ref, inv_ref, sign_ref, cos_ref, sin_ref):
    ang = pos_ref[...].astype(F32) * inv_ref[...]
    cos_ref[...] = jnp.cos(ang)
    sin_ref[...] = jnp.sin(ang) * sign_ref[...]


def _rope_table_call(positions, ts=512):
    b, s = positions.shape
    half = HEAD_DIM // 2
    inv = jnp.power(jnp.float32(ROPE_THETA), -jnp.arange(0, HEAD_DIM, 2, dtype=F32) / HEAD_DIM)
    inv128 = jnp.tile(inv, LANES // half).reshape(1, LANES)
    sign128 = jnp.tile(jnp.concatenate([-jnp.ones((half,), F32), jnp.ones((half,), F32)]),
                       LANES // HEAD_DIM).reshape(1, LANES)
    spec = pl.BlockSpec((None, ts, LANES), lambda bi, i: (bi, i, 0))
    return pl.pallas_call(
        _rope_table_kernel,
        out_shape=(jax.ShapeDtypeStruct((b, s, LANES), F32),) * 2,
        grid=(b, s // ts),
        in_specs=[pl.BlockSpec((None, ts, 1), lambda bi, i: (bi, i, 0)),
                  pl.BlockSpec((1, LANES), lambda bi, i: (0, 0)),
                  pl.BlockSpec((1, LANES), lambda bi, i: (0, 0))],
        out_specs=(spec, spec),
        compiler_params=_cparams(("arbitrary", "arbitrary")),
        name="rope_table",
    )(positions.reshape(b, s, 1), inv128, sign128)


def _rope_tile(x, cos, sin_signed, first_half):
    partner = jnp.where(first_half, pltpu.roll(x, LANES - HEAD_DIM // 2, 1),
                        pltpu.roll(x, HEAD_DIM // 2, 1))
    return x * cos + partner * sin_signed


def _ada_norm(x, g, sc, sh):
    y = x * lax.rsqrt(jnp.mean(x * x, axis=-1, keepdims=True) + EPS)
    return (y * g) * (1.0 + sc) + sh


VT_W = 256
INPROJ_TILE = 512
WI_ROWS = SUBLANES


def _inproj_kernel(h_ref, sh_ref, sc_ref, g_ref, w_ref, cos_ref, sin_ref,
                   qt_ref, qit_ref, wit_ref, k_ref, ki_ref, vt_ref, rest_ref):
    u = _ada_norm(h_ref[...], g_ref[...], sc_ref[...], sh_ref[...]).astype(BF16)
    cos = cos_ref[...]
    sin = sin_ref[...]
    lane = lax.broadcasted_iota(I32, cos.shape, 1)
    first_half = (lane % HEAD_DIM) < (HEAD_DIM // 2)
    rope = lambda x: _rope_tile(x, cos, sin, first_half)

    off = 0
    q_scale = HEAD_DIM ** -0.5
    for j in range(A_WIDTH // LANES):
        z = _mm(u, w_ref[:, off + j * LANES: off + (j + 1) * LANES])
        qt_ref[j * LANES:(j + 1) * LANES, :] = (rope(z) * q_scale).T.astype(BF16)
    off += A_WIDTH
    k_ref[...] = rope(_mm(u, w_ref[:, off:off + KV_WIDTH])).astype(BF16)
    off += KV_WIDTH
    vz = _mm(u, w_ref[:, off:off + KV_WIDTH])
    for j in range(vt_ref.shape[0]):
        vt_ref[j] = vz[j * VT_W:(j + 1) * VT_W].T.astype(BF16)
    off += KV_WIDTH
    for j in range(IDX_HEADS * IDX_DIM // LANES):
        z = _mm(u, w_ref[:, off + j * LANES: off + (j + 1) * LANES])
        qit_ref[j * LANES:(j + 1) * LANES, :] = rope(z).T.astype(BF16)
    off += IDX_HEADS * IDX_DIM
    z = _mm(u, w_ref[:, off:off + KIWI_W])
    ki_ref[...] = rope(z).astype(BF16)
    wi_scale = (IDX_HEADS * IDX_DIM) ** -0.5
    wit_ref[...] = (z * wi_scale).T[IDX_DIM:IDX_DIM + WI_ROWS, :]
    off += KIWI_W
    rest_ref[...] = _mm(u, w_ref[:, off:off + REST_W])


def _inproj_call(h, sh, sc, g, w_pack, cos, sin):
    b, s, d = h.shape
    tm = min(INPROJ_TILE, s)
    tok = lambda w: pl.BlockSpec((None, tm, w), lambda bi, i: (bi, i, 0))
    tr = lambda r: pl.BlockSpec((None, r, tm), lambda bi, i: (bi, 0, i))
    per_b = pl.BlockSpec((None, 1, d), lambda bi, i: (bi, 0, 0))
    sds = jax.ShapeDtypeStruct
    return pl.pallas_call(
        _inproj_kernel,
        out_shape=(sds((b, A_WIDTH, s), BF16), sds((b, IDX_HEADS * IDX_DIM, s), BF16),
                   sds((b, WI_ROWS, s), F32), sds((b, s, KV_WIDTH), BF16),
                   sds((b, s, KIWI_W), BF16), sds((b, s // VT_W, KV_WIDTH, VT_W), BF16),
                   sds((b, s, REST_W), F32)),
        grid=(b, s // tm),
        in_specs=[tok(d), per_b, per_b,
                  pl.BlockSpec((1, d), lambda bi, i: (0, 0)),
                  pl.BlockSpec((d, PACK_W), lambda bi, i: (0, 0)),
                  tok(LANES), tok(LANES)],
        out_specs=(tr(A_WIDTH), tr(IDX_HEADS * IDX_DIM), tr(WI_ROWS), tok(KV_WIDTH), tok(KIWI_W),
                   pl.BlockSpec((None, tm // VT_W, KV_WIDTH, VT_W), lambda bi, i: (bi, i, 0, 0)),
                   tok(REST_W)),
        compiler_params=_cparams(("arbitrary", "arbitrary")),
        name="inproj",
    )(h, sh, sc, g, w_pack, cos, sin)


CNT_ROWS = 32
CNT_ACCS = 4
SEARCH_HEAD_BITS = 19
ONES_ROWS = 16
TIE_BLOCK = 128
CHUNKS_PER_TRIP = 4


def _dsa_kernel(qt_ref, qit_ref, wit_ref, k_ref, ki_ref, vt_ref, lower_ref, o_ref,
                keys_ref, bias_ref, s_ref, acc_ref, *, kc, topk):
    qb = Q_BLOCK
    i = pl.program_id(1)
    t0 = i * qb
    nkc = (t0 + qb + kc - 1) // kc
    lane = lax.broadcasted_iota(I32, (1, qb), 1)
    limit = t0 + (lane // CHUNK + 1) * CHUNK
    key_off = lax.broadcasted_iota(I32, (kc, qb), 0)
    group = A_HEADS // A_KV_HEADS

    qit = qit_ref[...]
    qi_stack = jnp.concatenate(
        [qit[h * IDX_DIM:(h + 1) * IDX_DIM, :] for h in range(IDX_HEADS)], axis=1)
    wit = wit_ref[...]

    def score_body(c, carry):
        off = pl.multiple_of(c * kc, kc)
        ki = ki_ref[pl.ds(off, kc), 0:IDX_DIM]
        s = jnp.maximum(_mm(ki, qi_stack), 0.0)
        score = s[:, 0:qb] * wit[0:1, :]
        for h in range(1, IDX_HEADS):
            score = score + s[:, h * qb:(h + 1) * qb] * wit[h:h + 1, :]
        score = jnp.where(score == 0.0, 0.0, score)
        bits = lax.bitcast_convert_type(score, I32)
        key = jnp.where(bits < 0, bits ^ jnp.int32(0x7FFFFFFF), bits)
        keys_ref[pl.ds(off, kc), :] = jnp.where(off + key_off < limit, key, jnp.int32(INT_MIN))
        return carry

    n_trips = nkc // CHUNKS_PER_TRIP
    left = nkc - n_trips * CHUNKS_PER_TRIP

    def for_chunks(body, carry):
        def run(c0, chunks, cr):
            for j in range(chunks):
                cr = body(c0 + j, cr)
            return cr
        carry = lax.fori_loop(0, n_trips, lambda t, cr: run(t * CHUNKS_PER_TRIP, CHUNKS_PER_TRIP, cr),
                              carry)
        done = n_trips * CHUNKS_PER_TRIP
        size = CHUNKS_PER_TRIP // 2
        while size >= 1:
            take = (left & size) != 0
            carry = lax.cond(take, lambda cr, d=done, s=size: run(d, s, cr), lambda cr: cr, carry)
            done = done + jnp.where(take, size, 0)
            size //= 2
        return carry

    for_chunks(score_body, jnp.int32(0))

    def count(pred):
        def body(c, accs):
            off = pl.multiple_of(c * kc, kc)
            blk = keys_ref[pl.ds(off, kc), :]
            accs = list(accs)
            for j in range(kc // CNT_ROWS):
                a = accs[j % CNT_ACCS]
                accs[j % CNT_ACCS] = jnp.where(pred(blk[j * CNT_ROWS:(j + 1) * CNT_ROWS]), a + 1.0, a)
            return tuple(accs)
        accs = lax.fori_loop(0, nkc, body,
                             tuple(jnp.zeros((CNT_ROWS, qb), F32) for _ in range(CNT_ACCS)))
        return jnp.sum((accs[0] + accs[1]) + (accs[2] + accs[3]), axis=0, keepdims=True)

    topk_f = float(topk)
    cnt0 = count(lambda kk: kk >= 0)
    cnt1 = count(lambda kk: kk >= 1)
    nonneg = cnt0 >= topk_f
    thr0 = jnp.where(nonneg, 0, INT_MIN).astype(I32)
    cthr0 = jnp.where(nonneg, cnt0, 2.0 * kc * (nkc + 1).astype(F32))
    done0 = jnp.where((nonneg & (cnt1 < topk_f)) | (cthr0 == topk_f) | (limit <= topk), 1.0, 0.0)

    def try_bit(state, b):
        thr, cthr, done = state
        cand = thr + jnp.left_shift(jnp.int32(1), b)
        cnt = count(lambda kk: kk >= cand)
        take = (cnt >= topk_f) & (done < 0.5)
        thr = jnp.where(take, cand, thr)
        cthr = jnp.where(take, cnt, cthr)
        return thr, cthr, jnp.where(cthr == topk_f, 1.0, done)

    state = lax.fori_loop(0, SEARCH_HEAD_BITS, lambda j, st: try_bit(st, 30 - j),
                          (thr0, cthr0, done0))
    steps = 3
    tail_bits = 31 - SEARCH_HEAD_BITS

    def thr_cond(carry):
        g, st = carry
        return (g * steps < tail_bits) & (jnp.min(st[2]) < 0.5)

    def thr_body(carry):
        g, st = carry
        for jj in range(steps):
            st = try_bit(st, tail_bits - 1 - (g * steps + jj))
        return g + 1, st

    _, (thr, _, _) = lax.while_loop(thr_cond, thr_body, (jnp.int32(0), state))

    need = topk_f - count(lambda kk: kk > thr)
    lower = lower_ref[...]
    tie_off = lax.broadcasted_iota(I32, (TIE_BLOCK, qb), 0)

    def bias_body(c, seen):
        offs = [pl.multiple_of(c * kc + j * TIE_BLOCK, TIE_BLOCK) for j in range(kc // TIE_BLOCK)]
        kks = [keys_ref[pl.ds(off, TIE_BLOCK), :] for off in offs]
        ties = [jnp.where(kk == thr, 1.0, 0.0) for kk in kks]
        ranks = [_mm(lower, t.astype(BF16)) for t in ties]
        for off, kk, tie_f, rank in zip(offs, kks, ties, ranks):
            sel = ((kk > thr) | ((kk == thr) & (rank + seen <= need))) & (off + tie_off < limit)
            bias_ref[pl.ds(off, TIE_BLOCK), :] = jnp.where(sel, 0.0, NEG_BIG)
            seen = seen + jnp.sum(tie_f, axis=0, keepdims=True)
        return seen

    for_chunks(bias_body, jnp.zeros((1, qb), F32))

    qt = qt_ref[...]
    q_n = [jnp.concatenate([qt[(n * group + g) * HEAD_DIM:(n * group + g + 1) * HEAD_DIM, :]
                            for g in range(group)], axis=1) for n in range(A_KV_HEADS)]

    per = kc // VT_W

    def max_chunk(c, parts):
        out = list(parts)
        for j in range(per):
            cv = c * per + j
            off = pl.multiple_of(cv * VT_W, VT_W)
            bias = bias_ref[pl.ds(off, VT_W), :]
            for n in range(A_KV_HEADS):
                s = _mm(k_ref[pl.ds(off, VT_W), n * HEAD_DIM:(n + 1) * HEAD_DIM], q_n[n])
                s = jnp.concatenate([s[:, g * qb:(g + 1) * qb] + bias for g in range(group)], axis=1)
                s_ref[cv, n] = s
                out[n] = jnp.maximum(out[n], jnp.max(
                    s.reshape(VT_W // SUBLANES, SUBLANES, group * qb), axis=0))
        return tuple(out)

    parts = for_chunks(max_chunk, tuple(jnp.full((SUBLANES, group * qb), NEG_BIG, F32)
                                        for _ in range(A_KV_HEADS)))
    m_n = [jnp.max(p, axis=0, keepdims=True) for p in parts]

    acc_ref[...] = jnp.zeros(acc_ref.shape, F32)
    ones = jnp.ones((ONES_ROWS, VT_W), BF16)

    def pv_chunk(c, carry):
        for j in range(per):
            cv = c * per + j
            for n in range(A_KV_HEADS):
                pt = jnp.exp(s_ref[cv, n] - m_n[n]).astype(BF16)
                vt = jnp.concatenate([vt_ref[cv, n * HEAD_DIM:(n + 1) * HEAD_DIM, :], ones], axis=0)
                acc_ref[n] += _mm(vt, pt)
        return carry

    for_chunks(pv_chunk, jnp.int32(0))

    for n in range(A_KV_HEADS):
        for g in range(group):
            hh = n * group + g
            cols = slice(g * qb, (g + 1) * qb)
            o_ref[hh * HEAD_DIM:(hh + 1) * HEAD_DIM, :] = (
                acc_ref[n, 0:HEAD_DIM, cols] / acc_ref[n, HEAD_DIM:HEAD_DIM + 1, cols])


def _dsa_call(qt, qit, wit, k, ki, vt, kc=512):
    b, _, s = qt.shape
    kc = min(kc, s)
    assert kc % VT_W == 0 and s % kc == 0
    topk = min(IDX_TOPK_MAX, s // 4)
    group = A_HEADS // A_KV_HEADS
    lower = jnp.asarray(np.tril(np.ones((TIE_BLOCK, TIE_BLOCK), np.float32)), BF16)
    qblk = lambda r: pl.BlockSpec((None, r, Q_BLOCK), lambda bi, i: (bi, 0, i))
    seq = lambda w: pl.BlockSpec((None, s, w), lambda bi, i: (bi, 0, 0))
    return pl.pallas_call(
        functools.partial(_dsa_kernel, kc=kc, topk=topk),
        out_shape=jax.ShapeDtypeStruct((b, A_WIDTH, s), F32),
        grid=(b, s // Q_BLOCK),
        in_specs=[qblk(A_WIDTH), qblk(IDX_HEADS * IDX_DIM), qblk(WI_ROWS),
                  seq(KV_WIDTH), seq(KIWI_W),
                  pl.BlockSpec((None, s // VT_W, KV_WIDTH, VT_W), lambda bi, i: (bi, 0, 0, 0)),
                  pl.BlockSpec((TIE_BLOCK, TIE_BLOCK), lambda bi, i: (0, 0))],
        out_specs=qblk(A_WIDTH),
        scratch_shapes=[pltpu.VMEM((s, Q_BLOCK), I32),
                        pltpu.VMEM((s, Q_BLOCK), F32),
                        pltpu.VMEM((s // VT_W, A_KV_HEADS, VT_W, group * Q_BLOCK), F32),
                        pltpu.VMEM((A_KV_HEADS, HEAD_DIM + ONES_ROWS, group * Q_BLOCK), F32)],
        compiler_params=_cparams(("arbitrary", "arbitrary")),
        name="dsa",
    )(qt, qit, wit, k, ki, vt, lower)


POOL_HALO = 32


def _pool_kernel(u_ref, w_ref, scale_ref, o_ref, x_buf, a_buf, b_buf, *, tm):
    hl = POOL_HALO
    rows = tm + hl
    first = pl.program_id(1) == 0

    @pl.when(first)
    def _():
        x_buf[0:hl, :] = jnp.zeros((hl, B_WIDTH), F32)

    x = u_ref[...]
    x_buf[hl:rows, :] = x
    a_buf[8:rows, :] = x_buf[8:rows, :] + x_buf[7:rows - 1, :]
    b_buf[16:rows, :] = a_buf[16:rows, :] + a_buf[14:rows - 2, :]
    w2 = a_buf[hl:rows, :]
    w4 = b_buf[hl:rows, :]
    a_buf[24:rows, :] = b_buf[24:rows, :] + b_buf[20:rows - 4, :]
    w8 = a_buf[hl:rows, :]
    b_buf[hl:rows, :] = a_buf[hl:rows, :] + a_buf[hl - 8:rows - 8, :]
    w16 = b_buf[hl:rows, :]
    x_buf[0:hl, :] = x_buf[tm:rows, :]

    lane = lax.broadcasted_iota(I32, (tm, B_WIDTH), 1)
    grp = lane // POOL_GROUP
    wsum = jnp.where(grp == 0, w2, jnp.where(grp == 1, w4, jnp.where(grp == 2, w8, w16)))
    win = jnp.where(grp == 0, 2, jnp.where(grp == 1, 4, jnp.where(grp == 2, 8, 16)))
    t = pl.program_id(1) * tm + lax.broadcasted_iota(I32, (tm, B_WIDTH), 0)
    cnt = jnp.minimum(t + 1, win).astype(F32)
    pooled = wsum / cnt - x
    y = _mm(pooled.astype(BF16), w_ref[...])
    o_ref[...] = y * scale_ref[...]


def _pool_call(rest, w_bd, scale, tm=512):
    b, s, _ = rest.shape
    tm = min(tm, s)
    rows = tm + POOL_HALO
    return pl.pallas_call(
        functools.partial(_pool_kernel, tm=tm),
        out_shape=jax.ShapeDtypeStruct((b, s, B_WIDTH), F32),
        grid=(b, s // tm),
        in_specs=[pl.BlockSpec((None, tm, B_WIDTH), lambda bi, i: (bi, i, 0)),
                  pl.BlockSpec((B_WIDTH, B_WIDTH), lambda bi, i: (0, 0)),
                  pl.BlockSpec((1, B_WIDTH), lambda bi, i: (0, 0))],
        out_specs=pl.BlockSpec((None, tm, B_WIDTH), lambda bi, i: (bi, i, 0)),
        scratch_shapes=[pltpu.VMEM((rows, B_WIDTH), F32)] * 3,
        compiler_params=_cparams(("arbitrary", "arbitrary")),
        name="pool",
    )(rest, w_bd, scale)


def _hgrn_consts():
    tril = np.tril(np.ones((CHUNK, CHUNK), np.float32))
    mats = [tril]
    r = np.arange(CHUNK)
    for h in HGRN_LEVELS:
        mats.append(tril[(r // (2 * h)) * (2 * h) + h - 1])
    return np.concatenate(mats, axis=0)


def _split3(x):
    hi = x.astype(BF16)
    r1 = x - hi.astype(F32)
    mid = r1.astype(BF16)
    lo = (r1 - mid.astype(F32)).astype(BF16)
    return hi, mid, lo


def _hgrn_kernel(q_ref, f_ref, i_ref, g_ref, lb_ref, ng_ref, cm_ref, bd_ref, o_ref, state_ref,
                 *, layer, tm):
    @pl.when(pl.program_id(1) == 0)
    def _():
        state_ref[...] = jnp.zeros(state_ref.shape, F32)

    lbl = lb_ref[...]
    e = jnp.exp(lbl - jnp.max(lbl, axis=0, keepdims=True))
    p = e / jnp.sum(e, axis=0, keepdims=True)
    cum = p[0:1]
    for l in range(1, layer + 1):
        cum = cum + p[l:l + 1]
    lb = jnp.clip(cum - p[0:1], 0.0, 1.0)

    cm = cm_ref[...]
    ng = ng_ref[...]
    bd = bd_ref[...]
    bd_f = bd.astype(F32)
    row = lax.broadcasted_iota(I32, (CHUNK, 1), 0)
    tt = lax.broadcasted_iota(I32, (CHUNK, C_WIDTH), 0)
    ss = lax.broadcasted_iota(I32, (CHUNK, C_WIDTH), 1) % CHUNK
    lvl_mask = [tt == ss] + [(tt // (2 * h)) == (ss // (2 * h)) for h in HGRN_LEVELS]
    w = C_WIDTH
    heads = C_WIDTH // C_KDIM

    def expand(x16):
        return jnp.concatenate([x16] * heads, axis=0) * bd

    chunks = range(tm // CHUNK)
    rows = [slice(ci * CHUNK, (ci + 1) * CHUNK) for ci in chunks]
    z = f_ref[...]
    log_f = jnp.log(lb + (1.0 - lb) * jax.nn.sigmoid(z))
    kin = (1.0 - lb) * jax.nn.sigmoid(-z)
    qx = q_ref[...]
    qv = qx * jax.nn.sigmoid(qx)
    vb16 = i_ref[...].astype(BF16)
    hi, mid, lo = _split3(log_f)
    lf3 = jnp.concatenate([hi, mid, lo], axis=1)
    cs = [_mm(cm, lf3[r]) for r in rows]
    cs = [c[:, 0:w] + c[:, w:2 * w] + c[:, 2 * w:3 * w] for c in cs]
    odd = [((row // h) % 2) == 1 for h in HGRN_LEVELS]
    attn, q_dec, upd, s_dec = [], [], [], []
    for ci in chunks:
        r = rows[ci]
        bcum = cs[ci][0:CHUNK]
        b_last = bcum[CHUNK - 1:CHUNK]
        q_c, k_c = qv[r], kin[r]
        q_dec.append((q_c * jnp.exp(bcum)).astype(BF16))
        s_dec.append(jnp.exp(b_last))
        upd.append(_tn(vb16[r], (k_c * jnp.exp(b_last - bcum)).astype(BF16)) * bd_f)
        qs, ks = [q_c.astype(BF16)], [k_c.astype(BF16)]
        for li in range(len(HGRN_LEVELS)):
            ref = cs[ci][(li + 1) * CHUNK:(li + 2) * CHUNK]
            qs.append((q_c * jnp.exp(jnp.where(odd[li], bcum - ref, NEG_BIG))).astype(BF16))
            ks.append((k_c * jnp.exp(jnp.where(odd[li], NEG_BIG, ref - bcum))).astype(BF16))
        a = jnp.zeros((CHUNK, w), F32)
        for mask, ql, kl in zip(lvl_mask, qs, ks):
            a = a + jnp.where(mask, _nt(ql, expand(kl)), 0.0)
        attn.append(a)
    o_intra = [_mm(attn[ci].astype(BF16), expand(vb16[rows[ci]])) for ci in chunks]
    st = state_ref[...]
    outs = []
    for ci in chunks:
        outs.append(_nt(q_dec[ci], st.astype(BF16)) + o_intra[ci])
        st = st * s_dec[ci] + upd[ci]
    state_ref[...] = st
    o = jnp.concatenate(outs, axis=0)
    o2h, o2m, o2l = _split3(o * o)
    ms = (_mm(o2h, bd) + _mm(o2m, bd) + _mm(o2l, bd)) * (1.0 / C_VDIM)
    gx = g_ref[...]
    o_ref[...] = (o * lax.rsqrt(ms + EPS) * ng) * (gx * jax.nn.sigmoid(gx))


def _hgrn_call(rest, lb_logits, norm_g, layer, tm=512):
    b, s, _ = rest.shape
    depth = lb_logits.shape[0]
    cm = jnp.asarray(_hgrn_consts(), BF16)
    head_of = np.arange(C_WIDTH) // C_KDIM
    bd = jnp.asarray(head_of[:, None] == head_of[None, :], BF16)
    col = lambda j: pl.BlockSpec((None, tm, C_WIDTH), lambda bi, i, j=j: (bi, i, j))
    return pl.pallas_call(
        functools.partial(_hgrn_kernel, layer=layer, tm=tm),
        out_shape=jax.ShapeDtypeStruct((b, s, C_WIDTH), F32),
        grid=(b, s // tm),
        in_specs=[col(1), col(2), col(3), col(4),
                  pl.BlockSpec((depth, C_WIDTH), lambda bi, i: (0, 0)),
                  pl.BlockSpec((1, C_WIDTH), lambda bi, i: (0, 0)),
                  pl.BlockSpec(cm.shape, lambda bi, i: (0, 0)),
                  pl.BlockSpec((C_WIDTH, C_WIDTH), lambda bi, i: (0, 0))],
        out_specs=pl.BlockSpec((None, tm, C_WIDTH), lambda bi, i: (bi, i, 0)),
        scratch_shapes=[pltpu.VMEM((C_WIDTH, C_WIDTH), F32)],
        compiler_params=_cparams(("arbitrary", "arbitrary")),
        name="hgrn2",
    )(rest, rest, rest, rest, lb_logits, jnp.tile(norm_g, C_HEADS).reshape(1, C_WIDTH), cm, bd)


RANK_BITS = 20
CODE_ROWS = 8


def _outproj_router_kernel(ya_ref, yb_ref, yc_ref, h_ref, g1_ref, wo_ref, sh_ref, sc_ref, g_ref,
                           wr_ref, br_ref, tri_ref, h1_ref, u_ref, code_ref, gk_ref, cnt_ref,
                           run_ref):
    @pl.when((pl.program_id(0) == 0) & (pl.program_id(1) == 0))
    def _():
        run_ref[...] = jnp.zeros(run_ref.shape, F32)

    y = _tn(ya_ref[...].astype(BF16), wo_ref[0:A_WIDTH, :])
    y = y + _mm(yb_ref[...].astype(BF16), wo_ref[A_WIDTH:A_WIDTH + B_WIDTH, :])
    y = y + _mm(yc_ref[...].astype(BF16), wo_ref[A_WIDTH + B_WIDTH:, :])
    h1 = h_ref[...] + g1_ref[...] * y
    h1_ref[...] = h1
    u = _ada_norm(h1, g_ref[...], sc_ref[...], sh_ref[...])
    u_ref[...] = u

    u_hi = u.astype(BF16)
    u_lo = (u - u_hi.astype(F32)).astype(BF16)
    wr = wr_ref[...]
    wr_hi = wr.astype(BF16)
    wr_lo = (wr - wr_hi.astype(F32)).astype(BF16)
    logits = _mm(u_hi, wr_hi) + (_mm(u_hi, wr_lo) + _mm(u_lo, wr_hi)) + br_ref[...]
    lane = lax.broadcasted_iota(I32, logits.shape, 1).astype(F32)
    work = jnp.where(lane < N_EXPERTS, logits, -jnp.inf)
    picks, firsts, tops = [], [], []
    for k in range(TOP_K):
        m = jnp.max(work, axis=1, keepdims=True)
        first = jnp.min(jnp.where(work == m, lane, float(LANES)), axis=1, keepdims=True)
        pick = lane == first
        work = jnp.where(pick, -jnp.inf, work)
        picks.append(pick)
        firsts.append(first)
        tops.append(m)
    ex = [jnp.exp(m - tops[0]) for m in tops]
    den = ex[0] + ex[1] + ex[2] + ex[3]

    sel = picks[0] | picks[1] | picks[2] | picks[3]
    sel_f = jnp.where(sel, 1.0, 0.0)
    prefix = _mm(tri_ref[...], sel_f.astype(BF16)) + run_ref[...]
    run_ref[...] += jnp.sum(sel_f, axis=0, keepdims=True)
    cnt_ref[...] = run_ref[...]

    code = jnp.zeros(logits.shape, I32)
    gk = jnp.zeros(logits.shape, F32)
    for k in range(TOP_K):
        rank = jnp.sum(jnp.where(picks[k], prefix, 0.0), axis=1, keepdims=True)
        ck = (firsts[k].astype(I32) << RANK_BITS) | rank.astype(I32)
        code = jnp.where(lane == float(k), ck, code)
        gk = jnp.where(lane == float(k), ex[k] / den, gk)
    code_ref[...] = code.T[0:CODE_ROWS, :]
    gk_ref[...] = gk


def _outproj_router_call(ya, yb, yc, h, g1, w_out, sh, sc, g, w_router, b_router, tm=512):
    b, s, d = h.shape
    tm = min(tm, s)
    tri = jnp.asarray(np.tril(np.ones((tm, tm), np.float32), -1), BF16)
    tok = lambda w: pl.BlockSpec((None, tm, w), lambda bi, i: (bi, i, 0))
    per_b = pl.BlockSpec((None, 1, d), lambda bi, i: (bi, 0, 0))
    full = lambda a: pl.BlockSpec(a.shape, lambda bi, i: (0,) * a.ndim)
    return pl.pallas_call(
        _outproj_router_kernel,
        out_shape=(jax.ShapeDtypeStruct((b, s, d), F32), jax.ShapeDtypeStruct((b, s, d), F32),
                   jax.ShapeDtypeStruct((CODE_ROWS, b * s), I32),
                   jax.ShapeDtypeStruct((b, s, LANES), F32), jax.ShapeDtypeStruct((1, LANES), F32)),
        grid=(b, s // tm),
        in_specs=[pl.BlockSpec((None, A_WIDTH, tm), lambda bi, i: (bi, 0, i)),
                  tok(B_WIDTH), tok(C_WIDTH), tok(d), per_b, full(w_out),
                  per_b, per_b, full(g), full(w_router), full(b_router), full(tri)],
        out_specs=(tok(d), tok(d),
                   pl.BlockSpec((CODE_ROWS, tm), lambda bi, i: (0, bi * (s // tm) + i)),
                   tok(LANES), pl.BlockSpec((1, LANES), lambda bi, i: (0, 0))),
        scratch_shapes=[pltpu.VMEM((1, LANES), F32)],
        compiler_params=_cparams(("arbitrary", "arbitrary")),
        name="outproj_router",
    )(ya, yb, yc, h, g1, w_out, sh, sc, g, w_router, b_router, tri)


MOE_BLOCK = 512
PERM_W = 2 * LANES


ROWS_TILE = 2048


def _rows_kernel(start_ref, code_ref, row_ref):
    code = code_ref[...]
    expert = code >> RANK_BITS
    base = jnp.zeros(code.shape, I32)
    for e in range(N_EXPERTS):
        base = jnp.where(expert == e, start_ref[e], base)
    row_ref[...] = base + (code & ((1 << RANK_BITS) - 1))


def _rows_call(starts, code):
    r, n = code.shape
    tile = min(ROWS_TILE, n)
    return pl.pallas_call(
        _rows_kernel,
        out_shape=jax.ShapeDtypeStruct((r, n), I32),
        grid_spec=pltpu.PrefetchScalarGridSpec(
            num_scalar_prefetch=1, grid=(n // tile,),
            in_specs=[pl.BlockSpec((r, tile), lambda i, *_: (0, i))],
            out_specs=pl.BlockSpec((r, tile), lambda i, *_: (0, i))),
        compiler_params=_cparams(("arbitrary",)),
        name="moe_rows",
    )(starts, code)


def _drain_rows(src_row, dst_row, sem, n):
    def body(t, c):
        pltpu.make_async_copy(src_row, dst_row, sem).wait()
        return c
    lax.fori_loop(0, n, body, 0, unroll=4)


ZERO_ROWS = MOE_BLOCK // 2


def _dispatch_kernel(row_ref, padlo_ref, padhi_ref, x_ref, xs_ref, zbuf, sem, zsem, *, tm):
    tok0 = pl.program_id(0) * tm
    n_tok = row_ref.shape[0] // TOP_K

    @pl.when(pl.program_id(0) == 0)
    def _():
        zbuf[...] = jnp.zeros(zbuf.shape, F32)

        def pad_copies(e, fn):
            lo = padlo_ref[e]
            hi = padhi_ref[e]
            lo8 = jnp.minimum((lo + SUBLANES - 1) // SUBLANES * SUBLANES, hi)
            for j in range(SUBLANES - 1):

                @pl.when(lo + j < lo8)
                def _(j=j):
                    fn(pltpu.make_async_copy(zbuf.at[pl.ds(0, 1)], xs_ref.at[pl.ds(lo + j, 1)], zsem))
            n8 = hi - lo8
            size = ZERO_ROWS
            while size >= SUBLANES:
                off = pl.multiple_of(lo8 + (n8 & ~(2 * size - 1)), SUBLANES)

                @pl.when((n8 & size) != 0)
                def _(size=size, off=off):
                    fn(pltpu.make_async_copy(zbuf.at[pl.ds(0, size)], xs_ref.at[pl.ds(off, size)],
                                             zsem))
                size //= 2

        def tail_copies(fn):
            tail_lo = padhi_ref[N_EXPERTS - 1]

            def piece(j, c):
                off = pl.multiple_of(tail_lo + j * ZERO_ROWS, ZERO_ROWS)

                @pl.when(off < xs_ref.shape[0])
                def _():
                    fn(pltpu.make_async_copy(zbuf, xs_ref.at[pl.ds(off, ZERO_ROWS)], zsem))
                return c

            lax.fori_loop(0, N_EXPERTS * MOE_BLOCK // ZERO_ROWS, piece, 0)

        def start_e(e, c):
            pad_copies(e, lambda cp: cp.start())
            return c

        def wait_e(e, c):
            pad_copies(e, lambda cp: cp.wait())
            return c

        lax.fori_loop(0, N_EXPERTS, start_e, 0)
        tail_copies(lambda cp: cp.start())
        lax.fori_loop(0, N_EXPERTS, wait_e, 0)
        tail_copies(lambda cp: cp.wait())

    def issue(t8, c):
        base = pl.multiple_of(t8 * SUBLANES, SUBLANES)
        for j in range(SUBLANES):
            for k in range(TOP_K):
                dst = row_ref[k * n_tok + tok0 + base + j]
                pltpu.make_async_copy(x_ref.at[pl.ds(base + j, 1)], xs_ref.at[pl.ds(dst, 1)],
                                      sem).start(priority=k % 2)
        return c

    lax.fori_loop(0, tm // SUBLANES, issue, 0)
    _drain_rows(x_ref.at[pl.ds(0, 1)], xs_ref.at[pl.ds(0, 1)], sem, tm * TOP_K)


def _dispatch_call(row_ids, pad_lo, pad_hi, x, rows, tm=512):
    n, d = x.shape
    return pl.pallas_call(
        functools.partial(_dispatch_kernel, tm=tm),
        out_shape=jax.ShapeDtypeStruct((rows, d), F32),
        grid_spec=pltpu.PrefetchScalarGridSpec(
            num_scalar_prefetch=3, grid=(n // tm,),
            in_specs=[pl.BlockSpec((tm, d), lambda i, *_: (i, 0))],
            out_specs=pl.BlockSpec(memory_space=pl.ANY),
            scratch_shapes=[pltpu.VMEM((ZERO_ROWS, d), F32), pltpu.SemaphoreType.DMA,
                            pltpu.SemaphoreType.DMA]),
        compiler_params=_cparams(("arbitrary",)),
        name="moe_dispatch",
    )(row_ids, pad_lo, pad_hi, x)


def _ffn_kernel(blk_e_ref, nb_ref, xs_ref, w1_ref, b1g_ref, b1l_ref, w2_ref, b2_ref, perm_ref,
                ys_ref, w1g_s, w1l_s, w2_s):
    i = pl.program_id(0)
    e = blk_e_ref[i]
    live = i < nb_ref[0]
    fresh = (i == 0) | (e != blk_e_ref[jnp.maximum(i - 1, 0)])

    @pl.when(live & fresh)
    def _():
        perm = perm_ref[...]
        for j in range(w1_ref.shape[1] // PERM_W):
            t = _mm(w1_ref[:, j * PERM_W:(j + 1) * PERM_W].astype(BF16), perm).astype(BF16)
            w1g_s[:, j * LANES:(j + 1) * LANES] = t[:, 0:LANES]
            w1l_s[:, j * LANES:(j + 1) * LANES] = t[:, LANES:PERM_W]
        w2_s[...] = w2_ref[...].astype(BF16)

    @pl.when(live)
    def _():
        x = xs_ref[...].astype(BF16)
        glu = jnp.minimum(_mm(x, w1g_s[...]) + b1g_ref[...], SWIGLU_LIMIT)
        lin = jnp.clip(_mm(x, w1l_s[...]) + b1l_ref[...], -SWIGLU_LIMIT, SWIGLU_LIMIT)
        act = glu * jax.nn.sigmoid(SWIGLU_ALPHA * glu) * (lin + 1.0)
        ys_ref[...] = _mm(act.astype(BF16), w2_s[...]) + b2_ref[...]

    @pl.when(jnp.logical_not(live))
    def _():
        ys_ref[...] = jnp.zeros(ys_ref.shape, F32)


def _ffn_call(blk_e, nb_used, xs, w1, b1g, b1l, w2, b2, layer):
    rows, d = xs.shape
    _, ne, _, ff2 = w1.shape
    ff = ff2 // 2
    perm = np.zeros((PERM_W, PERM_W), np.float32)
    perm[2 * np.arange(LANES), np.arange(LANES)] = 1.0
    perm[2 * np.arange(LANES) + 1, LANES + np.arange(LANES)] = 1.0
    ex = lambda r, c: pl.BlockSpec((None, r, c), lambda i, be, nb: (be[i], 0, 0))
    exl = lambda r, c: pl.BlockSpec((None, None, r, c), lambda i, be, nb: (layer, be[i], 0, 0))
    return pl.pallas_call(
        _ffn_kernel,
        out_shape=jax.ShapeDtypeStruct((rows, d), F32),
        grid_spec=pltpu.PrefetchScalarGridSpec(
            num_scalar_prefetch=2, grid=(rows // MOE_BLOCK,),
            in_specs=[pl.BlockSpec((MOE_BLOCK, d),
                                   lambda i, be, nb: (jnp.minimum(i, jnp.maximum(nb[0] - 1, 0)), 0)),
                      exl(d, ff2), ex(1, ff), ex(1, ff), exl(ff, d), ex(1, d),
                      pl.BlockSpec((PERM_W, PERM_W), lambda i, *_: (0, 0))],
            out_specs=pl.BlockSpec((MOE_BLOCK, d), lambda i, *_: (i, 0)),
            scratch_shapes=[pltpu.VMEM((d, ff), BF16), pltpu.VMEM((d, ff), BF16),
                            pltpu.VMEM((ff, d), BF16)]),
        compiler_params=_cparams(("arbitrary",)),
        name="moe_ffn",
    )(blk_e, nb_used, xs, w1, b1g, b1l, w2, b2, jnp.asarray(perm, BF16))


def _combine_kernel(row_ref, gk_ref, h_ref, g2_ref, fg_ref, ys_ref, o_ref, buf, sem,
                    *, tm, final_norm):
    tok0 = pl.program_id(0) * tm
    n_tok = row_ref.shape[0] // TOP_K

    def issue(t8, c):
        base = pl.multiple_of(t8 * SUBLANES, SUBLANES)
        for j in range(SUBLANES):
            for k in range(TOP_K):
                src = row_ref[k * n_tok + tok0 + base + j]
                pltpu.make_async_copy(ys_ref.at[pl.ds(src, 1)], buf.at[k, pl.ds(base + j, 1)],
                                      sem).start(priority=k % 2)
        return c

    lax.fori_loop(0, tm // SUBLANES, issue, 0)
    _drain_rows(ys_ref.at[pl.ds(0, 1)], buf.at[0, pl.ds(0, 1)], sem, tm * TOP_K)

    gk = gk_ref[...]
    acc = buf[0] * gk[:, 0:1]
    for k in range(1, TOP_K):
        acc = acc + buf[k] * gk[:, k:k + 1]
    out = h_ref[...] + g2_ref[...] * acc
    if final_norm:
        out = out * lax.rsqrt(jnp.mean(out * out, axis=-1, keepdims=True) + EPS) * fg_ref[...]
    o_ref[...] = out


def _combine_call(row_ids, gk, h1, g2, final_g, ys, seq, final_norm, tm=256):
    n, d = h1.shape
    return pl.pallas_call(
        functools.partial(_combine_kernel, tm=tm, final_norm=final_norm),
        out_shape=jax.ShapeDtypeStruct((n, d), F32),
        grid_spec=pltpu.PrefetchScalarGridSpec(
            num_scalar_prefetch=1, grid=(n // tm,),
            in_specs=[pl.BlockSpec((tm, LANES), lambda i, *_: (i, 0)),
                      pl.BlockSpec((tm, d), lambda i, *_: (i, 0)),
                      pl.BlockSpec((None, 1, d), lambda i, *_: ((i * tm) // seq, 0, 0)),
                      pl.BlockSpec((1, d), lambda i, *_: (0, 0)),
                      pl.BlockSpec(memory_space=pl.ANY)],
            out_specs=pl.BlockSpec((tm, d), lambda i, *_: (i, 0)),
            scratch_shapes=[pltpu.VMEM((TOP_K, tm, d), F32), pltpu.SemaphoreType.DMA]),
        compiler_params=_cparams(("arbitrary",)),
        name="moe_combine",
    )(row_ids, gk, h1, g2, final_g, ys)


def _moe_call(u, code, gk, counts, h1, g2, w1, b1, w2, b2, final_g, layer, final_norm):
    b, s, d = h1.shape
    n = b * s
    ne = w1.shape[1]
    rows = n * TOP_K + ne * MOE_BLOCK
    cnt = counts[0, :ne].astype(I32)
    padded = (cnt + MOE_BLOCK - 1) // MOE_BLOCK * MOE_BLOCK
    ends = jnp.cumsum(padded)
    starts = (ends - padded).astype(I32)
    blk_start = jnp.arange(rows // MOE_BLOCK, dtype=I32) * MOE_BLOCK
    blk_e = jnp.minimum(jnp.sum((ends[None, :] <= blk_start[:, None]).astype(I32), axis=1), ne - 1)
    nb_used = (ends[-1:] // MOE_BLOCK).astype(I32)

    row_ids = _rows_call(starts, code)[0:TOP_K].reshape(TOP_K * n)

    xs = _dispatch_call(row_ids, starts + cnt, ends.astype(I32), u.reshape(n, d), rows)
    ff = w1.shape[3] // 2
    ys = _ffn_call(blk_e, nb_used, xs, w1, b1[:, 0::2].reshape(ne, 1, ff),
                   b1[:, 1::2].reshape(ne, 1, ff), w2, b2.reshape(ne, 1, d), layer)
    out = _combine_call(row_ids, gk.reshape(n, LANES), h1.reshape(n, d), g2, final_g, ys, s,
                        final_norm)
    return out.reshape(b, s, d)


def _pack_w_in(w_in):
    d = w_in.shape[0]
    offs = np.cumsum((0,) + IN_SPLITS)
    head = w_in[:, :offs[5]]
    wi = w_in[:, offs[5]:offs[6]]
    pad = jnp.zeros((d, KIWI_W - IDX_DIM - IDX_HEADS), w_in.dtype)
    return jnp.concatenate([head, wi, pad, w_in[:, offs[6]:]], axis=1).astype(BF16)


def _block_diag(pool_w):
    g, c, _ = pool_w.shape
    out = jnp.zeros((g * c, g * c), pool_w.dtype)
    for j in range(g):
        out = out.at[j * c:(j + 1) * c, j * c:(j + 1) * c].set(pool_w[j])
    return out.astype(BF16)


def kernel(x, c, positions, w_ada, b_ada, norm1_g, norm2_g, w_in, w_out, pool_w, pool_scale,
           hgrn_norm_g, lb_logits, w_router, b_router, w1, b1, w2, b2, final_g):
    bsz, s, d = x.shape
    depth = w_ada.shape[0]
    mod = _ada_call(c, w_ada, b_ada)
    cos, sin = _rope_table_call(positions)
    h = x
    for l in range(depth):
        sh1, sc1, g1, sh2, sc2, g2 = [mod[l, :, j * d:(j + 1) * d].reshape(bsz, 1, d)
                                      for j in range(6)]
        qt, qit, wit, k, ki, vt, rest = _inproj_call(
            h, sh1, sc1, norm1_g[l].reshape(1, d), _pack_w_in(w_in[l]), cos, sin)
        ya = _dsa_call(qt, qit, wit, k, ki, vt)
        yb = _pool_call(rest, _block_diag(pool_w[l]), pool_scale[l].reshape(1, B_WIDTH))
        yc = _hgrn_call(rest, lb_logits, hgrn_norm_g[l], l)
        wr = jnp.pad(w_router[l], ((0, 0), (0, LANES - N_EXPERTS)))
        br = jnp.pad(b_router[l], (0, LANES - N_EXPERTS)).reshape(1, LANES)
        h1, u2, code, gk, counts = _outproj_router_call(
            ya, yb, yc, h, g1, w_out[l].astype(BF16), sh2, sc2, norm2_g[l].reshape(1, d), wr, br)
        h = _moe_call(u2, code, gk, counts, h1, g2, w1, b1[l], w2, b2[l],
                      final_g.reshape(1, d), l, final_norm=(l == depth - 1))
    return h
```
